```python
import jax
import jax.numpy as jnp
from jax import lax
import numpy as np

D_MODEL = 2048
BATCH = 8
SEQ = 4096
DEPTH = 2

N_MEM = 256
HEAD_DIM = 64
ROPE_DIM = HEAD_DIM // 4
ROPE_THETA = 500000.0
NSA_HEADS = 16
NSA_KV_HEADS = 4
NSA_GROUP = NSA_HEADS // NSA_KV_HEADS
NSA_WIDTH = NSA_HEADS * HEAD_DIM
NSA_KV_WIDTH = NSA_KV_HEADS * HEAD_DIM
CMP_BLOCK = 32
CMP_STRIDE = 16
CMP_HIDDEN = 256
SLC_BLOCK = 64
SLC_TOPN = 8
WINDOW = 512
NSA_Q_BLOCK = 64
SEL_FORCE = 1e9
RW_HEADS = 16
RW_HEAD = 64
RW_WIDTH = RW_HEADS * RW_HEAD
RW_DECAY_LORA = 64
RW_A_LORA = 64
RW_V_LORA = 32
RW_GATE_LORA = 160
RW_LNX_EPS = 64e-5
RW_SHIFT_SIZES = (RW_WIDTH, RW_WIDTH, RW_WIDTH, RW_DECAY_LORA, RW_A_LORA, RW_GATE_LORA)
RW_SHIFT_WIDTH = sum(RW_SHIFT_SIZES)
POOL_WINDOWS = (2, 4, 8, 16)
POOL_GROUP = 256
POOL_WIDTH = POOL_GROUP * len(POOL_WINDOWS)
XA_HEADS = 4
XA_HEAD_DIM = 128
XA_WIDTH = XA_HEADS * XA_HEAD_DIM
D_FF = 5632
N_EXPERTS = 8
TOP_K = 2
D_FF_EXPERT = 7168
MOE_BLOCK = 256
N_DENSE = (DEPTH + 1) // 2
N_MOE = DEPTH // 2
DN_ALPHA = (2 * DEPTH) ** 0.25
DN_BETA = (8 * DEPTH) ** -0.25
LN_EPS = 1e-5
NEG_INF = -1e30
N_BRANCH = 3
IN_SIZES = (NSA_WIDTH,) + (NSA_KV_WIDTH,) * 6 + (N_BRANCH * NSA_HEADS,) + RW_SHIFT_SIZES + (POOL_WIDTH,) + (D_MODEL,) * N_BRANCH
IN_OFFSETS = tuple(int(o) for o in np.cumsum(IN_SIZES)[:-1])
N_IN = sum(IN_SIZES)

kernel_name = 'hybrid_nsa_rwkv7_pool_moe_deepnorm'


def layer_norm(x, g, b):
    xf = x.astype(jnp.float32)
    mu = xf.mean(-1, keepdims=True)
    var = jnp.square(xf - mu).mean(-1, keepdims=True)
    return ((xf - mu) * lax.rsqrt(var + LN_EPS) * g + b).astype(x.dtype)


def rope_partial(x, pos):
    half = ROPE_DIM // 2
    inv_freq = ROPE_THETA ** (-jnp.arange(half, dtype=jnp.float32) / half)
    ang = pos.astype(jnp.float32)[..., None] * inv_freq
    cos = jnp.cos(ang)[:, :, None, :]
    sin = jnp.sin(ang)[:, :, None, :]
    xf = x.astype(jnp.float32)
    x1 = xf[..., :half]
    x2 = xf[..., half:ROPE_DIM]
    out = jnp.concatenate([x1 * cos - x2 * sin, x2 * cos + x1 * sin, xf[..., ROPE_DIM:]], axis=-1)
    return out.astype(x.dtype)


def compress_tokens(kv, pe, w1, b1, w2):
    Bsz, S, nh, dh = kv.shape
    c = kv.reshape(Bsz, S // CMP_STRIDE, CMP_STRIDE, nh, dh)
    blocks = jnp.concatenate([c[:, :-1], c[:, 1:]], axis=2) + pe[:, None, :]
    flat = blocks.transpose(0, 1, 3, 2, 4).reshape(Bsz, -1, nh, CMP_BLOCK * dh)
    return jax.nn.gelu(flat @ w1 + b1) @ w2


def cmp_slc_weights(n_cmp, n_slc):
    c0 = np.arange(n_cmp)[:, None] * CMP_STRIDE
    s0 = np.arange(n_slc)[None, :] * SLC_BLOCK
    shared = np.minimum(c0 + CMP_BLOCK, s0 + SLC_BLOCK) - np.maximum(c0, s0)
    return (np.maximum(shared, 0) / CMP_STRIDE).astype(np.float32)


def nsa_attention(q, k_cmp, v_cmp, k_slc, v_slc, k_win, v_win, gate_logits, pos,
                  cmp_pe, cmp_w1, cmp_b1, cmp_w2):
    f32 = jnp.float32
    Bsz, S, _ = q.shape
    n_cmp = S // CMP_STRIDE - 1
    n_slc = S // SLC_BLOCK
    n_sel = min(SLC_TOPN, n_slc)
    n_qb = S // NSA_Q_BLOCK
    scale = HEAD_DIM ** -0.5

    def heads(t, n):
        return t.reshape(Bsz, S, n, HEAD_DIM)

    q = rope_partial(heads(q, NSA_HEADS), pos)
    k_slc = rope_partial(heads(k_slc, NSA_KV_HEADS), pos)
    k_win = rope_partial(heads(k_win, NSA_KV_HEADS), pos)
    v_slc = heads(v_slc, NSA_KV_HEADS)
    v_win = heads(v_win, NSA_KV_HEADS)

    cmp_end = jnp.arange(n_cmp) * CMP_STRIDE + (CMP_BLOCK - 1)
    kc = compress_tokens(heads(k_cmp, NSA_KV_HEADS), cmp_pe[0], cmp_w1[0], cmp_b1[0], cmp_w2[0])
    kc = rope_partial(kc, pos[:, cmp_end])
    vc = compress_tokens(heads(v_cmp, NSA_KV_HEADS), cmp_pe[1], cmp_w1[1], cmp_b1[1], cmp_w2[1])
    cmp_to_slc = jnp.asarray(cmp_slc_weights(n_cmp, n_slc))

    ks_blk = k_slc.reshape(Bsz, n_slc, SLC_BLOCK, NSA_KV_HEADS, HEAD_DIM).transpose(0, 3, 1, 2, 4)
    vs_blk = v_slc.reshape(Bsz, n_slc, SLC_BLOCK, NSA_KV_HEADS, HEAD_DIM).transpose(0, 3, 1, 2, 4)
    kw_pad = jnp.pad(k_win, ((0, 0), (WINDOW, 0), (0, 0), (0, 0)))
    vw_pad = jnp.pad(v_win, ((0, 0), (WINDOW, 0), (0, 0), (0, 0)))

    g = jax.nn.sigmoid(gate_logits.astype(f32)).reshape(Bsz, S, N_BRANCH, NSA_KV_HEADS, NSA_GROUP)
    q_blocks = q.reshape(Bsz, n_qb, NSA_Q_BLOCK, NSA_KV_HEADS, NSA_GROUP, HEAD_DIM).swapaxes(0, 1)
    g_blocks = g.reshape(Bsz, n_qb, NSA_Q_BLOCK, N_BRANCH, NSA_KV_HEADS, NSA_GROUP).swapaxes(0, 1)
    starts = jnp.arange(n_qb, dtype=jnp.int32) * NSA_Q_BLOCK
    b_ix = jnp.arange(Bsz)[:, None, None, None]
    h_ix = jnp.arange(NSA_KV_HEADS)[None, None, :, None]
    blk = jnp.arange(n_slc)

    def query_block(args):
        qi, gi, start = args
        t = start + jnp.arange(NSA_Q_BLOCK)
        s_c = jnp.einsum('bqhgd,bchd->bqhgc', qi, kc, preferred_element_type=f32) * scale
        mask_c = (cmp_end[None, :] <= t[:, None])[None, :, None, None, :]
        p_c = jax.nn.softmax(jnp.where(mask_c, s_c, NEG_INF), axis=-1) * mask_c
        o_c = jnp.einsum('bqhgc,bchd->bqhgd', p_c.astype(vc.dtype), vc)
        imp = jnp.einsum('bqhgc,cn->bqhn', p_c, cmp_to_slc)
        cur = (t // SLC_BLOCK)[:, None]
        forced = (blk == 0) | (blk == cur) | (blk == cur - 1)
        imp = jnp.where(forced[None, :, None, :], SEL_FORCE,
                        jnp.where((blk <= cur)[None, :, None, :], imp, -SEL_FORCE))
        _, sel = lax.top_k(imp, n_sel)
        n_key = n_sel * SLC_BLOCK
        k_sel = ks_blk[b_ix, h_ix, sel].reshape(Bsz, NSA_Q_BLOCK, NSA_KV_HEADS, n_key, HEAD_DIM)
        v_sel = vs_blk[b_ix, h_ix, sel].reshape(Bsz, NSA_Q_BLOCK, NSA_KV_HEADS, n_key, HEAD_DIM)
        kpos = (sel[..., None] * SLC_BLOCK + jnp.arange(SLC_BLOCK)).reshape(Bsz, NSA_Q_BLOCK, NSA_KV_HEADS, n_key)
        mask_s = (kpos <= t[None, :, None, None])[:, :, :, None, :]
        s_s = jnp.einsum('bqhgd,bqhkd->bqhgk', qi, k_sel, preferred_element_type=f32) * scale
        p_s = jax.nn.softmax(jnp.where(mask_s, s_s, NEG_INF), axis=-1)
        o_s = jnp.einsum('bqhgk,bqhkd->bqhgd', p_s.astype(v_sel.dtype), v_sel)
        kw = lax.dynamic_slice_in_dim(kw_pad, start, NSA_Q_BLOCK + WINDOW, axis=1)
        vw = lax.dynamic_slice_in_dim(vw_pad, start, NSA_Q_BLOCK + WINDOW, axis=1)
        kpos_w = start - WINDOW + jnp.arange(NSA_Q_BLOCK + WINDOW)
        diff = t[:, None] - kpos_w[None, :]
        mask_w = ((diff >= 0) & (diff < WINDOW) & (kpos_w >= 0)[None, :])[None, :, None, None, :]
        s_w = jnp.einsum('bqhgd,bkhd->bqhgk', qi, kw, preferred_element_type=f32) * scale
        p_w = jax.nn.softmax(jnp.where(mask_w, s_w, NEG_INF), axis=-1)
        o_w = jnp.einsum('bqhgk,bkhd->bqhgd', p_w.astype(vw.dtype), vw)
        o = (gi[:, :, 0, :, :, None] * o_c + gi[:, :, 1, :, :, None] * o_s
             + gi[:, :, 2, :, :, None] * o_w)
        return o.astype(qi.dtype)

    out = lax.map(query_block, (q_blocks, g_blocks, starts))
    return out.swapaxes(0, 1).reshape(Bsz, S, NSA_WIDTH)


def token_shift(u):
    return jnp.pad(u, ((0, 0), (1, 0), (0, 0)))[:, :-1]


def wkv7_scan(r, w, k, v, a, b):
    Bsz, S, H, N = r.shape

    def step(state, inp):
        r_t, w_t, k_t, v_t, a_t, b_t = inp
        sa = jnp.einsum('bhvk,bhk->bhv', state, a_t)
        state = (state * w_t[:, :, None, :] + sa[..., None] * b_t[:, :, None, :]
                 + v_t[..., None] * k_t[:, :, None, :])
        return state, jnp.einsum('bhvk,bhk->bhv', state, r_t)

    xs = tuple(jnp.moveaxis(t, 1, 0) for t in (r, w, k, v, a, b))
    init = jnp.zeros((Bsz, H, N, N), jnp.float32)
    _, ys = lax.scan(step, init, xs)
    return jnp.moveaxis(ys, 0, 1)


def rwkv7_time_mix(r, k, v, wl, al, gl, mu, w0, w2, a0, a2, g2, k_k, k_a, r_k, lnx_g, lnx_b,
                   v_first, vres):
    f32 = jnp.float32
    Bsz, S, _ = r.shape
    mus = jnp.split(mu, [int(o) for o in np.cumsum(RW_SHIFT_SIZES)[:-1]])
    r, k, v, wl, al, gl = [t + (token_shift(t) - t) * m for t, m in zip((r, k, v, wl, al, gl), mus)]
    w = -jax.nn.softplus(-(w0 + jnp.tanh(wl) @ w2).astype(f32)) - 0.5
    decay = jnp.exp(-jnp.exp(w))
    a = jax.nn.sigmoid((a0 + al @ a2).astype(f32))
    g = jax.nn.sigmoid(gl) @ g2
    if vres is not None:
        v0, v1, v2 = vres
        v = v + (v_first - v) * jax.nn.sigmoid(v0 + (v @ v1) @ v2)

    def hs(t):
        return t.reshape(Bsz, S, RW_HEADS, RW_HEAD)

    kk = hs((k * k_k).astype(f32))
    kk = kk / jnp.maximum(jnp.sqrt(jnp.sum(kk * kk, axis=-1, keepdims=True)), 1e-12)
    k_mod = k.astype(f32) * (1.0 + (a - 1.0) * k_a)
    rh, kh, vh, ah = hs(r.astype(f32)), hs(k_mod), hs(v.astype(f32)), hs(a)
    y = wkv7_scan(rh, hs(decay), kh, vh, -kk, kk * ah)
    ym = y.mean(-1, keepdims=True)
    yv = jnp.square(y - ym).mean(-1, keepdims=True)
    y = ((y - ym) * lax.rsqrt(yv + RW_LNX_EPS)).reshape(Bsz, S, RW_WIDTH) * lnx_g + lnx_b
    bonus = jnp.sum(rh * kh * r_k, axis=-1, keepdims=True) * vh
    y = (y + bonus.reshape(Bsz, S, RW_WIDTH)) * g
    return y.astype(r.dtype), v


def multiscale_pool(u, pool_w, pool_scale):
    Bsz, S, C = u.shape
    uf = u.astype(jnp.float32)
    cs = jnp.pad(jnp.cumsum(uf, axis=1), ((0, 0), (1, 0), (0, 0)))
    count = jnp.arange(1, S + 1, dtype=jnp.float32)[None, :, None]
    outs = []
    for gi, win in enumerate(POOL_WINDOWS):
        sl = slice(gi * POOL_GROUP, (gi + 1) * POOL_GROUP)
        c = cs[..., sl]
        lo = jnp.pad(c, ((0, 0), (win, 0), (0, 0)))[:, 1:S + 1]
        mean = (c[:, 1:] - lo) / jnp.minimum(count, float(win))
        outs.append(mean - uf[..., sl])
    y = jnp.stack(outs, axis=2)
    y = jnp.einsum('bsgc,gcd->bsgd', y, pool_w.astype(jnp.float32))
    return (y.reshape(Bsz, S, C) * pool_scale).astype(u.dtype)


def memory_cross_attention(h, mem_k, mem_v, wq, wo):
    Bsz, S, _ = h.shape
    q = (h @ wq).reshape(Bsz, S, XA_HEADS, XA_HEAD_DIM)
    s = jnp.einsum('bshd,bmhd->bhsm', q, mem_k, preferred_element_type=jnp.float32) * XA_HEAD_DIM ** -0.5
    p = jax.nn.softmax(s, axis=-1)
    o = jnp.einsum('bhsm,bmhd->bshd', p.astype(mem_v.dtype), mem_v).reshape(Bsz, S, XA_WIDTH)
    return o @ wo


def swiglu(h, wg, wu, wd):
    return (jax.nn.silu(h @ wg) * (h @ wu)) @ wd


def moe_swiglu(h, router, w_gate, w_up, w_down):
    Bsz, S, D = h.shape
    xt = h.reshape(-1, D)
    n_tok = xt.shape[0]
    n_assign = n_tok * TOP_K
    logits = jnp.matmul(xt, router, preferred_element_type=jnp.float32)
    top_val, top_idx = lax.top_k(logits, TOP_K)
    gate = jax.nn.softmax(top_val, axis=-1).reshape(-1)
    expert = top_idx.reshape(-1)
    token = jnp.arange(n_assign) // TOP_K
    order = jnp.argsort(expert, stable=True)
    e_sorted = expert[order]
    counts = jnp.bincount(expert, length=N_EXPERTS)
    starts = jnp.cumsum(counts) - counts
    padded = (counts + MOE_BLOCK - 1) // MOE_BLOCK * MOE_BLOCK
    pad_end = jnp.cumsum(padded)
    pad_start = pad_end - padded
    dest = pad_start[e_sorted] + jnp.arange(n_assign) - starts[e_sorted]
    n_pad = n_assign + N_EXPERTS * MOE_BLOCK
    n_blocks = n_pad // MOE_BLOCK
    tok_pad = jnp.full((n_pad,), n_tok, jnp.int32).at[dest].set(token[order].astype(jnp.int32))
    gate_pad = jnp.zeros((n_pad,), jnp.float32).at[dest].set(gate[order])
    block_expert = jnp.minimum(
        jnp.searchsorted(pad_end, jnp.arange(n_blocks) * MOE_BLOCK, side='right'), N_EXPERTS - 1)
    x_ext = jnp.concatenate([xt, jnp.zeros((1, D), xt.dtype)], axis=0)

    def expert_block(args):
        tok, e = args
        xb = x_ext[tok]
        return (jax.nn.silu(xb @ w_gate[e]) * (xb @ w_up[e])) @ w_down[e]

    y = lax.map(expert_block, (tok_pad.reshape(n_blocks, MOE_BLOCK), block_expert))
    y = (y.reshape(n_pad, D) * gate_pad[:, None]).astype(h.dtype)
    out = jnp.zeros((n_tok + 1, D), h.dtype).at[tok_pad].add(y)[:n_tok]
    return out.reshape(Bsz, S, D)


def setup_inputs(seed: int = 0) -> dict:
    key = jax.random.key(seed)
    keys = jax.random.split(key, 64)
    counter = [0]
    f32 = jnp.float32

    def nk():
        counter[0] += 1
        return keys[counter[0] - 1]

    def nrm(shape, scale):
        return jax.random.normal(nk(), shape, f32) * scale

    D = D_MODEL
    inp = {}
    inp['x'] = nrm((BATCH, SEQ, D), 1.0)
    inp['mem'] = nrm((BATCH, N_MEM, D), 1.0)
    offset = jax.random.randint(nk(), (BATCH, 1), 0, 1024, dtype=jnp.int32)
    inp['positions'] = jnp.arange(SEQ, dtype=jnp.int32)[None, :] + offset
    inp['w_in'] = nrm((DEPTH, D, N_IN), D ** -0.5)
    inp['nsa_cmp_pe'] = nrm((DEPTH, 2, CMP_BLOCK, HEAD_DIM), 0.02)
    inp['nsa_cmp_w1'] = nrm((DEPTH, 2, CMP_BLOCK * HEAD_DIM, CMP_HIDDEN), (CMP_BLOCK * HEAD_DIM) ** -0.5)
    inp['nsa_cmp_b1'] = nrm((DEPTH, 2, CMP_HIDDEN), 0.02)
    inp['nsa_cmp_w2'] = nrm((DEPTH, 2, CMP_HIDDEN, HEAD_DIM), CMP_HIDDEN ** -0.5)
    inp['rwkv_mu'] = jax.random.uniform(nk(), (DEPTH, RW_SHIFT_WIDTH), f32, 0.0, 1.0)
    inp['rwkv_w0'] = nrm((DEPTH, RW_WIDTH), 1.0) - 1.0
    inp['rwkv_w2'] = nrm((DEPTH, RW_DECAY_LORA, RW_WIDTH), 0.1)
    inp['rwkv_a0'] = nrm((DEPTH, RW_WIDTH), 0.5)
    inp['rwkv_a2'] = nrm((DEPTH, RW_A_LORA, RW_WIDTH), 0.1)
    inp['rwkv_g2'] = nrm((DEPTH, RW_GATE_LORA, RW_WIDTH), RW_GATE_LORA ** -0.5)
    inp['rwkv_k_k'] = 0.85 + nrm((DEPTH, RW_WIDTH), 0.02)
    inp['rwkv_k_a'] = 1.0 + nrm((DEPTH, RW_WIDTH), 0.02)
    inp['rwkv_r_k'] = nrm((DEPTH, RW_HEADS, RW_HEAD), 0.1)
    inp['rwkv_lnx_g'] = 1.0 + nrm((DEPTH, RW_WIDTH), 0.02)
    inp['rwkv_lnx_b'] = nrm((DEPTH, RW_WIDTH), 0.02)
    inp['rwkv_v0'] = nrm((DEPTH - 1, RW_WIDTH), 0.5)
    inp['rwkv_v1'] = nrm((DEPTH - 1, RW_WIDTH, RW_V_LORA), RW_WIDTH ** -0.5)
    inp['rwkv_v2'] = nrm((DEPTH - 1, RW_V_LORA, RW_WIDTH), 0.1)
    inp['pool_w'] = nrm((DEPTH, len(POOL_WINDOWS), POOL_GROUP, POOL_GROUP), POOL_GROUP ** -0.5)
    inp['pool_scale'] = 1.0 + nrm((DEPTH, POOL_WIDTH), 0.02)
    inp['w_branch'] = nrm((DEPTH, N_BRANCH, NSA_WIDTH, D), NSA_WIDTH ** -0.5)
    inp['w_out'] = nrm((DEPTH, D, D), D ** -0.5 * DN_BETA)
    inp['mem_ln_g'] = 1.0 + nrm((D,), 0.02)
    inp['mem_ln_b'] = nrm((D,), 0.02)
    inp['mem_wkv'] = nrm((D, 2 * XA_WIDTH), D ** -0.5)
    inp['xa_wq'] = nrm((DEPTH, D, XA_WIDTH), D ** -0.5)
    inp['xa_wo'] = nrm((DEPTH, XA_WIDTH, D), XA_WIDTH ** -0.5 * DN_BETA)
    inp['ln_g'] = 1.0 + nrm((DEPTH, 3, D), 0.02)
    inp['ln_b'] = nrm((DEPTH, 3, D), 0.02)
    inp['ffn_w_gate'] = nrm((N_DENSE, D, D_FF), D ** -0.5)
    inp['ffn_w_up'] = nrm((N_DENSE, D, D_FF), D ** -0.5)
    inp['ffn_w_down'] = nrm((N_DENSE, D_FF, D), D_FF ** -0.5 * DN_BETA)
    inp['moe_router'] = nrm((N_MOE, D, N_EXPERTS), D ** -0.5)
    inp['moe_w_gate'] = nrm((N_MOE, N_EXPERTS, D, D_FF_EXPERT), D ** -0.5)
    inp['moe_w_up'] = nrm((N_MOE, N_EXPERTS, D, D_FF_EXPERT), D ** -0.5)
    inp['moe_w_down'] = nrm((N_MOE, N_EXPERTS, D_FF_EXPERT, D), D_FF_EXPERT ** -0.5 * DN_BETA)
    return inp


def reference(x, mem, positions, w_in, nsa_cmp_pe, nsa_cmp_w1, nsa_cmp_b1, nsa_cmp_w2,
              rwkv_mu, rwkv_w0, rwkv_w2, rwkv_a0, rwkv_a2, rwkv_g2, rwkv_k_k, rwkv_k_a, rwkv_r_k,
              rwkv_lnx_g, rwkv_lnx_b, rwkv_v0, rwkv_v1, rwkv_v2, pool_w, pool_scale, w_branch, w_out,
              mem_ln_g, mem_ln_b, mem_wkv, xa_wq, xa_wo, ln_g, ln_b,
              ffn_w_gate, ffn_w_up, ffn_w_down, moe_router, moe_w_gate, moe_w_up, moe_w_down):
    Bsz = x.shape[0]
    m = layer_norm(mem, mem_ln_g, mem_ln_b)
    mem_k, mem_v = jnp.split(m @ mem_wkv, 2, axis=-1)
    mem_k = mem_k.reshape(Bsz, N_MEM, XA_HEADS, XA_HEAD_DIM)
    mem_v = mem_v.reshape(Bsz, N_MEM, XA_HEADS, XA_HEAD_DIM)

    h = x
    v_first = None
    for l in range(DEPTH):
        (q, k_c, v_c, k_s, v_s, k_w, v_w, nsa_g, r, k, v, wl, al, gl, pool_in,
         gate_a, gate_b, gate_c) = jnp.split(h @ w_in[l], IN_OFFSETS, axis=-1)
        y_a = nsa_attention(q, k_c, v_c, k_s, v_s, k_w, v_w, nsa_g, positions,
                            nsa_cmp_pe[l], nsa_cmp_w1[l], nsa_cmp_b1[l], nsa_cmp_w2[l])
        vres = None if l == 0 else (rwkv_v0[l - 1], rwkv_v1[l - 1], rwkv_v2[l - 1])
        y_b, v_l = rwkv7_time_mix(r, k, v, wl, al, gl, rwkv_mu[l], rwkv_w0[l], rwkv_w2[l],
                                  rwkv_a0[l], rwkv_a2[l], rwkv_g2[l], rwkv_k_k[l], rwkv_k_a[l],
                                  rwkv_r_k[l], rwkv_lnx_g[l], rwkv_lnx_b[l], v_first, vres)
        if l == 0:
            v_first = v_l
        y_c = multiscale_pool(pool_in, pool_w[l], pool_scale[l])
        merged = (jax.nn.sigmoid(gate_a) * (y_a @ w_branch[l, 0])
                  + jax.nn.sigmoid(gate_b) * (y_b @ w_branch[l, 1])
                  + jax.nn.sigmoid(gate_c) * (y_c @ w_branch[l, 2]))
        h = layer_norm(DN_ALPHA * h + merged @ w_out[l], ln_g[l, 0], ln_b[l, 0])
        h = layer_norm(DN_ALPHA * h + memory_cross_attention(h, mem_k, mem_v, xa_wq[l], xa_wo[l]),
                       ln_g[l, 1], ln_b[l, 1])
        if l % 2 == 0:
            f = swiglu(h, ffn_w_gate[l // 2], ffn_w_up[l // 2], ffn_w_down[l // 2])
        else:
            f = moe_swiglu(h, moe_router[l // 2], moe_w_gate[l // 2], moe_w_up[l // 2], moe_w_down[l // 2])
        h = layer_norm(DN_ALPHA * h + f, ln_g[l, 2], ln_b[l, 2])
    return h
```

```python
import functools

import numpy as np
import jax
import jax.numpy as jnp
from jax import lax
from jax.experimental import pallas as pl
from jax.experimental.pallas import tpu as pltpu

F32 = jnp.float32
BF16 = jnp.bfloat16
I32 = jnp.int32

HEAD_DIM = 64
ROPE_DIM = HEAD_DIM // 4
ROPE_HALF = ROPE_DIM // 2
ROPE_THETA = 500000.0
NSA_HEADS = 16
NSA_KV_HEADS = 4
NSA_GROUP = NSA_HEADS // NSA_KV_HEADS
NSA_WIDTH = NSA_HEADS * HEAD_DIM
NSA_KV_WIDTH = NSA_KV_HEADS * HEAD_DIM
CMP_BLOCK = 32
CMP_STRIDE = 16
CMP_HIDDEN = 256
SLC_BLOCK = 64
SLC_TOPN = 8
WINDOW = 512
SEL_FORCE = 1e9
RW_HEADS = 16
RW_HEAD = 64
RW_WIDTH = RW_HEADS * RW_HEAD
RW_DECAY_LORA = 64
RW_A_LORA = 64
RW_V_LORA = 32
RW_GATE_LORA = 160
RW_LNX_EPS = 64e-5
POOL_WINDOWS = (2, 4, 8, 16)
POOL_GROUP = 256
POOL_WIDTH = POOL_GROUP * len(POOL_WINDOWS)
XA_HEADS = 4
XA_HEAD_DIM = 128
XA_WIDTH = XA_HEADS * XA_HEAD_DIM
N_EXPERTS = 8
TOP_K = 2
N_BRANCH = 3
LN_EPS = 1e-5
NEG_INF = -1e30
VERY_NEG = -3e38
MASK_BIG = 2.0 ** 100

LANES = 128
VMEM_LIMIT = 56 * 1024 * 1024

NP_COLS = 13312
OFF_GATES = 0
OFF_Q = 6144
OFF_R = 7168
OFF_K = 8192
OFF_V = 9216
OFF_POOL = 10240
OFF_KVC = 11264
OFF_KVS = 11776
OFF_KVW = 12288
OFF_SMALL = 12800
SMALL_W = 512
SM_WL = 128
SM_AL = 192
SM_GL = 256

NSA_TQ = 128
RW_CHUNK = 64
MOE_TM = 512


def _cp(sem, vmem=VMEM_LIMIT):
    return pltpu.CompilerParams(dimension_semantics=sem, vmem_limit_bytes=vmem)


def _dot(a, b):
    return jnp.dot(a, b, preferred_element_type=F32)


def _dot_nt(a, b):
    return lax.dot_general(a, b, (((1,), (1,)), ((), ())), preferred_element_type=F32)


def _layer_norm_rows(y, g, b):
    mu = jnp.mean(y, axis=-1, keepdims=True)
    d = y - mu
    var = jnp.mean(d * d, axis=-1, keepdims=True)
    return d * lax.rsqrt(var + LN_EPS) * g + b


def _mm_kernel(x_ref, w_ref, o_ref):
    o_ref[...] = _dot(x_ref[...], w_ref[...]).astype(o_ref.dtype)


def _matmul(x, w, *, tm, tn, out_dtype):
    m, k = x.shape
    n = w.shape[1]
    return pl.pallas_call(
        _mm_kernel,
        grid=(n // tn, m // tm),
        in_specs=[pl.BlockSpec((tm, k), lambda j, i: (i, 0)),
                  pl.BlockSpec((k, tn), lambda j, i: (0, j))],
        out_specs=pl.BlockSpec((tm, tn), lambda j, i: (i, j)),
        out_shape=jax.ShapeDtypeStruct((m, n), out_dtype),
        compiler_params=_cp(("parallel", "parallel")),
        name="matmul",
    )(x, w)


def _mm_ln_kernel(x_ref, w_ref, res_ref, g_ref, b_ref, o32_ref, o16_ref, *, alpha):
    y = alpha * res_ref[...] + _dot(x_ref[...], w_ref[...])
    out = _layer_norm_rows(y, g_ref[...], b_ref[...])
    o32_ref[...] = out
    o16_ref[...] = out.astype(BF16)


def _matmul_res_ln(x, w, res, g, b, *, alpha, tm=256):
    m, k = x.shape
    n = w.shape[1]
    row = lambda i: (i, 0)
    fix = lambda i: (0, 0)
    return pl.pallas_call(
        functools.partial(_mm_ln_kernel, alpha=alpha),
        grid=(m // tm,),
        in_specs=[pl.BlockSpec((tm, k), row), pl.BlockSpec((k, n), fix), pl.BlockSpec((tm, n), row),
                  pl.BlockSpec((1, n), fix), pl.BlockSpec((1, n), fix)],
        out_specs=[pl.BlockSpec((tm, n), row), pl.BlockSpec((tm, n), row)],
        out_shape=[jax.ShapeDtypeStruct((m, n), F32), jax.ShapeDtypeStruct((m, n), BF16)],
        compiler_params=_cp(("parallel",)),
        name="matmul_res_ln",
    )(x, w, res, g.reshape(1, n), b.reshape(1, n))


def _memkv_kernel(m_ref, g_ref, b_ref, w_ref, o_ref):
    xn = _layer_norm_rows(m_ref[...], g_ref[...], b_ref[...])
    o_ref[...] = _dot(xn.astype(BF16), w_ref[...]).astype(o_ref.dtype)


def _memkv(mem2d, g, b, w, *, tm=256):
    m, d = mem2d.shape
    n = w.shape[1]
    row = lambda i: (i, 0)
    fix = lambda i: (0, 0)
    return pl.pallas_call(
        _memkv_kernel,
        grid=(m // tm,),
        in_specs=[pl.BlockSpec((tm, d), row), pl.BlockSpec((1, d), fix), pl.BlockSpec((1, d), fix),
                  pl.BlockSpec((d, n), fix)],
        out_specs=pl.BlockSpec((tm, n), row),
        out_shape=jax.ShapeDtypeStruct((m, n), BF16),
        compiler_params=_cp(("parallel",)),
        name="memkv",
    )(mem2d, g.reshape(1, d), b.reshape(1, d), w)


def _rope_table_kernel(pos_ref, invf_ref, c_ref, shi_ref, slo_ref):
    ang = pos_ref[...].astype(F32) * invf_ref[...]
    d = lax.broadcasted_iota(I32, ang.shape, 1) % HEAD_DIM
    cs = jnp.cos(ang)
    sn = jnp.sin(ang)
    c_ref[...] = jnp.where(d < ROPE_DIM, cs, 1.0)
    shi_ref[...] = jnp.where((d >= ROPE_HALF) & (d < ROPE_DIM), sn, 0.0)
    slo_ref[...] = jnp.where(d < ROPE_HALF, -sn, 0.0)


def _rope_tables(pos_col, invf_lane, *, ts):
    n = pos_col.shape[0]
    row = lambda i: (i, 0)
    out = jax.ShapeDtypeStruct((n, LANES), F32)
    return pl.pallas_call(
        _rope_table_kernel,
        grid=(n // ts,),
        in_specs=[pl.BlockSpec((ts, 1), row), pl.BlockSpec((1, LANES), lambda i: (0, 0))],
        out_specs=[pl.BlockSpec((ts, LANES), row)] * 3,
        out_shape=[out, out, out],
        compiler_params=_cp(("parallel",)),
        name="rope_tables",
    )(pos_col, invf_lane)


def _rope128(x, c, shi, slo):
    return x * c + pltpu.roll(x, ROPE_HALF, 1) * shi + pltpu.roll(x, LANES - ROPE_HALF, 1) * slo


def _nsa_prep_kernel(q_ref, ks_ref, kw_ref, sm_ref, c_ref, shi_ref, slo_ref,
                     qo_ref, ksa_ref, kwo_ref, g_ref):
    c = c_ref[...]
    shi = shi_ref[...]
    slo = slo_ref[...]
    ts = c.shape[0]
    scale = HEAD_DIM ** -0.5
    for j in range(NSA_WIDTH // LANES):
        sl = slice(LANES * j, LANES * (j + 1))
        qo_ref[:, sl] = (_rope128(q_ref[:, sl].astype(F32), c, shi, slo) * scale).astype(BF16)
    t = pl.program_id(1) * ts + lax.broadcasted_iota(I32, (ts, LANES), 0)
    lane = lax.broadcasted_iota(I32, (ts, LANES), 1)
    onehot = jnp.where(lane - HEAD_DIM == t // SLC_BLOCK, 1.0, 0.0)
    lo = lane < HEAD_DIM
    for j in range(NSA_KV_WIDTH // LANES):
        sl = slice(LANES * j, LANES * (j + 1))
        ks = _rope128(ks_ref[:, sl].astype(F32), c, shi, slo)
        kwo_ref[:, sl] = _rope128(kw_ref[:, sl].astype(F32), c, shi, slo).astype(BF16)
        ksa_ref[2 * j] = jnp.where(lo, ks, onehot).astype(BF16)
        ksa_ref[2 * j + 1] = jnp.where(lo, pltpu.roll(ks, HEAD_DIM, 1), onehot).astype(BF16)
    g_ref[...] = jax.nn.sigmoid(sm_ref[:, 0:LANES].astype(F32))


def _nsa_prep(proj3, c, shi, slo, *, ts=512):
    b, s, _ = proj3.shape
    cb = lambda w, off: off // w
    tab = pl.BlockSpec((None, ts, LANES), lambda bi, i: (bi, i, 0))
    return pl.pallas_call(
        _nsa_prep_kernel,
        grid=(b, s // ts),
        in_specs=[
            pl.BlockSpec((None, ts, NSA_WIDTH), lambda bi, i: (bi, i, cb(NSA_WIDTH, OFF_Q))),
            pl.BlockSpec((None, ts, NSA_KV_WIDTH), lambda bi, i: (bi, i, cb(NSA_KV_WIDTH, OFF_KVS))),
            pl.BlockSpec((None, ts, NSA_KV_WIDTH), lambda bi, i: (bi, i, cb(NSA_KV_WIDTH, OFF_KVW))),
            pl.BlockSpec((None, ts, SMALL_W), lambda bi, i: (bi, i, cb(SMALL_W, OFF_SMALL))),
            tab, tab, tab],
        out_specs=[
            pl.BlockSpec((None, ts, NSA_WIDTH), lambda bi, i: (bi, i, 0)),
            pl.BlockSpec((None, NSA_KV_HEADS, ts, LANES), lambda bi, i: (bi, 0, i, 0)),
            pl.BlockSpec((None, ts, NSA_KV_WIDTH), lambda bi, i: (bi, i, 0)),
            pl.BlockSpec((None, ts, LANES), lambda bi, i: (bi, i, 0))],
        out_shape=[jax.ShapeDtypeStruct((b, s, NSA_WIDTH), BF16),
                   jax.ShapeDtypeStruct((b, NSA_KV_HEADS, s, LANES), BF16),
                   jax.ShapeDtypeStruct((b, s, NSA_KV_WIDTH), BF16),
                   jax.ShapeDtypeStruct((b, s, LANES), F32)],
        compiler_params=_cp(("parallel", "parallel")),
        name="nsa_prep",
    )(proj3, proj3, proj3, proj3, c, shi, slo)


def _gelu_tanh(x):
    return 0.5 * x * (1.0 + jnp.tanh(0.7978845608028654 * (x + 0.044715 * x * x * x)))


def _cmp_kernel(x_ref, pe_ref, w1_ref, b1_ref, w2_ref, c_ref, shi_ref, slo_ref, o_ref, *, rope):
    x = x_ref[...].astype(F32)
    n = x.shape[0]
    half = x.shape[1]
    xa = (x + pe_ref[0:1, :]).astype(BF16)
    xb = (x + pe_ref[1:2, :]).astype(BF16)
    ha = _dot(xa, w1_ref[0:half, :])
    hb = _dot(xb, w1_ref[half:2 * half, :])
    hid = ha + pltpu.roll(hb, n - 1, 0) + b1_ref[...]
    out = _dot(_gelu_tanh(hid).astype(BF16), w2_ref[...])
    if rope:
        out = _rope128(out, c_ref[...], shi_ref[...], slo_ref[...])
    o_ref[...] = out[:, 0:HEAD_DIM].astype(BF16)


def _compress(xu, pe, w1, b1, w2p, c, shi, slo, *, rope):
    b, nh, n, w = xu.shape
    fix2 = lambda bi, h: (0, 0)
    tab = pl.BlockSpec((None, n, LANES), lambda bi, h: (bi, 0, 0))
    return pl.pallas_call(
        functools.partial(_cmp_kernel, rope=rope),
        grid=(b, nh),
        in_specs=[pl.BlockSpec((None, None, n, w), lambda bi, h: (bi, h, 0, 0)),
                  pl.BlockSpec((2, w), fix2), pl.BlockSpec((2 * w, CMP_HIDDEN), fix2),
                  pl.BlockSpec((1, CMP_HIDDEN), fix2), pl.BlockSpec((CMP_HIDDEN, LANES), fix2),
                  tab, tab, tab],
        out_specs=pl.BlockSpec((None, None, n, HEAD_DIM), lambda bi, h: (bi, h, 0, 0)),
        out_shape=jax.ShapeDtypeStruct((b, nh, n, HEAD_DIM), BF16),
        compiler_params=_cp(("parallel", "parallel")),
        name="nsa_compress",
    )(xu, pe, w1, b1, w2p, c, shi, slo)


def _nsa_kernel(q_ref, g_ref, kc_ref, vc_ref, m_ref, ksa_ref, vs_ref, kw_ref, vw_ref, o_ref):
    tq = q_ref.shape[0]
    rows = NSA_GROUP * tq
    n_cmp = kc_ref.shape[1]
    n_slc = m_ref.shape[1]
    i = pl.program_id(1)
    t0 = i * tq
    tok = t0 + lax.broadcasted_iota(I32, (tq, 1), 0)
    tok_r = jnp.concatenate([tok] * NSA_GROUP, axis=0)
    loc_r = tok_r - t0
    cmp_end = lax.broadcasted_iota(I32, (rows, n_cmp), 1) * CMP_STRIDE + (CMP_BLOCK - 1)
    cmp_mask = cmp_end <= tok_r
    blk = lax.broadcasted_iota(I32, (tq, n_slc), 1)
    blk_f = blk.astype(F32)
    cur = tok // SLC_BLOCK
    forced = (blk == 0) | (blk == cur) | (blk == cur - 1)
    allowed = blk <= cur
    col_q = lax.broadcasted_iota(I32, (rows, tq), 1)
    diag_mask = col_q <= loc_r
    n_win = WINDOW + tq
    ws = pl.multiple_of(jnp.maximum(t0 - WINDOW, 0), tq)
    dwin = tok_r - (ws + lax.broadcasted_iota(I32, (rows, n_win), 1))
    win_mask = (dwin >= 0) & (dwin < WINDOW)
    cmp_to_slc = m_ref[...]

    for h in range(NSA_KV_HEADS):
        qh = jnp.concatenate(
            [q_ref[:, HEAD_DIM * (NSA_GROUP * h + g):HEAD_DIM * (NSA_GROUP * h + g + 1)]
             for g in range(NSA_GROUP)], axis=0)
        s = jnp.where(cmp_mask, _dot_nt(qh, kc_ref[h]), NEG_INF)
        mx = jnp.max(s, axis=1, keepdims=True)
        e = jnp.where(cmp_mask, jnp.exp(s - mx), 0.0)
        den = jnp.sum(e, axis=1, keepdims=True)
        p = e / jnp.maximum(den, 1e-30)
        o_c = _dot(p.astype(BF16), vc_ref[h])
        p4 = p[0:tq]
        for g in range(1, NSA_GROUP):
            p4 = p4 + p[g * tq:(g + 1) * tq]
        p_hi = p4.astype(BF16)
        p_lo = (p4 - p_hi.astype(F32)).astype(BF16)
        imp = _dot(p_hi, cmp_to_slc) + _dot(p_lo, cmp_to_slc)
        imp = jnp.where(forced, SEL_FORCE, jnp.where(allowed, imp, -SEL_FORCE))
        sel = jnp.zeros((tq, n_slc), F32)
        for _ in range(SLC_TOPN):
            best = jnp.max(imp, axis=1, keepdims=True)
            idx = jnp.min(jnp.where(imp == best, blk_f, float(n_slc)), axis=1, keepdims=True)
            hit = blk_f == idx
            sel = jnp.where(hit, 1.0, sel)
            imp = jnp.where(hit, VERY_NEG, imp)
        bias = jnp.where(sel > 0.5, 0.0, -MASK_BIG).astype(BF16)
        q_aug = jnp.concatenate([qh, jnp.concatenate([bias] * NSA_GROUP, axis=0)], axis=1)
        def sel_step(j, carry, diag):
            m_i, l_i, acc = carry
            k0 = pl.multiple_of(j * tq, tq)
            sc = _dot_nt(q_aug, ksa_ref[h, pl.ds(k0, tq), :])
            if diag:
                sc = jnp.where(diag_mask, sc, NEG_INF)
            m_new = jnp.maximum(m_i, jnp.max(sc, axis=1, keepdims=True))
            alpha = jnp.exp(m_i - m_new)
            pp = jnp.exp(sc - m_new)
            l_new = alpha * l_i + jnp.sum(pp, axis=1, keepdims=True)
            vt = vs_ref[pl.ds(k0, tq), HEAD_DIM * h:HEAD_DIM * (h + 1)]
            return m_new, l_new, alpha * acc + _dot(pp.astype(BF16), vt)

        carry = (jnp.full((rows, 1), VERY_NEG, F32), jnp.zeros((rows, 1), F32),
                 jnp.zeros((rows, HEAD_DIM), F32))
        carry = lax.fori_loop(0, i, lambda j, cr: sel_step(j, cr, False), carry)
        _, l_s, acc_s = sel_step(i, carry, True)
        o_s = acc_s / l_s
        kwt = kw_ref[pl.ds(ws, n_win), HEAD_DIM * h:HEAD_DIM * (h + 1)]
        vwt = vw_ref[pl.ds(ws, n_win), HEAD_DIM * h:HEAD_DIM * (h + 1)]
        sw = jnp.where(win_mask, _dot_nt(qh, kwt), NEG_INF)
        ew = jnp.exp(sw - jnp.max(sw, axis=1, keepdims=True))
        o_w = _dot(ew.astype(BF16), vwt) / jnp.sum(ew, axis=1, keepdims=True)
        for g in range(NSA_GROUP):
            head = NSA_GROUP * h + g
            rs = slice(g * tq, (g + 1) * tq)
            o = (g_ref[:, head:head + 1] * o_c[rs]
                 + g_ref[:, NSA_HEADS + head:NSA_HEADS + head + 1] * o_s[rs]
                 + g_ref[:, 2 * NSA_HEADS + head:2 * NSA_HEADS + head + 1] * o_w[rs])
            o_ref[:, HEAD_DIM * head:HEAD_DIM * (head + 1)] = o.astype(BF16)


def _cmp_slc_weights(n_cmp_pad, n_cmp, n_slc_pad):
    c0 = np.arange(n_cmp_pad)[:, None] * CMP_STRIDE
    s0 = np.arange(n_slc_pad)[None, :] * SLC_BLOCK
    shared = np.minimum(c0 + CMP_BLOCK, s0 + SLC_BLOCK) - np.maximum(c0, s0)
    w = np.maximum(shared, 0) / CMP_STRIDE
    w[n_cmp:] = 0.0
    return w.astype(np.float32)


def _nsa_attention(q_r, gates, kc, vc, ksa, kw_r, proj3):
    b, s, _ = q_r.shape
    tq = NSA_TQ
    n_cmp_pad = kc.shape[2]
    n_slc = HEAD_DIM
    assert SLC_TOPN <= s // SLC_BLOCK <= n_slc and s >= WINDOW + tq and s % tq == 0
    m = jnp.asarray(_cmp_slc_weights(n_cmp_pad, s // CMP_STRIDE - 1, n_slc), BF16)
    per_b3 = lambda bi, i: (bi, 0, 0)
    per_b4 = lambda bi, i: (bi, 0, 0, 0)
    cbv = lambda off: (off + NSA_KV_WIDTH) // NSA_KV_WIDTH
    return pl.pallas_call(
        _nsa_kernel,
        grid=(b, s // tq),
        in_specs=[
            pl.BlockSpec((None, tq, NSA_WIDTH), lambda bi, i: (bi, i, 0)),
            pl.BlockSpec((None, tq, LANES), lambda bi, i: (bi, i, 0)),
            pl.BlockSpec((None, NSA_KV_HEADS, n_cmp_pad, HEAD_DIM), per_b4),
            pl.BlockSpec((None, NSA_KV_HEADS, n_cmp_pad, HEAD_DIM), per_b4),
            pl.BlockSpec((n_cmp_pad, n_slc), lambda bi, i: (0, 0)),
            pl.BlockSpec((None, NSA_KV_HEADS, s, LANES), per_b4),
            pl.BlockSpec((None, s, NSA_KV_WIDTH), lambda bi, i: (bi, 0, cbv(OFF_KVS))),
            pl.BlockSpec((None, s, NSA_KV_WIDTH), per_b3),
            pl.BlockSpec((None, s, NSA_KV_WIDTH), lambda bi, i: (bi, 0, cbv(OFF_KVW)))],
        out_specs=pl.BlockSpec((None, tq, NSA_WIDTH), lambda bi, i: (bi, i, 0)),
        out_shape=jax.ShapeDtypeStruct((b, s, NSA_WIDTH), BF16),
        compiler_params=_cp(("parallel", "arbitrary")),
        name="nsa_attention",
    )(q_r, gates, kc, vc, m, ksa, proj3, kw_r, proj3)


def _softplus(x):
    return jnp.maximum(x, 0.0) + jnp.log(1.0 + jnp.exp(-jnp.abs(x)))


def _split3(x):
    h1 = x.astype(BF16)
    r1 = x - h1.astype(F32)
    h2 = r1.astype(BF16)
    h3 = (r1 - h2.astype(F32)).astype(BF16)
    return h1, h2, h3


def _seg_sum(x, bd):
    cols = []
    for j in range(x.shape[1] // LANES):
        xs = x[:, LANES * j:LANES * (j + 1)]
        h1, h2, h3 = _split3(xs)
        cols.append(_dot(h1, bd) + _dot(h2, bd) + _dot(h3, bd))
    return jnp.concatenate(cols, axis=1)


def _rwkv_kernel(*refs, has_vres):
    it = iter(refs)
    r_ref, k_ref, v_ref, sm_ref = next(it), next(it), next(it), next(it)
    vf_ref = next(it) if has_vres else None
    mu_ref, musm_ref, w0_ref, w2_ref, a0_ref, a2_ref, g2_ref = (next(it) for _ in range(7))
    kk_ref, ka_ref, rk_ref, lng_ref, lnb_ref = (next(it) for _ in range(5))
    if has_vres:
        v0_ref, v1_ref, v2_ref = next(it), next(it), next(it)
    tri_ref, bd_ref = next(it), next(it)
    y_ref, vo_ref = next(it), next(it)
    z_ref, prev_ref, prevsm_ref = next(it), next(it), next(it)

    c = r_ref.shape[0]
    width = r_ref.shape[1]

    @pl.when(pl.program_id(1) == 0)
    def _():
        z_ref[...] = jnp.zeros(z_ref.shape, F32)
        prev_ref[...] = jnp.zeros(prev_ref.shape, F32)
        prevsm_ref[...] = jnp.zeros(prevsm_ref.shape, F32)

    row = lax.broadcasted_iota(I32, (c, 1), 0)

    def shift_mix(x, pref, idx, mu):
        prev = pref[idx, 0:1, :]
        xs = jnp.where(row == 0, prev, pltpu.roll(x, 1, 0))
        pref[idx, 0:1, :] = x[c - 1:c, :]
        return x + (xs - x) * mu

    xr = shift_mix(r_ref[...].astype(F32), prev_ref, 0, mu_ref[0:1, :])
    xk = shift_mix(k_ref[...].astype(F32), prev_ref, 1, mu_ref[1:2, :])
    xv = shift_mix(v_ref[...].astype(F32), prev_ref, 2, mu_ref[2:3, :])
    sm = shift_mix(sm_ref[...].astype(F32), prevsm_ref, 0, musm_ref[...])

    wpre = w0_ref[...] + _dot(jnp.tanh(sm).astype(BF16), w2_ref[...])
    lw = -jnp.exp(-_softplus(-wpre) - 0.5)
    a = jax.nn.sigmoid(a0_ref[...] + _dot(sm.astype(BF16), a2_ref[...]))
    gate = _dot(jax.nn.sigmoid(sm).astype(BF16), g2_ref[...])
    if has_vres:
        lo = _dot(_dot(xv.astype(BF16), v1_ref[...]).astype(BF16), v2_ref[...])
        xv = xv + (vf_ref[...].astype(F32) - xv) * jax.nn.sigmoid(v0_ref[...] + lo)
    vo_ref[...] = xv.astype(BF16)

    bd = bd_ref[...]
    kk = xk * kk_ref[...]
    kk = kk / jnp.maximum(jnp.sqrt(_seg_sum(kk * kk, bd)), 1e-12)
    kmod = xk * (1.0 + (a - 1.0) * ka_ref[...])
    ah = -kk
    bh = kk * a

    tri = tri_ref[...]
    l1, l2, l3 = _split3(lw)
    cum = _dot(tri, l1) + _dot(tri, l2) + _dot(tri, l3)
    cumx = cum - lw
    cmid = cum[c // 2 - 1:c // 2, :]
    cend = cum[c - 1:c, :]
    e_inv = jnp.exp(cmid - cum)
    e_end = jnp.exp(cend - cum)
    a_t = (ah * jnp.exp(cumx - cmid)).astype(BF16)
    r_t = (xr * jnp.exp(cum - cmid)).astype(BF16)
    b_t = (bh * e_inv).astype(BF16)
    k_t = (kmod * e_inv).astype(BF16)
    a_0 = (ah * jnp.exp(cumx)).astype(BF16)
    r_0 = (xr * jnp.exp(cum)).astype(BF16)
    b_e = (bh * e_end).astype(BF16)
    k_e = (kmod * e_end).astype(BF16)
    w_end = jnp.exp(cend)
    bonus_in = xr * kmod * rk_ref[...]

    ri = lax.broadcasted_iota(I32, (c, c), 0)
    ci = lax.broadcasted_iota(I32, (c, c), 1)
    strict = ci < ri
    incl = ci <= ri
    n_levels = int(np.log2(c))
    assert 2 ** n_levels == c

    for h in range(width // RW_HEAD):
        sl = slice(RW_HEAD * h, RW_HEAD * (h + 1))
        zt = z_ref[h]
        ztb = zt.astype(BF16)
        vh = xv[:, sl]
        vhb = vh.astype(BF16)
        ar = jnp.concatenate([a_t[:, sl], r_t[:, sl]], axis=0)
        bk = jnp.concatenate([b_t[:, sl], k_t[:, sl]], axis=0)
        m1 = _dot_nt(ar, bk)
        a_ab = jnp.where(strict, m1[0:c, 0:c], 0.0)
        a_ak = jnp.where(strict, m1[0:c, c:2 * c], 0.0).astype(BF16)
        pq = jnp.concatenate([jnp.where(incl, m1[c:2 * c, 0:c], 0.0),
                              jnp.where(incl, m1[c:2 * c, c:2 * c], 0.0)], axis=1).astype(BF16)
        u = _dot_nt(a_0[:, sl], ztb) + _dot(a_ak, vhb)
        npow = a_ab.astype(BF16)
        for lvl in range(n_levels):
            u = u + _dot(npow, u.astype(BF16))
            if lvl + 1 < n_levels:
                npow = _dot(npow, npow).astype(BF16)
        uv = jnp.concatenate([u, vh], axis=0)
        y = _dot_nt(r_0[:, sl], ztb) + _dot(pq, uv.astype(BF16))
        bke = jnp.concatenate([b_e[:, sl], k_e[:, sl]], axis=0)
        z_ref[h] = zt * w_end[:, sl] + _dot(uv.T.astype(BF16), bke)
        ym = jnp.mean(y, axis=1, keepdims=True)
        yd = y - ym
        yv = jnp.mean(yd * yd, axis=1, keepdims=True)
        yn = yd * lax.rsqrt(yv + RW_LNX_EPS) * lng_ref[:, sl] + lnb_ref[:, sl]
        bonus = jnp.sum(bonus_in[:, sl], axis=1, keepdims=True) * vh
        y_ref[:, sl] = ((yn + bonus) * gate[:, sl]).astype(BF16)


def _rwkv(proj3, vfirst, p, *, has_vres):
    b, s, _ = proj3.shape
    c = RW_CHUNK
    w = RW_WIDTH
    cb = lambda off: off // w
    fix = lambda bi, i: (0, 0)
    blk = lambda off: pl.BlockSpec((None, c, w), lambda bi, i: (bi, i, cb(off)))
    row_w = pl.BlockSpec((1, w), fix)
    in_specs = [blk(OFF_R), blk(OFF_K), blk(OFF_V),
                pl.BlockSpec((None, c, SMALL_W), lambda bi, i: (bi, i, OFF_SMALL // SMALL_W))]
    args = [proj3, proj3, proj3, proj3]
    if has_vres:
        in_specs.append(pl.BlockSpec((None, c, w), lambda bi, i: (bi, i, 0)))
        args.append(vfirst)
    in_specs += [pl.BlockSpec((3, w), fix), pl.BlockSpec((1, SMALL_W), fix), row_w,
                 pl.BlockSpec((SMALL_W, w), fix), row_w, pl.BlockSpec((SMALL_W, w), fix),
                 pl.BlockSpec((SMALL_W, w), fix), row_w, row_w, row_w, row_w, row_w]
    args += [p["mu"], p["mu_sm"], p["w0"], p["w2"], p["a0"], p["a2"], p["g2"],
             p["k_k"], p["k_a"], p["r_k"], p["lnx_g"], p["lnx_b"]]
    if has_vres:
        in_specs += [row_w, pl.BlockSpec((w, LANES), fix), pl.BlockSpec((LANES, w), fix)]
        args += [p["v0"], p["v1"], p["v2"]]
    tri = jnp.asarray(np.tril(np.ones((c, c), np.float32)), BF16)
    lane_head = np.arange(LANES) // RW_HEAD
    bd = jnp.asarray((lane_head[:, None] == lane_head[None, :]).astype(np.float32), BF16)
    in_specs += [pl.BlockSpec((c, c), fix), pl.BlockSpec((LANES, LANES), fix)]
    args += [tri, bd]
    out_blk = pl.BlockSpec((None, c, w), lambda bi, i: (bi, i, 0))
    return pl.pallas_call(
        functools.partial(_rwkv_kernel, has_vres=has_vres),
        grid=(b, s // c),
        in_specs=in_specs,
        out_specs=[out_blk, out_blk],
        out_shape=[jax.ShapeDtypeStruct((b, s, w), BF16), jax.ShapeDtypeStruct((b, s, w), BF16)],
        scratch_shapes=[pltpu.VMEM((w // RW_HEAD, RW_HEAD, RW_HEAD), F32),
                        pltpu.VMEM((3, 8, w), F32), pltpu.VMEM((1, 8, SMALL_W), F32)],
        compiler_params=_cp(("parallel", "arbitrary")),
        name="rwkv7",
    )(*args)


POOL_HALO = 16


def _pool_kernel(cur_ref, halo_ref, w_ref, sc_ref, o_ref):
    i = pl.program_id(1)
    tp = cur_ref.shape[0]
    cur = cur_ref[...].astype(F32)
    halo = jnp.where(i > 0, halo_ref[...].astype(F32), 0.0)
    ext = jnp.concatenate([halo, cur], axis=0)
    sums = {1: ext}
    win = 1
    while win < max(POOL_WINDOWS):
        sums[2 * win] = sums[win] + pltpu.roll(sums[win], win, 0)
        win *= 2
    count = (i * tp + 1 + lax.broadcasted_iota(I32, (tp, 1), 0)).astype(F32)
    for gi, win in enumerate(POOL_WINDOWS):
        sl = slice(POOL_GROUP * gi, POOL_GROUP * (gi + 1))
        mean = sums[win][POOL_HALO:, sl] / jnp.minimum(count, float(win))
        y = _dot((mean - cur[:, sl]).astype(BF16), w_ref[gi])
        o_ref[:, sl] = (y * sc_ref[:, sl]).astype(BF16)


def _pool(proj3, pool_w, pool_scale, *, tp=256):
    b, s, _ = proj3.shape
    w = POOL_WIDTH
    cb = OFF_POOL // w
    per = tp // POOL_HALO
    return pl.pallas_call(
        _pool_kernel,
        grid=(b, s // tp),
        in_specs=[pl.BlockSpec((None, tp, w), lambda bi, i: (bi, i, cb)),
                  pl.BlockSpec((None, POOL_HALO, w), lambda bi, i: (bi, jnp.maximum(i * per - 1, 0), cb)),
                  pl.BlockSpec((len(POOL_WINDOWS), POOL_GROUP, POOL_GROUP), lambda bi, i: (0, 0, 0)),
                  pl.BlockSpec((1, w), lambda bi, i: (0, 0))],
        out_specs=pl.BlockSpec((None, tp, w), lambda bi, i: (bi, i, 0)),
        out_shape=jax.ShapeDtypeStruct((b, s, w), BF16),
        compiler_params=_cp(("parallel", "parallel")),
        name="pool",
    )(proj3, proj3, pool_w, pool_scale.reshape(1, w))


def _merge_kernel(ya_ref, yb_ref, yc_ref, ga_ref, gb_ref, gc_ref, wb_ref, o_ref):
    acc = jax.nn.sigmoid(ga_ref[...].astype(F32)) * _dot(ya_ref[...], wb_ref[0])
    acc = acc + jax.nn.sigmoid(gb_ref[...].astype(F32)) * _dot(yb_ref[...], wb_ref[1])
    acc = acc + jax.nn.sigmoid(gc_ref[...].astype(F32)) * _dot(yc_ref[...], wb_ref[2])
    o_ref[...] = acc.astype(BF16)


def _merge(ya, yb, yc, proj, wb, *, tm=256):
    t, w = ya.shape
    d = wb.shape[2]
    row = lambda i: (i, 0)
    gate = lambda k: pl.BlockSpec((tm, d), lambda i: (i, k))
    return pl.pallas_call(
        _merge_kernel,
        grid=(t // tm,),
        in_specs=[pl.BlockSpec((tm, w), row)] * 3 + [gate(0), gate(1), gate(2),
                                                      pl.BlockSpec((N_BRANCH, w, d), lambda i: (0, 0, 0))],
        out_specs=pl.BlockSpec((tm, d), row),
        out_shape=jax.ShapeDtypeStruct((t, d), BF16),
        compiler_params=_cp(("parallel",)),
        name="merge",
    )(ya, yb, yc, proj, proj, proj, wb)


def _xattn_kernel(h16_ref, h32_ref, wq_ref, mkv_ref, wo_ref, g_ref, b_ref, o32_ref, o16_ref, *, alpha):
    q = _dot(h16_ref[...], wq_ref[...])
    scale = XA_HEAD_DIM ** -0.5
    outs = []
    for hd in range(XA_HEADS):
        sl = slice(XA_HEAD_DIM * hd, XA_HEAD_DIM * (hd + 1))
        slv = slice(XA_WIDTH + XA_HEAD_DIM * hd, XA_WIDTH + XA_HEAD_DIM * (hd + 1))
        s = _dot_nt(q[:, sl].astype(BF16), mkv_ref[:, sl]) * scale
        e = jnp.exp(s - jnp.max(s, axis=1, keepdims=True))
        p = e / jnp.sum(e, axis=1, keepdims=True)
        outs.append(_dot(p.astype(BF16), mkv_ref[:, slv]))
    o = jnp.concatenate(outs, axis=1).astype(BF16)
    y = alpha * h32_ref[...] + _dot(o, wo_ref[...])
    out = _layer_norm_rows(y, g_ref[...], b_ref[...])
    o32_ref[...] = out
    o16_ref[...] = out.astype(BF16)


def _xattn(h16, h32, wq, memkv, wo, g, b, *, alpha, seq, tm=256):
    t, d = h16.shape
    per_b = seq // tm
    nm = memkv.shape[1]
    row = lambda i: (i, 0)
    fix = lambda i: (0, 0)
    return pl.pallas_call(
        functools.partial(_xattn_kernel, alpha=alpha),
        grid=(t // tm,),
        in_specs=[pl.BlockSpec((tm, d), row), pl.BlockSpec((tm, d), row),
                  pl.BlockSpec((d, XA_WIDTH), fix),
                  pl.BlockSpec((None, nm, 2 * XA_WIDTH), lambda i: (i // per_b, 0, 0)),
                  pl.BlockSpec((XA_WIDTH, d), fix), pl.BlockSpec((1, d), fix), pl.BlockSpec((1, d), fix)],
        out_specs=[pl.BlockSpec((tm, d), row), pl.BlockSpec((tm, d), row)],
        out_shape=[jax.ShapeDtypeStruct((t, d), F32), jax.ShapeDtypeStruct((t, d), BF16)],
        compiler_params=_cp(("parallel",)),
        name="xattn",
    )(h16, h32, wq, memkv, wo, g.reshape(1, d), b.reshape(1, d))


def _ffn_kernel(*refs, ln, alpha):
    if ln:
        _, x_ref, wg_ref, wu_ref, wd_ref, res_ref, g_ref, b_ref, o32_ref, o16_ref, acc_ref = refs
    else:
        _, x_ref, wg_ref, wu_ref, wd_ref, o_ref, acc_ref = refs
    f = pl.program_id(1)

    @pl.when(f == 0)
    def _():
        acc_ref[...] = jnp.zeros(acc_ref.shape, F32)

    x = x_ref[...].astype(BF16)
    gt = _dot(x, wg_ref[0])
    up = _dot(x, wu_ref[0])
    act = (gt * jax.nn.sigmoid(gt) * up).astype(BF16)
    acc_ref[...] += _dot(act, wd_ref[0])

    @pl.when(f == pl.num_programs(1) - 1)
    def _():
        if ln:
            out = _layer_norm_rows(alpha * res_ref[...] + acc_ref[...], g_ref[...], b_ref[...])
            o32_ref[...] = out
            o16_ref[...] = out.astype(BF16)
        else:
            o_ref[...] = acc_ref[...]


def _ffn(block_expert, x, wg, wu, wd, *, tm, tf, res=None, g=None, b=None, alpha=1.0):
    n, d = x.shape
    ff = wg.shape[2]
    ln = res is not None
    row = lambda i, f, be: (i, 0)
    fix = lambda i, f, be: (0, 0)
    in_specs = [pl.BlockSpec((tm, d), row),
                pl.BlockSpec((1, d, tf), lambda i, f, be: (be[i], 0, f)),
                pl.BlockSpec((1, d, tf), lambda i, f, be: (be[i], 0, f)),
                pl.BlockSpec((1, tf, d), lambda i, f, be: (be[i], f, 0))]
    args = [x, wg, wu, wd]
    if ln:
        in_specs += [pl.BlockSpec((tm, d), row), pl.BlockSpec((1, d), fix), pl.BlockSpec((1, d), fix)]
        args += [res, g.reshape(1, d), b.reshape(1, d)]
        out_specs = [pl.BlockSpec((tm, d), row), pl.BlockSpec((tm, d), row)]
        out_shape = [jax.ShapeDtypeStruct((n, d), F32), jax.ShapeDtypeStruct((n, d), BF16)]
    else:
        out_specs = pl.BlockSpec((tm, d), row)
        out_shape = jax.ShapeDtypeStruct((n, d), F32)
    return pl.pallas_call(
        functools.partial(_ffn_kernel, ln=ln, alpha=alpha),
        grid_spec=pltpu.PrefetchScalarGridSpec(
            num_scalar_prefetch=1, grid=(n // tm, ff // tf),
            in_specs=in_specs, out_specs=out_specs,
            scratch_shapes=[pltpu.VMEM((tm, d), F32)]),
        out_shape=out_shape,
        compiler_params=_cp(("parallel", "arbitrary")),
        name="swiglu",
    )(block_expert, *args)


def _router_kernel(x_ref, w_ref, o_ref):
    logits = _dot(x_ref[...], w_ref[...])
    lane = lax.broadcasted_iota(I32, logits.shape, 1)
    lane_f = lane.astype(F32)
    l1 = jnp.where(lane < N_EXPERTS, logits, VERY_NEG)
    m1 = jnp.max(l1, axis=1, keepdims=True)
    i1 = jnp.min(jnp.where(l1 == m1, lane_f, float(LANES)), axis=1, keepdims=True)
    l2 = jnp.where(lane_f == i1, VERY_NEG, l1)
    m2 = jnp.max(l2, axis=1, keepdims=True)
    i2 = jnp.min(jnp.where(l2 == m2, lane_f, float(LANES)), axis=1, keepdims=True)
    e = jnp.exp(m2 - m1)
    g1 = 1.0 / (1.0 + e)
    g2 = e / (1.0 + e)
    o_ref[...] = jnp.where(lane == 0, i1, jnp.where(lane == 1, i2, jnp.where(lane == 2, g1,
                           jnp.where(lane == 3, g2, 0.0))))


def _router(h16, wr, *, tm=512):
    t, d = h16.shape
    return pl.pallas_call(
        _router_kernel,
        grid=(t // tm,),
        in_specs=[pl.BlockSpec((tm, d), lambda i: (i, 0)), pl.BlockSpec((d, LANES), lambda i: (0, 0))],
        out_specs=pl.BlockSpec((tm, LANES), lambda i: (i, 0)),
        out_shape=jax.ShapeDtypeStruct((t, LANES), F32),
        compiler_params=_cp(("parallel",)),
        name="router",
    )(h16, wr)


GATHER_ROWS = 256


def _gather_kernel(idx_ref, src_ref, o_ref, sem):
    n = o_ref.shape[0]

    def row_copy(r, src_row):
        return pltpu.make_async_copy(src_ref.at[pl.ds(src_row, 1)], o_ref.at[pl.ds(r, 1)], sem)

    def start(r, carry):
        row_copy(r, idx_ref[0, r]).start()
        return carry

    def wait(r, carry):
        row_copy(r, 0).wait()
        return carry

    lax.fori_loop(0, n, start, 0)
    lax.fori_loop(0, n, wait, 0)


def _gather_rows(src, idx):
    n = idx.shape[0]
    d = src.shape[1]
    rows = GATHER_ROWS
    return pl.pallas_call(
        _gather_kernel,
        grid=(n // rows,),
        in_specs=[pl.BlockSpec((None, 1, rows), lambda i: (i, 0, 0), memory_space=pltpu.SMEM),
                  pl.BlockSpec(memory_space=pl.ANY)],
        out_specs=pl.BlockSpec((rows, d), lambda i: (i, 0)),
        out_shape=jax.ShapeDtypeStruct((n, d), src.dtype),
        scratch_shapes=[pltpu.SemaphoreType.DMA(())],
        compiler_params=_cp(("arbitrary",)),
        name="gather_rows",
    )(idx.reshape(n // rows, 1, rows), src)


def _combine_kernel(y_ref, rt_ref, res_ref, g_ref, b_ref, o_ref, *, alpha):
    d = res_ref.shape[1]
    f = y_ref[:, 0:d] * rt_ref[:, 2:3] + y_ref[:, d:2 * d] * rt_ref[:, 3:4]
    o_ref[...] = _layer_norm_rows(alpha * res_ref[...] + f, g_ref[...], b_ref[...])


def _combine_ln(y2, rt, res, g, b, *, alpha, tm=256):
    t, d = res.shape
    row = lambda i: (i, 0)
    fix = lambda i: (0, 0)
    return pl.pallas_call(
        functools.partial(_combine_kernel, alpha=alpha),
        grid=(t // tm,),
        in_specs=[pl.BlockSpec((tm, 2 * d), row), pl.BlockSpec((tm, LANES), row), pl.BlockSpec((tm, d), row),
                  pl.BlockSpec((1, d), fix), pl.BlockSpec((1, d), fix)],
        out_specs=pl.BlockSpec((tm, d), row),
        out_shape=jax.ShapeDtypeStruct((t, d), F32),
        compiler_params=_cp(("parallel",)),
        name="moe_combine_ln",
    )(y2, rt, res, g.reshape(1, d), b.reshape(1, d))


def _moe(h16, h32, router, w_gate, w_up, w_down, g, b, *, alpha):
    t, d = h32.shape
    tm = MOE_TM
    wr = jnp.pad(router, ((0, 0), (0, LANES - N_EXPERTS))).astype(BF16)
    rt = _router(h16, wr)
    expert = rt[:, 0:TOP_K].astype(I32).reshape(-1)
    n_assign = t * TOP_K
    onehot = (expert[:, None] == jnp.arange(N_EXPERTS, dtype=I32)[None, :]).astype(I32)
    csum = jnp.cumsum(onehot, axis=0)
    rank = jnp.sum(csum * onehot, axis=1) - 1
    counts = csum[-1]
    padded = (counts + tm - 1) // tm * tm
    pad_end = jnp.cumsum(padded)
    pad_start = pad_end - padded
    dest = (pad_start[expert] + rank).astype(I32)
    n_pad = n_assign + N_EXPERTS * tm
    n_blocks = n_pad // tm
    tok_pad = jnp.zeros((n_pad,), I32).at[dest].set(jnp.arange(n_assign, dtype=I32) // TOP_K)
    block_expert = jnp.minimum(
        jnp.searchsorted(pad_end, jnp.arange(n_blocks, dtype=I32) * tm, side="right"),
        N_EXPERTS - 1).astype(I32)
    x_sorted = _gather_rows(h32, tok_pad)
    y_sorted = _ffn(block_expert, x_sorted, w_gate, w_up, w_down, tm=tm, tf=512)
    y_tok = _gather_rows(y_sorted, dest)
    return _combine_ln(y_tok.reshape(t, TOP_K * d), rt, h32, g, b, alpha=alpha)


def _perm_w_in(w, d_model):
    sizes = ((NSA_WIDTH,) + (NSA_KV_WIDTH,) * 6 + (N_BRANCH * NSA_HEADS,)
             + (RW_WIDTH, RW_WIDTH, RW_WIDTH, RW_DECAY_LORA, RW_A_LORA, RW_GATE_LORA)
             + (POOL_WIDTH,) + (d_model,) * N_BRANCH)
    offs = np.concatenate([[0], np.cumsum(sizes)])
    seg = [w[:, int(offs[i]):int(offs[i + 1])] for i in range(len(sizes))]
    (q, kc, vc, ks, vs, kw, vw, ng, r, k, v, wl, al, gl, pool, ga, gb, gc) = seg
    zeros = lambda n: jnp.zeros((w.shape[0], n), w.dtype)
    cols = [ga, gb, gc, q, r, k, v, pool, kc, vc, ks, vs, kw, vw,
            ng, zeros(SM_WL - N_BRANCH * NSA_HEADS), wl, al, gl, zeros(SMALL_W - SM_GL - RW_GATE_LORA)]
    out = jnp.concatenate(cols, axis=1).astype(BF16)
    assert out.shape[1] == NP_COLS and d_model == 2048
    return out


def _rwkv_params(l, mu, w0, w2, a0, a2, g2, k_k, k_a, r_k, lnx_g, lnx_b, v0, v1, v2):
    w = RW_WIDTH
    rowv = lambda a: a.reshape(1, w).astype(F32)
    m = mu[l]
    o = np.cumsum((0, w, w, w, RW_DECAY_LORA, RW_A_LORA, RW_GATE_LORA))
    mu_sm = jnp.zeros((1, SMALL_W), F32)
    mu_sm = mu_sm.at[0, SM_WL:SM_WL + RW_DECAY_LORA].set(m[o[3]:o[4]])
    mu_sm = mu_sm.at[0, SM_AL:SM_AL + RW_A_LORA].set(m[o[4]:o[5]])
    mu_sm = mu_sm.at[0, SM_GL:SM_GL + RW_GATE_LORA].set(m[o[5]:o[6]])

    def pad_rows(a, off):
        return jnp.zeros((SMALL_W, w), F32).at[off:off + a.shape[0]].set(a).astype(BF16)

    p = dict(mu=m[0:3 * w].reshape(3, w), mu_sm=mu_sm, w0=rowv(w0[l]), w2=pad_rows(w2[l], SM_WL),
             a0=rowv(a0[l]), a2=pad_rows(a2[l], SM_AL), g2=pad_rows(g2[l], SM_GL),
             k_k=rowv(k_k[l]), k_a=rowv(k_a[l]), r_k=rowv(r_k[l]), lnx_g=rowv(lnx_g[l]), lnx_b=rowv(lnx_b[l]))
    if l > 0:
        p["v0"] = rowv(v0[l - 1])
        p["v1"] = jnp.pad(v1[l - 1], ((0, 0), (0, LANES - RW_V_LORA))).astype(BF16)
        p["v2"] = jnp.pad(v2[l - 1], ((0, LANES - RW_V_LORA), (0, 0))).astype(BF16)
    return p


def _stride_units(x2d, b, s):
    x = x2d.reshape(b, s // CMP_STRIDE, CMP_STRIDE, NSA_KV_HEADS, HEAD_DIM)
    return x.transpose(0, 3, 1, 2, 4).reshape(b, NSA_KV_HEADS, s // CMP_STRIDE, CMP_STRIDE * HEAD_DIM)


def _token_mixer(l, h16, proj, b, s, tabs, tabs_c, prm, vfirst):
    d_model = h16.shape[1]
    t = b * s
    proj3 = proj.reshape(b, s, NP_COLS)
    c, shi, slo = tabs
    q_r, ksa, kw_r, gates = _nsa_prep(proj3, c, shi, slo)
    n_units = s // CMP_STRIDE
    half = CMP_STRIDE * HEAD_DIM
    kvc = proj[:, OFF_KVC:OFF_KVC + 2 * NSA_KV_WIDTH]
    cmp_out = []
    for j in range(2):
        xu = _stride_units(kvc[:, NSA_KV_WIDTH * j:NSA_KV_WIDTH * (j + 1)], b, s)
        pe = prm["nsa_cmp_pe"][l, j].reshape(2, half)
        w1 = prm["nsa_cmp_w1"][l, j].astype(BF16)
        b1 = prm["nsa_cmp_b1"][l, j].reshape(1, CMP_HIDDEN)
        w2p = jnp.pad(prm["nsa_cmp_w2"][l, j], ((0, 0), (0, LANES - HEAD_DIM))).astype(BF16)
        cmp_out.append(_compress(xu, pe, w1, b1, w2p, *tabs_c, rope=(j == 0)))
    kc, vc = cmp_out
    y_a = _nsa_attention(q_r, gates, kc, vc, ksa, kw_r, proj3).reshape(t, NSA_WIDTH)
    rp = _rwkv_params(l, prm["rwkv_mu"], prm["rwkv_w0"], prm["rwkv_w2"], prm["rwkv_a0"], prm["rwkv_a2"],
                      prm["rwkv_g2"], prm["rwkv_k_k"], prm["rwkv_k_a"], prm["rwkv_r_k"], prm["rwkv_lnx_g"],
                      prm["rwkv_lnx_b"], prm["rwkv_v0"], prm["rwkv_v1"], prm["rwkv_v2"])
    y_b, v_l = _rwkv(proj3, vfirst, rp, has_vres=(l > 0))
    y_c = _pool(proj3, prm["pool_w"][l].astype(BF16), prm["pool_scale"][l])
    merged = _merge(y_a, y_b.reshape(t, RW_WIDTH), y_c.reshape(t, POOL_WIDTH), proj,
                    prm["w_branch"][l].astype(BF16))
    return merged, v_l


def kernel(x, mem, positions, w_in, nsa_cmp_pe, nsa_cmp_w1, nsa_cmp_b1, nsa_cmp_w2, rwkv_mu, rwkv_w0, rwkv_w2, rwkv_a0, rwkv_a2, rwkv_g2, rwkv_k_k, rwkv_k_a, rwkv_r_k, rwkv_lnx_g, rwkv_lnx_b, rwkv_v0, rwkv_v1, rwkv_v2, pool_w, pool_scale, w_branch, w_out, mem_ln_g, mem_ln_b, mem_wkv, xa_wq, xa_wo, ln_g, ln_b, ffn_w_gate, ffn_w_up, ffn_w_down, moe_router, moe_w_gate, moe_w_up, moe_w_down):
    prm = dict(nsa_cmp_pe=nsa_cmp_pe, nsa_cmp_w1=nsa_cmp_w1, nsa_cmp_b1=nsa_cmp_b1, nsa_cmp_w2=nsa_cmp_w2,
               rwkv_mu=rwkv_mu, rwkv_w0=rwkv_w0, rwkv_w2=rwkv_w2, rwkv_a0=rwkv_a0, rwkv_a2=rwkv_a2,
               rwkv_g2=rwkv_g2, rwkv_k_k=rwkv_k_k, rwkv_k_a=rwkv_k_a, rwkv_r_k=rwkv_r_k,
               rwkv_lnx_g=rwkv_lnx_g, rwkv_lnx_b=rwkv_lnx_b, rwkv_v0=rwkv_v0, rwkv_v1=rwkv_v1,
               rwkv_v2=rwkv_v2, pool_w=pool_w, pool_scale=pool_scale, w_branch=w_branch)
    b, s, d = x.shape
    t = b * s
    depth = w_in.shape[0]
    alpha = (2 * depth) ** 0.25
    n_mem = mem.shape[1]

    inv_freq = ROPE_THETA ** (-jnp.arange(ROPE_HALF, dtype=F32) / ROPE_HALF)
    dd = np.arange(LANES) % HEAD_DIM
    invf_lane = jnp.where(jnp.asarray(dd < ROPE_DIM), inv_freq[jnp.asarray(dd % ROPE_HALF)], 0.0).reshape(1, LANES)
    tabs = tuple(a.reshape(b, s, LANES) for a in _rope_tables(positions.reshape(t, 1), invf_lane, ts=512))
    n_units = s // CMP_STRIDE
    pos_c = positions[:, CMP_BLOCK - 1::CMP_STRIDE]
    pos_c = jnp.concatenate([pos_c, pos_c[:, -1:]], axis=1)
    tabs_c = tuple(a.reshape(b, n_units, LANES)
                   for a in _rope_tables(pos_c.reshape(b * n_units, 1), invf_lane, ts=n_units))

    memkv = _memkv(mem.reshape(b * n_mem, d), mem_ln_g, mem_ln_b, mem_wkv.astype(BF16))
    memkv = memkv.reshape(b, n_mem, 2 * XA_WIDTH)

    h32 = x.reshape(t, d)
    h16 = h32.astype(BF16)
    vfirst = None
    for l in range(depth):
        proj = _matmul(h16, _perm_w_in(w_in[l], d), tm=512, tn=1024, out_dtype=BF16)
        merged, v_l = _token_mixer(l, h16, proj, b, s, tabs, tabs_c, prm, vfirst)
        if l == 0:
            vfirst = v_l
        h32, h16 = _matmul_res_ln(merged, w_out[l].astype(BF16), h32, ln_g[l, 0], ln_b[l, 0], alpha=alpha)
        h32, h16 = _xattn(h16, h32, xa_wq[l].astype(BF16), memkv, xa_wo[l].astype(BF16),
                          ln_g[l, 1], ln_b[l, 1], alpha=alpha, seq=s)
        if l % 2 == 0:
            e = l // 2
            be = jnp.zeros((t // 512,), I32)
            h32, h16 = _ffn(be, h16, ffn_w_gate[e:e + 1].astype(BF16), ffn_w_up[e:e + 1].astype(BF16),
                            ffn_w_down[e:e + 1].astype(BF16), tm=512, tf=512,
                            res=h32, g=ln_g[l, 2], b=ln_b[l, 2], alpha=alpha)
        else:
            e = l // 2
            h32 = _moe(h16, h32, moe_router[e], moe_w_gate[e].astype(BF16), moe_w_up[e].astype(BF16),
                       moe_w_down[e].astype(BF16), ln_g[l, 2], ln_b[l, 2], alpha=alpha)
            h16 = h32.astype(BF16)
    return h32.reshape(b, s, d)
```

```python
import functools

import numpy as np
import jax
import jax.numpy as jnp
from jax import lax
from jax.experimental import pallas as pl
from jax.experimental.pallas import tpu as pltpu

F32 = jnp.float32
BF16 = jnp.bfloat16
I32 = jnp.int32

HEAD_DIM = 64
ROPE_DIM = HEAD_DIM // 4
ROPE_HALF = ROPE_DIM // 2
ROPE_THETA = 500000.0
NSA_HEADS = 16
NSA_KV_HEADS = 4
NSA_GROUP = NSA_HEADS // NSA_KV_HEADS
NSA_WIDTH = NSA_HEADS * HEAD_DIM
NSA_KV_WIDTH = NSA_KV_HEADS * HEAD_DIM
CMP_BLOCK = 32
CMP_STRIDE = 16
CMP_HIDDEN = 256
SLC_BLOCK = 64
SLC_TOPN = 8
WINDOW = 512
SEL_FORCE = 1e9
RW_HEADS = 16
RW_HEAD = 64
RW_WIDTH = RW_HEADS * RW_HEAD
RW_DECAY_LORA = 64
RW_A_LORA = 64
RW_V_LORA = 32
RW_GATE_LORA = 160
RW_LNX_EPS = 64e-5
POOL_WINDOWS = (2, 4, 8, 16)
POOL_GROUP = 256
POOL_WIDTH = POOL_GROUP * len(POOL_WINDOWS)
XA_HEADS = 4
XA_HEAD_DIM = 128
XA_WIDTH = XA_HEADS * XA_HEAD_DIM
N_EXPERTS = 8
TOP_K = 2
N_BRANCH = 3
LN_EPS = 1e-5
NEG_INF = -1e30
VERY_NEG = -3e38
MASK_BIG = 2.0 ** 100

LANES = 128
VMEM_LIMIT = 56 * 1024 * 1024

NP_COLS = 13312
OFF_GATES = 0
OFF_Q = 6144
OFF_R = 7168
OFF_K = 8192
OFF_V = 9216
OFF_POOL = 10240
OFF_KVC = 11264
OFF_KVS = 11776
OFF_KVW = 12288
OFF_SMALL = 12800
SMALL_W = 512
SM_WL = 128
SM_AL = 192
SM_GL = 256

NSA_TQ = 128
RW_CHUNK = 64
MOE_TM = 512


def _cp(sem, vmem=VMEM_LIMIT):
    return pltpu.CompilerParams(dimension_semantics=sem, vmem_limit_bytes=vmem)


def _dot(a, b):
    return jnp.dot(a, b, preferred_element_type=F32)


def _dot_nt(a, b):
    return lax.dot_general(a, b, (((1,), (1,)), ((), ())), preferred_element_type=F32)


def _layer_norm_rows(y, g, b):
    mu = jnp.mean(y, axis=-1, keepdims=True)
    d = y - mu
    var = jnp.mean(d * d, axis=-1, keepdims=True)
    return d * lax.rsqrt(var + LN_EPS) * g + b


def _mm_kernel(x_ref, w_ref, o_ref):
    o_ref[...] = _dot(x_ref[...], w_ref[...]).astype(o_ref.dtype)


def _matmul(x, w, *, tm, tn, out_dtype):
    m, k = x.shape
    n = w.shape[1]
    return pl.pallas_call(
        _mm_kernel,
        grid=(n // tn, m // tm),
        in_specs=[pl.BlockSpec((tm, k), lambda j, i: (i, 0)),
                  pl.BlockSpec((k, tn), lambda j, i: (0, j))],
        out_specs=pl.BlockSpec((tm, tn), lambda j, i: (i, j)),
        out_shape=jax.ShapeDtypeStruct((m, n), out_dtype),
        compiler_params=_cp(("parallel", "parallel")),
        name="matmul",
    )(x, w)


def _mm_ln_kernel(x_ref, w_ref, res_ref, g_ref, b_ref, o32_ref, o16_ref, *, alpha):
    y = alpha * res_ref[...] + _dot(x_ref[...], w_ref[...])
    out = _layer_norm_rows(y, g_ref[...], b_ref[...])
    o32_ref[...] = out
    o16_ref[...] = out.astype(BF16)


def _matmul_res_ln(x, w, res, g, b, *, alpha, tm=256):
    m, k = x.shape
    n = w.shape[1]
    row = lambda i: (i, 0)
    fix = lambda i: (0, 0)
    return pl.pallas_call(
        functools.partial(_mm_ln_kernel, alpha=alpha),
        grid=(m // tm,),
        in_specs=[pl.BlockSpec((tm, k), row), pl.BlockSpec((k, n), fix), pl.BlockSpec((tm, n), row),
                  pl.BlockSpec((1, n), fix), pl.BlockSpec((1, n), fix)],
        out_specs=[pl.BlockSpec((tm, n), row), pl.BlockSpec((tm, n), row)],
        out_shape=[jax.ShapeDtypeStruct((m, n), F32), jax.ShapeDtypeStruct((m, n), BF16)],
        compiler_params=_cp(("parallel",)),
        name="matmul_res_ln",
    )(x, w, res, g.reshape(1, n), b.reshape(1, n))


def _memkv_kernel(m_ref, g_ref, b_ref, w_ref, o_ref):
    xn = _layer_norm_rows(m_ref[...], g_ref[...], b_ref[...])
    o_ref[...] = _dot(xn.astype(BF16), w_ref[...]).astype(o_ref.dtype)


def _memkv(mem2d, g, b, w, *, tm=256):
    m, d = mem2d.shape
    n = w.shape[1]
    row = lambda i: (i, 0)
    fix = lambda i: (0, 0)
    return pl.pallas_call(
        _memkv_kernel,
        grid=(m // tm,),
        in_specs=[pl.BlockSpec((tm, d), row), pl.BlockSpec((1, d), fix), pl.BlockSpec((1, d), fix),
                  pl.BlockSpec((d, n), fix)],
        out_specs=pl.BlockSpec((tm, n), row),
        out_shape=jax.ShapeDtypeStruct((m, n), BF16),
        compiler_params=_cp(("parallel",)),
        name="memkv",
    )(mem2d, g.reshape(1, d), b.reshape(1, d), w)


def _rope_table_kernel(pos_ref, invf_ref, c_ref, shi_ref, slo_ref):
    ang = pos_ref[...].astype(F32) * invf_ref[...]
    d = lax.broadcasted_iota(I32, ang.shape, 1) % HEAD_DIM
    cs = jnp.cos(ang)
    sn = jnp.sin(ang)
    c_ref[...] = jnp.where(d < ROPE_DIM, cs, 1.0)
    shi_ref[...] = jnp.where((d >= ROPE_HALF) & (d < ROPE_DIM), sn, 0.0)
    slo_ref[...] = jnp.where(d < ROPE_HALF, -sn, 0.0)


def _rope_tables(pos_col, invf_lane, *, ts):
    n = pos_col.shape[0]
    row = lambda i: (i, 0)
    out = jax.ShapeDtypeStruct((n, LANES), F32)
    return pl.pallas_call(
        _rope_table_kernel,
        grid=(n // ts,),
        in_specs=[pl.BlockSpec((ts, 1), row), pl.BlockSpec((1, LANES), lambda i: (0, 0))],
        out_specs=[pl.BlockSpec((ts, LANES), row)] * 3,
        out_shape=[out, out, out],
        compiler_params=_cp(("parallel",)),
        name="rope_tables",
    )(pos_col, invf_lane)


def _rope128(x, c, shi, slo):
    return x * c + pltpu.roll(x, ROPE_HALF, 1) * shi + pltpu.roll(x, LANES - ROPE_HALF, 1) * slo


def _nsa_prep_kernel(q_ref, ks_ref, kw_ref, sm_ref, c_ref, shi_ref, slo_ref,
                     qo_ref, ksa_ref, kwo_ref, g_ref):
    c = c_ref[...]
    shi = shi_ref[...]
    slo = slo_ref[...]
    ts = c.shape[0]
    scale = HEAD_DIM ** -0.5
    for j in range(NSA_WIDTH // LANES):
        sl = slice(LANES * j, LANES * (j + 1))
        qo_ref[:, sl] = (_rope128(q_ref[:, sl].astype(F32), c, shi, slo) * scale).astype(BF16)
    t = pl.program_id(1) * ts + lax.broadcasted_iota(I32, (ts, LANES), 0)
    lane = lax.broadcasted_iota(I32, (ts, LANES), 1)
    onehot = jnp.where(lane - HEAD_DIM == t // SLC_BLOCK, 1.0, 0.0)
    lo = lane < HEAD_DIM
    for j in range(NSA_KV_WIDTH // LANES):
        sl = slice(LANES * j, LANES * (j + 1))
        ks = _rope128(ks_ref[:, sl].astype(F32), c, shi, slo)
        kwo_ref[:, sl] = _rope128(kw_ref[:, sl].astype(F32), c, shi, slo).astype(BF16)
        ksa_ref[2 * j] = jnp.where(lo, ks, onehot).astype(BF16)
        ksa_ref[2 * j + 1] = jnp.where(lo, pltpu.roll(ks, HEAD_DIM, 1), onehot).astype(BF16)
    g_ref[...] = jax.nn.sigmoid(sm_ref[:, 0:LANES].astype(F32))


def _nsa_prep(proj3, c, shi, slo, *, ts=512):
    b, s, _ = proj3.shape
    cb = lambda w, off: off // w
    tab = pl.BlockSpec((None, ts, LANES), lambda bi, i: (bi, i, 0))
    return pl.pallas_call(
        _nsa_prep_kernel,
        grid=(b, s // ts),
        in_specs=[
            pl.BlockSpec((None, ts, NSA_WIDTH), lambda bi, i: (bi, i, cb(NSA_WIDTH, OFF_Q))),
            pl.BlockSpec((None, ts, NSA_KV_WIDTH), lambda bi, i: (bi, i, cb(NSA_KV_WIDTH, OFF_KVS))),
            pl.BlockSpec((None, ts, NSA_KV_WIDTH), lambda bi, i: (bi, i, cb(NSA_KV_WIDTH, OFF_KVW))),
            pl.BlockSpec((None, ts, SMALL_W), lambda bi, i: (bi, i, cb(SMALL_W, OFF_SMALL))),
            tab, tab, tab],
        out_specs=[
            pl.BlockSpec((None, ts, NSA_WIDTH), lambda bi, i: (bi, i, 0)),
            pl.BlockSpec((None, NSA_KV_HEADS, ts, LANES), lambda bi, i: (bi, 0, i, 0)),
            pl.BlockSpec((None, ts, NSA_KV_WIDTH), lambda bi, i: (bi, i, 0)),
            pl.BlockSpec((None, ts, LANES), lambda bi, i: (bi, i, 0))],
        out_shape=[jax.ShapeDtypeStruct((b, s, NSA_WIDTH), BF16),
                   jax.ShapeDtypeStruct((b, NSA_KV_HEADS, s, LANES), BF16),
                   jax.ShapeDtypeStruct((b, s, NSA_KV_WIDTH), BF16),
                   jax.ShapeDtypeStruct((b, s, LANES), F32)],
        compiler_params=_cp(("parallel", "parallel")),
        name="nsa_prep",
    )(proj3, proj3, proj3, proj3, c, shi, slo)


def _gelu_tanh(x):
    return 0.5 * x * (1.0 + jnp.tanh(0.7978845608028654 * (x + 0.044715 * x * x * x)))


def _cmp_kernel(x_ref, pe_ref, w1_ref, b1_ref, w2_ref, c_ref, shi_ref, slo_ref, o_ref, *, rope):
    x = x_ref[...].astype(F32)
    n = x.shape[0]
    half = x.shape[1]
    xa = (x + pe_ref[0:1, :]).astype(BF16)
    xb = (x + pe_ref[1:2, :]).astype(BF16)
    ha = _dot(xa, w1_ref[0:half, :])
    hb = _dot(xb, w1_ref[half:2 * half, :])
    hid = ha + pltpu.roll(hb, n - 1, 0) + b1_ref[...]
    out = _dot(_gelu_tanh(hid).astype(BF16), w2_ref[...])
    if rope:
        out = _rope128(out, c_ref[...], shi_ref[...], slo_ref[...])
    o_ref[...] = out[:, 0:HEAD_DIM].astype(BF16)


def _compress(xu, pe, w1, b1, w2p, c, shi, slo, *, rope):
    b, nh, n, w = xu.shape
    fix2 = lambda bi, h: (0, 0)
    tab = pl.BlockSpec((None, n, LANES), lambda bi, h: (bi, 0, 0))
    return pl.pallas_call(
        functools.partial(_cmp_kernel, rope=rope),
        grid=(b, nh),
        in_specs=[pl.BlockSpec((None, None, n, w), lambda bi, h: (bi, h, 0, 0)),
                  pl.BlockSpec((2, w), fix2), pl.BlockSpec((2 * w, CMP_HIDDEN), fix2),
                  pl.BlockSpec((1, CMP_HIDDEN), fix2), pl.BlockSpec((CMP_HIDDEN, LANES), fix2),
                  tab, tab, tab],
        out_specs=pl.BlockSpec((None, None, n, HEAD_DIM), lambda bi, h: (bi, h, 0, 0)),
        out_shape=jax.ShapeDtypeStruct((b, nh, n, HEAD_DIM), BF16),
        compiler_params=_cp(("parallel", "parallel")),
        name="nsa_compress",
    )(xu, pe, w1, b1, w2p, c, shi, slo)


def _nsa_kernel(q_ref, g_ref, kc_ref, vc_ref, m_ref, ksa_ref, vs_ref, kw_ref, vw_ref, o_ref):
    tq = q_ref.shape[0]
    rows = NSA_GROUP * tq
    n_cmp = kc_ref.shape[1]
    n_slc = m_ref.shape[1]
    i = pl.program_id(1)
    t0 = i * tq
    tok = t0 + lax.broadcasted_iota(I32, (tq, 1), 0)
    tok_r = jnp.concatenate([tok] * NSA_GROUP, axis=0)
    loc_r = tok_r - t0
    cmp_end = lax.broadcasted_iota(I32, (rows, n_cmp), 1) * CMP_STRIDE + (CMP_BLOCK - 1)
    cmp_mask = cmp_end <= tok_r
    blk = lax.broadcasted_iota(I32, (tq, n_slc), 1)
    blk_f = blk.astype(F32)
    cur = tok // SLC_BLOCK
    forced = (blk == 0) | (blk == cur) | (blk == cur - 1)
    allowed = blk <= cur
    col_q = lax.broadcasted_iota(I32, (rows, tq), 1)
    diag_mask = col_q <= loc_r
    n_win = WINDOW + tq
    ws = pl.multiple_of(jnp.maximum(t0 - WINDOW, 0), tq)
    dwin = tok_r - (ws + lax.broadcasted_iota(I32, (rows, n_win), 1))
    win_mask = (dwin >= 0) & (dwin < WINDOW)
    cmp_to_slc = m_ref[...]

    for h in range(NSA_KV_HEADS):
        qh = jnp.concatenate(
            [q_ref[:, HEAD_DIM * (NSA_GROUP * h + g):HEAD_DIM * (NSA_GROUP * h + g + 1)]
             for g in range(NSA_GROUP)], axis=0)
        s = jnp.where(cmp_mask, _dot_nt(qh, kc_ref[h]), NEG_INF)
        mx = jnp.max(s, axis=1, keepdims=True)
        e = jnp.where(cmp_mask, jnp.exp(s - mx), 0.0)
        den = jnp.sum(e, axis=1, keepdims=True)
        p = e / jnp.maximum(den, 1e-30)
        o_c = _dot(p.astype(BF16), vc_ref[h])
        p4 = p[0:tq]
        for g in range(1, NSA_GROUP):
            p4 = p4 + p[g * tq:(g + 1) * tq]
        p_hi = p4.astype(BF16)
        p_lo = (p4 - p_hi.astype(F32)).astype(BF16)
        imp = _dot(p_hi, cmp_to_slc) + _dot(p_lo, cmp_to_slc)
        imp = jnp.where(forced, SEL_FORCE, jnp.where(allowed, imp, -SEL_FORCE))
        sel = jnp.zeros((tq, n_slc), F32)
        for _ in range(SLC_TOPN):
            best = jnp.max(imp, axis=1, keepdims=True)
            idx = jnp.min(jnp.where(imp == best, blk_f, float(n_slc)), axis=1, keepdims=True)
            hit = blk_f == idx
            sel = jnp.where(hit, 1.0, sel)
            imp = jnp.where(hit, VERY_NEG, imp)
        bias = jnp.where(sel > 0.5, 0.0, -MASK_BIG).astype(BF16)
        q_aug = jnp.concatenate([qh, jnp.concatenate([bias] * NSA_GROUP, axis=0)], axis=1)
        def sel_step(j, carry, diag):
            m_i, l_i, acc = carry
            k0 = pl.multiple_of(j * tq, tq)
            sc = _dot_nt(q_aug, ksa_ref[h, pl.ds(k0, tq), :])
            if diag:
                sc = jnp.where(diag_mask, sc, NEG_INF)
            m_new = jnp.maximum(m_i, jnp.max(sc, axis=1, keepdims=True))
            alpha = jnp.exp(m_i - m_new)
            pp = jnp.exp(sc - m_new)
            l_new = alpha * l_i + jnp.sum(pp, axis=1, keepdims=True)
            vt = vs_ref[pl.ds(k0, tq), HEAD_DIM * h:HEAD_DIM * (h + 1)]
            return m_new, l_new, alpha * acc + _dot(pp.astype(BF16), vt)

        carry = (jnp.full((rows, 1), VERY_NEG, F32), jnp.zeros((rows, 1), F32),
                 jnp.zeros((rows, HEAD_DIM), F32))
        carry = lax.fori_loop(0, i, lambda j, cr: sel_step(j, cr, False), carry)
        _, l_s, acc_s = sel_step(i, carry, True)
        o_s = acc_s / l_s
        kwt = kw_ref[pl.ds(ws, n_win), HEAD_DIM * h:HEAD_DIM * (h + 1)]
        vwt = vw_ref[pl.ds(ws, n_win), HEAD_DIM * h:HEAD_DIM * (h + 1)]
        sw = jnp.where(win_mask, _dot_nt(qh, kwt), NEG_INF)
        ew = jnp.exp(sw - jnp.max(sw, axis=1, keepdims=True))
        o_w = _dot(ew.astype(BF16), vwt) / jnp.sum(ew, axis=1, keepdims=True)
        for g in range(NSA_GROUP):
            head = NSA_GROUP * h + g
            rs = slice(g * tq, (g + 1) * tq)
            o = (g_ref[:, head:head + 1] * o_c[rs]
                 + g_ref[:, NSA_HEADS + head:NSA_HEADS + head + 1] * o_s[rs]
                 + g_ref[:, 2 * NSA_HEADS + head:2 * NSA_HEADS + head + 1] * o_w[rs])
            o_ref[:, HEAD_DIM * head:HEAD_DIM * (head + 1)] = o.astype(BF16)


def _cmp_slc_weights(n_cmp_pad, n_cmp, n_slc_pad):
    c0 = np.arange(n_cmp_pad)[:, None] * CMP_STRIDE
    s0 = np.arange(n_slc_pad)[None, :] * SLC_BLOCK
    shared = np.minimum(c0 + CMP_BLOCK, s0 + SLC_BLOCK) - np.maximum(c0, s0)
    w = np.maximum(shared, 0) / CMP_STRIDE
    w[n_cmp:] = 0.0
    return w.astype(np.float32)


def _nsa_attention(q_r, gates, kc, vc, ksa, kw_r, proj3):
    b, s, _ = q_r.shape
    tq = NSA_TQ
    n_cmp_pad = kc.shape[2]
    n_slc = HEAD_DIM
    assert SLC_TOPN <= s // SLC_BLOCK <= n_slc and s >= WINDOW + tq and s % tq == 0
    m = jnp.asarray(_cmp_slc_weights(n_cmp_pad, s // CMP_STRIDE - 1, n_slc), BF16)
    per_b3 = lambda bi, i: (bi, 0, 0)
    per_b4 = lambda bi, i: (bi, 0, 0, 0)
    cbv = lambda off: (off + NSA_KV_WIDTH) // NSA_KV_WIDTH
    return pl.pallas_call(
        _nsa_kernel,
        grid=(b, s // tq),
        in_specs=[
            pl.BlockSpec((None, tq, NSA_WIDTH), lambda bi, i: (bi, i, 0)),
            pl.BlockSpec((None, tq, LANES), lambda bi, i: (bi, i, 0)),
            pl.BlockSpec((None, NSA_KV_HEADS, n_cmp_pad, HEAD_DIM), per_b4),
            pl.BlockSpec((None, NSA_KV_HEADS, n_cmp_pad, HEAD_DIM), per_b4),
            pl.BlockSpec((n_cmp_pad, n_slc), lambda bi, i: (0, 0)),
            pl.BlockSpec((None, NSA_KV_HEADS, s, LANES), per_b4),
            pl.BlockSpec((None, s, NSA_KV_WIDTH), lambda bi, i: (bi, 0, cbv(OFF_KVS))),
            pl.BlockSpec((None, s, NSA_KV_WIDTH), per_b3),
            pl.BlockSpec((None, s, NSA_KV_WIDTH), lambda bi, i: (bi, 0, cbv(OFF_KVW)))],
        out_specs=pl.BlockSpec((None, tq, NSA_WIDTH), lambda bi, i: (bi, i, 0)),
        out_shape=jax.ShapeDtypeStruct((b, s, NSA_WIDTH), BF16),
        compiler_params=_cp(("parallel", "arbitrary")),
        name="nsa_attention",
    )(q_r, gates, kc, vc, m, ksa, proj3, kw_r, proj3)


def _softplus(x):
    return jnp.maximum(x, 0.0) + jnp.log(1.0 + jnp.exp(-jnp.abs(x)))


def _split3(x):
    h1 = x.astype(BF16)
    r1 = x - h1.astype(F32)
    h2 = r1.astype(BF16)
    h3 = (r1 - h2.astype(F32)).astype(BF16)
    return h1, h2, h3


def _seg_sum(x, bd):
    cols = []
    for j in range(x.shape[1] // LANES):
        xs = x[:, LANES * j:LANES * (j + 1)]
        h1, h2, h3 = _split3(xs)
        cols.append(_dot(h1, bd) + _dot(h2, bd) + _dot(h3, bd))
    return jnp.concatenate(cols, axis=1)


def _rwkv_kernel(*refs, has_vres):
    it = iter(refs)
    r_ref, k_ref, v_ref, sm_ref = next(it), next(it), next(it), next(it)
    vf_ref = next(it) if has_vres else None
    mu_ref, musm_ref, w0_ref, w2_ref, a0_ref, a2_ref, g2_ref = (next(it) for _ in range(7))
    kk_ref, ka_ref, rk_ref, lng_ref, lnb_ref = (next(it) for _ in range(5))
    if has_vres:
        v0_ref, v1_ref, v2_ref = next(it), next(it), next(it)
    tri_ref, bd_ref = next(it), next(it)
    y_ref, vo_ref = next(it), next(it)
    z_ref, prev_ref, prevsm_ref = next(it), next(it), next(it)

    c = r_ref.shape[0]
    width = r_ref.shape[1]

    @pl.when(pl.program_id(1) == 0)
    def _():
        z_ref[...] = jnp.zeros(z_ref.shape, F32)
        prev_ref[...] = jnp.zeros(prev_ref.shape, F32)
        prevsm_ref[...] = jnp.zeros(prevsm_ref.shape, F32)

    row = lax.broadcasted_iota(I32, (c, 1), 0)

    def shift_mix(x, pref, idx, mu):
        prev = pref[idx, 0:1, :]
        xs = jnp.where(row == 0, prev, pltpu.roll(x, 1, 0))
        pref[idx, 0:1, :] = x[c - 1:c, :]
        return x + (xs - x) * mu

    xr = shift_mix(r_ref[...].astype(F32), prev_ref, 0, mu_ref[0:1, :])
    xk = shift_mix(k_ref[...].astype(F32), prev_ref, 1, mu_ref[1:2, :])
    xv = shift_mix(v_ref[...].astype(F32), prev_ref, 2, mu_ref[2:3, :])
    sm = shift_mix(sm_ref[...].astype(F32), prevsm_ref, 0, musm_ref[...])

    wpre = w0_ref[...] + _dot(jnp.tanh(sm).astype(BF16), w2_ref[...])
    lw = -jnp.exp(-_softplus(-wpre) - 0.5)
    a = jax.nn.sigmoid(a0_ref[...] + _dot(sm.astype(BF16), a2_ref[...]))
    gate = _dot(jax.nn.sigmoid(sm).astype(BF16), g2_ref[...])
    if has_vres:
        lo = _dot(_dot(xv.astype(BF16), v1_ref[...]).astype(BF16), v2_ref[...])
        xv = xv + (vf_ref[...].astype(F32) - xv) * jax.nn.sigmoid(v0_ref[...] + lo)
    vo_ref[...] = xv.astype(BF16)

    bd = bd_ref[...]
    kk = xk * kk_ref[...]
    kk = kk / jnp.maximum(jnp.sqrt(_seg_sum(kk * kk, bd)), 1e-12)
    kmod = xk * (1.0 + (a - 1.0) * ka_ref[...])
    ah = -kk
    bh = kk * a

    tri = tri_ref[...]
    l1, l2, l3 = _split3(lw)
    cum = _dot(tri, l1) + _dot(tri, l2) + _dot(tri, l3)
    cumx = cum - lw
    cmid = cum[c // 2 - 1:c // 2, :]
    cend = cum[c - 1:c, :]
    e_inv = jnp.exp(cmid - cum)
    e_end = jnp.exp(cend - cum)
    a_t = (ah * jnp.exp(cumx - cmid)).astype(BF16)
    r_t = (xr * jnp.exp(cum - cmid)).astype(BF16)
    b_t = (bh * e_inv).astype(BF16)
    k_t = (kmod * e_inv).astype(BF16)
    a_0 = (ah * jnp.exp(cumx)).astype(BF16)
    r_0 = (xr * jnp.exp(cum)).astype(BF16)
    b_e = (bh * e_end).astype(BF16)
    k_e = (kmod * e_end).astype(BF16)
    w_end = jnp.exp(cend)
    bonus_in = xr * kmod * rk_ref[...]

    ri = lax.broadcasted_iota(I32, (c, c), 0)
    ci = lax.broadcasted_iota(I32, (c, c), 1)
    strict = ci < ri
    incl = ci <= ri
    n_levels = int(np.log2(c))
    assert 2 ** n_levels == c

    heads = range(width // RW_HEAD)
    sls = [slice(RW_HEAD * h, RW_HEAD * (h + 1)) for h in heads]
    zts = [z_ref[h] for h in heads]
    ztbs = [zt.astype(BF16) for zt in zts]
    vhs = [xv[:, sl] for sl in sls]
    vhbs = [vh.astype(BF16) for vh in vhs]
    m1s = [_dot_nt(jnp.concatenate([a_t[:, sl], r_t[:, sl]], axis=0),
                   jnp.concatenate([b_t[:, sl], k_t[:, sl]], axis=0)) for sl in sls]
    npows = [jnp.where(strict, m1[0:c, 0:c], 0.0).astype(BF16) for m1 in m1s]
    a_aks = [jnp.where(strict, m1[0:c, c:2 * c], 0.0).astype(BF16) for m1 in m1s]
    pqs = [jnp.concatenate([jnp.where(incl, m1[c:2 * c, 0:c], 0.0),
                            jnp.where(incl, m1[c:2 * c, c:2 * c], 0.0)], axis=1).astype(BF16) for m1 in m1s]
    us = [_dot_nt(a_0[:, sl], ztb) + _dot(a_ak, vhb)
          for sl, ztb, a_ak, vhb in zip(sls, ztbs, a_aks, vhbs)]
    y0s = [_dot_nt(r_0[:, sl], ztb) for sl, ztb in zip(sls, ztbs)]
    for lvl in range(n_levels):
        us = [u + _dot(npow, u.astype(BF16)) for u, npow in zip(us, npows)]
        if lvl + 1 < n_levels:
            npows = [_dot(npow, npow).astype(BF16) for npow in npows]
    uvs = [jnp.concatenate([u, vh], axis=0) for u, vh in zip(us, vhs)]
    ys = [y0 + _dot(pq, uv.astype(BF16)) for y0, pq, uv in zip(y0s, pqs, uvs)]
    for h in heads:
        sl = sls[h]
        bke = jnp.concatenate([b_e[:, sl], k_e[:, sl]], axis=0)
        z_ref[h] = zts[h] * w_end[:, sl] + _dot(uvs[h].T.astype(BF16), bke)
    for h in heads:
        sl = sls[h]
        y = ys[h]
        ym = jnp.mean(y, axis=1, keepdims=True)
        yd = y - ym
        yv = jnp.mean(yd * yd, axis=1, keepdims=True)
        yn = yd * lax.rsqrt(yv + RW_LNX_EPS) * lng_ref[:, sl] + lnb_ref[:, sl]
        bonus = jnp.sum(bonus_in[:, sl], axis=1, keepdims=True) * vhs[h]
        y_ref[:, sl] = ((yn + bonus) * gate[:, sl]).astype(BF16)


def _rwkv(proj3, vfirst, p, *, has_vres):
    b, s, _ = proj3.shape
    c = RW_CHUNK
    w = RW_WIDTH
    cb = lambda off: off // w
    fix = lambda bi, i: (0, 0)
    blk = lambda off: pl.BlockSpec((None, c, w), lambda bi, i: (bi, i, cb(off)))
    row_w = pl.BlockSpec((1, w), fix)
    in_specs = [blk(OFF_R), blk(OFF_K), blk(OFF_V),
                pl.BlockSpec((None, c, SMALL_W), lambda bi, i: (bi, i, OFF_SMALL // SMALL_W))]
    args = [proj3, proj3, proj3, proj3]
    if has_vres:
        in_specs.append(pl.BlockSpec((None, c, w), lambda bi, i: (bi, i, 0)))
        args.append(vfirst)
    in_specs += [pl.BlockSpec((3, w), fix), pl.BlockSpec((1, SMALL_W), fix), row_w,
                 pl.BlockSpec((SMALL_W, w), fix), row_w, pl.BlockSpec((SMALL_W, w), fix),
                 pl.BlockSpec((SMALL_W, w), fix), row_w, row_w, row_w, row_w, row_w]
    args += [p["mu"], p["mu_sm"], p["w0"], p["w2"], p["a0"], p["a2"], p["g2"],
             p["k_k"], p["k_a"], p["r_k"], p["lnx_g"], p["lnx_b"]]
    if has_vres:
        in_specs += [row_w, pl.BlockSpec((w, LANES), fix), pl.BlockSpec((LANES, w), fix)]
        args += [p["v0"], p["v1"], p["v2"]]
    tri = jnp.asarray(np.tril(np.ones((c, c), np.float32)), BF16)
    lane_head = np.arange(LANES) // RW_HEAD
    bd = jnp.asarray((lane_head[:, None] == lane_head[None, :]).astype(np.float32), BF16)
    in_specs += [pl.BlockSpec((c, c), fix), pl.BlockSpec((LANES, LANES), fix)]
    args += [tri, bd]
    out_blk = pl.BlockSpec((None, c, w), lambda bi, i: (bi, i, 0))
    return pl.pallas_call(
        functools.partial(_rwkv_kernel, has_vres=has_vres),
        grid=(b, s // c),
        in_specs=in_specs,
        out_specs=[out_blk, out_blk],
        out_shape=[jax.ShapeDtypeStruct((b, s, w), BF16), jax.ShapeDtypeStruct((b, s, w), BF16)],
        scratch_shapes=[pltpu.VMEM((w // RW_HEAD, RW_HEAD, RW_HEAD), F32),
                        pltpu.VMEM((3, 8, w), F32), pltpu.VMEM((1, 8, SMALL_W), F32)],
        compiler_params=_cp(("parallel", "arbitrary")),
        name="rwkv7",
    )(*args)


POOL_HALO = 16


def _pool_kernel(cur_ref, halo_ref, w_ref, sc_ref, o_ref):
    i = pl.program_id(1)
    tp = cur_ref.shape[0]
    cur = cur_ref[...].astype(F32)
    halo = jnp.where(i > 0, halo_ref[...].astype(F32), 0.0)
    ext = jnp.concatenate([halo, cur], axis=0)
    sums = {1: ext}
    win = 1
    while win < max(POOL_WINDOWS):
        sums[2 * win] = sums[win] + pltpu.roll(sums[win], win, 0)
        win *= 2
    count = (i * tp + 1 + lax.broadcasted_iota(I32, (tp, 1), 0)).astype(F32)
    for gi, win in enumerate(POOL_WINDOWS):
        sl = slice(POOL_GROUP * gi, POOL_GROUP * (gi + 1))
        mean = sums[win][POOL_HALO:, sl] / jnp.minimum(count, float(win))
        y = _dot((mean - cur[:, sl]).astype(BF16), w_ref[gi])
        o_ref[:, sl] = (y * sc_ref[:, sl]).astype(BF16)


def _pool(proj3, pool_w, pool_scale, *, tp=256):
    b, s, _ = proj3.shape
    w = POOL_WIDTH
    cb = OFF_POOL // w
    per = tp // POOL_HALO
    return pl.pallas_call(
        _pool_kernel,
        grid=(b, s // tp),
        in_specs=[pl.BlockSpec((None, tp, w), lambda bi, i: (bi, i, cb)),
                  pl.BlockSpec((None, POOL_HALO, w), lambda bi, i: (bi, jnp.maximum(i * per - 1, 0), cb)),
                  pl.BlockSpec((len(POOL_WINDOWS), POOL_GROUP, POOL_GROUP), lambda bi, i: (0, 0, 0)),
                  pl.BlockSpec((1, w), lambda bi, i: (0, 0))],
        out_specs=pl.BlockSpec((None, tp, w), lambda bi, i: (bi, i, 0)),
        out_shape=jax.ShapeDtypeStruct((b, s, w), BF16),
        compiler_params=_cp(("parallel", "parallel")),
        name="pool",
    )(proj3, proj3, pool_w, pool_scale.reshape(1, w))


def _merge_kernel(ya_ref, yb_ref, yc_ref, ga_ref, gb_ref, gc_ref, wb_ref, o_ref):
    acc = jax.nn.sigmoid(ga_ref[...].astype(F32)) * _dot(ya_ref[...], wb_ref[0])
    acc = acc + jax.nn.sigmoid(gb_ref[...].astype(F32)) * _dot(yb_ref[...], wb_ref[1])
    acc = acc + jax.nn.sigmoid(gc_ref[...].astype(F32)) * _dot(yc_ref[...], wb_ref[2])
    o_ref[...] = acc.astype(BF16)


def _merge(ya, yb, yc, proj, wb, *, tm=256):
    t, w = ya.shape
    d = wb.shape[2]
    row = lambda i: (i, 0)
    gate = lambda k: pl.BlockSpec((tm, d), lambda i: (i, k))
    return pl.pallas_call(
        _merge_kernel,
        grid=(t // tm,),
        in_specs=[pl.BlockSpec((tm, w), row)] * 3 + [gate(0), gate(1), gate(2),
                                                      pl.BlockSpec((N_BRANCH, w, d), lambda i: (0, 0, 0))],
        out_specs=pl.BlockSpec((tm, d), row),
        out_shape=jax.ShapeDtypeStruct((t, d), BF16),
        compiler_params=_cp(("parallel",)),
        name="merge",
    )(ya, yb, yc, proj, proj, proj, wb)


def _xattn_kernel(h16_ref, h32_ref, wq_ref, mkv_ref, wo_ref, g_ref, b_ref, o32_ref, o16_ref, *, alpha):
    q = _dot(h16_ref[...], wq_ref[...])
    scale = XA_HEAD_DIM ** -0.5
    outs = []
    for hd in range(XA_HEADS):
        sl = slice(XA_HEAD_DIM * hd, XA_HEAD_DIM * (hd + 1))
        slv = slice(XA_WIDTH + XA_HEAD_DIM * hd, XA_WIDTH + XA_HEAD_DIM * (hd + 1))
        s = _dot_nt(q[:, sl].astype(BF16), mkv_ref[:, sl]) * scale
        e = jnp.exp(s - jnp.max(s, axis=1, keepdims=True))
        p = e / jnp.sum(e, axis=1, keepdims=True)
        outs.append(_dot(p.astype(BF16), mkv_ref[:, slv]))
    o = jnp.concatenate(outs, axis=1).astype(BF16)
    y = alpha * h32_ref[...] + _dot(o, wo_ref[...])
    out = _layer_norm_rows(y, g_ref[...], b_ref[...])
    o32_ref[...] = out
    o16_ref[...] = out.astype(BF16)


def _xattn(h16, h32, wq, memkv, wo, g, b, *, alpha, seq, tm=256):
    t, d = h16.shape
    per_b = seq // tm
    nm = memkv.shape[1]
    row = lambda i: (i, 0)
    fix = lambda i: (0, 0)
    return pl.pallas_call(
        functools.partial(_xattn_kernel, alpha=alpha),
        grid=(t // tm,),
        in_specs=[pl.BlockSpec((tm, d), row), pl.BlockSpec((tm, d), row),
                  pl.BlockSpec((d, XA_WIDTH), fix),
                  pl.BlockSpec((None, nm, 2 * XA_WIDTH), lambda i: (i // per_b, 0, 0)),
                  pl.BlockSpec((XA_WIDTH, d), fix), pl.BlockSpec((1, d), fix), pl.BlockSpec((1, d), fix)],
        out_specs=[pl.BlockSpec((tm, d), row), pl.BlockSpec((tm, d), row)],
        out_shape=[jax.ShapeDtypeStruct((t, d), F32), jax.ShapeDtypeStruct((t, d), BF16)],
        compiler_params=_cp(("parallel",)),
        name="xattn",
    )(h16, h32, wq, memkv, wo, g.reshape(1, d), b.reshape(1, d))


def _ffn_kernel(*refs, ln, alpha):
    if ln:
        _, x_ref, wg_ref, wu_ref, wd_ref, res_ref, g_ref, b_ref, o32_ref, o16_ref, acc_ref = refs
    else:
        _, x_ref, wg_ref, wu_ref, wd_ref, o_ref, acc_ref = refs
    f = pl.program_id(1)

    @pl.when(f == 0)
    def _():
        acc_ref[...] = jnp.zeros(acc_ref.shape, F32)

    x = x_ref[...].astype(BF16)
    gt = _dot(x, wg_ref[0])
    up = _dot(x, wu_ref[0])
    act = (gt * jax.nn.sigmoid(gt) * up).astype(BF16)
    acc_ref[...] += _dot(act, wd_ref[0])

    @pl.when(f == pl.num_programs(1) - 1)
    def _():
        if ln:
            out = _layer_norm_rows(alpha * res_ref[...] + acc_ref[...], g_ref[...], b_ref[...])
            o32_ref[...] = out
            o16_ref[...] = out.astype(BF16)
        else:
            o_ref[...] = acc_ref[...]


def _ffn(block_expert, x, wg, wu, wd, *, tm, tf, res=None, g=None, b=None, alpha=1.0):
    n, d = x.shape
    ff = wg.shape[2]
    ln = res is not None
    row = lambda i, f, be: (i, 0)
    fix = lambda i, f, be: (0, 0)
    in_specs = [pl.BlockSpec((tm, d), row),
                pl.BlockSpec((1, d, tf), lambda i, f, be: (be[i], 0, f)),
                pl.BlockSpec((1, d, tf), lambda i, f, be: (be[i], 0, f)),
                pl.BlockSpec((1, tf, d), lambda i, f, be: (be[i], f, 0))]
    args = [x, wg, wu, wd]
    if ln:
        in_specs += [pl.BlockSpec((tm, d), row), pl.BlockSpec((1, d), fix), pl.BlockSpec((1, d), fix)]
        args += [res, g.reshape(1, d), b.reshape(1, d)]
        out_specs = [pl.BlockSpec((tm, d), row), pl.BlockSpec((tm, d), row)]
        out_shape = [jax.ShapeDtypeStruct((n, d), F32), jax.ShapeDtypeStruct((n, d), BF16)]
    else:
        out_specs = pl.BlockSpec((tm, d), row)
        out_shape = jax.ShapeDtypeStruct((n, d), F32)
    return pl.pallas_call(
        functools.partial(_ffn_kernel, ln=ln, alpha=alpha),
        grid_spec=pltpu.PrefetchScalarGridSpec(
            num_scalar_prefetch=1, grid=(n // tm, ff // tf),
            in_specs=in_specs, out_specs=out_specs,
            scratch_shapes=[pltpu.VMEM((tm, d), F32)]),
        out_shape=out_shape,
        compiler_params=_cp(("parallel", "arbitrary")),
        name="swiglu",
    )(block_expert, *args)


def _router_kernel(x_ref, w_ref, o_ref):
    logits = _dot(x_ref[...], w_ref[...])
    lane = lax.broadcasted_iota(I32, logits.shape, 1)
    lane_f = lane.astype(F32)
    l1 = jnp.where(lane < N_EXPERTS, logits, VERY_NEG)
    m1 = jnp.max(l1, axis=1, keepdims=True)
    i1 = jnp.min(jnp.where(l1 == m1, lane_f, float(LANES)), axis=1, keepdims=True)
    l2 = jnp.where(lane_f == i1, VERY_NEG, l1)
    m2 = jnp.max(l2, axis=1, keepdims=True)
    i2 = jnp.min(jnp.where(l2 == m2, lane_f, float(LANES)), axis=1, keepdims=True)
    e = jnp.exp(m2 - m1)
    g1 = 1.0 / (1.0 + e)
    g2 = e / (1.0 + e)
    o_ref[...] = jnp.where(lane == 0, i1, jnp.where(lane == 1, i2, jnp.where(lane == 2, g1,
                           jnp.where(lane == 3, g2, 0.0))))


def _router(h16, wr, *, tm=512):
    t, d = h16.shape
    return pl.pallas_call(
        _router_kernel,
        grid=(t // tm,),
        in_specs=[pl.BlockSpec((tm, d), lambda i: (i, 0)), pl.BlockSpec((d, LANES), lambda i: (0, 0))],
        out_specs=pl.BlockSpec((tm, LANES), lambda i: (i, 0)),
        out_shape=jax.ShapeDtypeStruct((t, LANES), F32),
        compiler_params=_cp(("parallel",)),
        name="router",
    )(h16, wr)


GATHER_ROWS = 256


def _gather_kernel(idx_ref, src_ref, o_ref, sem):
    n = o_ref.shape[0]

    def row_copy(r, src_row):
        return pltpu.make_async_copy(src_ref.at[pl.ds(src_row, 1)], o_ref.at[pl.ds(r, 1)], sem)

    def start(r, carry):
        row_copy(r, idx_ref[0, r]).start()
        return carry

    def wait(r, carry):
        row_copy(r, 0).wait()
        return carry

    lax.fori_loop(0, n, start, 0)
    lax.fori_loop(0, n, wait, 0)


def _gather_rows(src, idx):
    n = idx.shape[0]
    d = src.shape[1]
    rows = GATHER_ROWS
    return pl.pallas_call(
        _gather_kernel,
        grid=(n // rows,),
        in_specs=[pl.BlockSpec((None, 1, rows), lambda i: (i, 0, 0), memory_space=pltpu.SMEM),
                  pl.BlockSpec(memory_space=pl.ANY)],
        out_specs=pl.BlockSpec((rows, d), lambda i: (i, 0)),
        out_shape=jax.ShapeDtypeStruct((n, d), src.dtype),
        scratch_shapes=[pltpu.SemaphoreType.DMA(())],
        compiler_params=_cp(("arbitrary",)),
        name="gather_rows",
    )(idx.reshape(n // rows, 1, rows), src)


def _combine_kernel(y_ref, rt_ref, res_ref, g_ref, b_ref, o_ref, *, alpha):
    d = res_ref.shape[1]
    f = y_ref[:, 0:d] * rt_ref[:, 2:3] + y_ref[:, d:2 * d] * rt_ref[:, 3:4]
    o_ref[...] = _layer_norm_rows(alpha * res_ref[...] + f, g_ref[...], b_ref[...])


def _combine_ln(y2, rt, res, g, b, *, alpha, tm=256):
    t, d = res.shape
    row = lambda i: (i, 0)
    fix = lambda i: (0, 0)
    return pl.pallas_call(
        functools.partial(_combine_kernel, alpha=alpha),
        grid=(t // tm,),
        in_specs=[pl.BlockSpec((tm, 2 * d), row), pl.BlockSpec((tm, LANES), row), pl.BlockSpec((tm, d), row),
                  pl.BlockSpec((1, d), fix), pl.BlockSpec((1, d), fix)],
        out_specs=pl.BlockSpec((tm, d), row),
        out_shape=jax.ShapeDtypeStruct((t, d), F32),
        compiler_params=_cp(("parallel",)),
        name="moe_combine_ln",
    )(y2, rt, res, g.reshape(1, d), b.reshape(1, d))


def _moe(h16, h32, router, w_gate, w_up, w_down, g, b, *, alpha):
    t, d = h32.shape
    tm = MOE_TM
    wr = jnp.pad(router, ((0, 0), (0, LANES - N_EXPERTS))).astype(BF16)
    rt = _router(h16, wr)
    expert = rt[:, 0:TOP_K].astype(I32).reshape(-1)
    n_assign = t * TOP_K
    onehot = (expert[:, None] == jnp.arange(N_EXPERTS, dtype=I32)[None, :]).astype(I32)
    csum = jnp.cumsum(onehot, axis=0)
    rank = jnp.sum(csum * onehot, axis=1) - 1
    counts = csum[-1]
    padded = (counts + tm - 1) // tm * tm
    pad_end = jnp.cumsum(padded)
    pad_start = pad_end - padded
    dest = (pad_start[expert] + rank).astype(I32)
    n_pad = n_assign + N_EXPERTS * tm
    n_blocks = n_pad // tm
    tok_pad = jnp.zeros((n_pad,), I32).at[dest].set(jnp.arange(n_assign, dtype=I32) // TOP_K)
    block_expert = jnp.minimum(
        jnp.searchsorted(pad_end, jnp.arange(n_blocks, dtype=I32) * tm, side="right"),
        N_EXPERTS - 1).astype(I32)
    x_sorted = _gather_rows(h32, tok_pad)
    y_sorted = _ffn(block_expert, x_sorted, w_gate, w_up, w_down, tm=tm, tf=512)
    y_tok = _gather_rows(y_sorted, dest)
    return _combine_ln(y_tok.reshape(t, TOP_K * d), rt, h32, g, b, alpha=alpha)


def _perm_w_in(w, d_model):
    sizes = ((NSA_WIDTH,) + (NSA_KV_WIDTH,) * 6 + (N_BRANCH * NSA_HEADS,)
             + (RW_WIDTH, RW_WIDTH, RW_WIDTH, RW_DECAY_LORA, RW_A_LORA, RW_GATE_LORA)
             + (POOL_WIDTH,) + (d_model,) * N_BRANCH)
    offs = np.concatenate([[0], np.cumsum(sizes)])
    seg = [w[:, int(offs[i]):int(offs[i + 1])] for i in range(len(sizes))]
    (q, kc, vc, ks, vs, kw, vw, ng, r, k, v, wl, al, gl, pool, ga, gb, gc) = seg
    zeros = lambda n: jnp.zeros((w.shape[0], n), w.dtype)
    cols = [ga, gb, gc, q, r, k, v, pool, kc, vc, ks, vs, kw, vw,
            ng, zeros(SM_WL - N_BRANCH * NSA_HEADS), wl, al, gl, zeros(SMALL_W - SM_GL - RW_GATE_LORA)]
    out = jnp.concatenate(cols, axis=1).astype(BF16)
    assert out.shape[1] == NP_COLS and d_model == 2048
    return out


def _rwkv_params(l, mu, w0, w2, a0, a2, g2, k_k, k_a, r_k, lnx_g, lnx_b, v0, v1, v2):
    w = RW_WIDTH
    rowv = lambda a: a.reshape(1, w).astype(F32)
    m = mu[l]
    o = np.cumsum((0, w, w, w, RW_DECAY_LORA, RW_A_LORA, RW_GATE_LORA))
    mu_sm = jnp.zeros((1, SMALL_W), F32)
    mu_sm = mu_sm.at[0, SM_WL:SM_WL + RW_DECAY_LORA].set(m[o[3]:o[4]])
    mu_sm = mu_sm.at[0, SM_AL:SM_AL + RW_A_LORA].set(m[o[4]:o[5]])
    mu_sm = mu_sm.at[0, SM_GL:SM_GL + RW_GATE_LORA].set(m[o[5]:o[6]])

    def pad_rows(a, off):
        return jnp.zeros((SMALL_W, w), F32).at[off:off + a.shape[0]].set(a).astype(BF16)

    p = dict(mu=m[0:3 * w].reshape(3, w), mu_sm=mu_sm, w0=rowv(w0[l]), w2=pad_rows(w2[l], SM_WL),
             a0=rowv(a0[l]), a2=pad_rows(a2[l], SM_AL), g2=pad_rows(g2[l], SM_GL),
             k_k=rowv(k_k[l]), k_a=rowv(k_a[l]), r_k=rowv(r_k[l]), lnx_g=rowv(lnx_g[l]), lnx_b=rowv(lnx_b[l]))
    if l > 0:
        p["v0"] = rowv(v0[l - 1])
        p["v1"] = jnp.pad(v1[l - 1], ((0, 0), (0, LANES - RW_V_LORA))).astype(BF16)
        p["v2"] = jnp.pad(v2[l - 1], ((0, LANES - RW_V_LORA), (0, 0))).astype(BF16)
    return p


def _stride_units(x2d, b, s):
    x = x2d.reshape(b, s // CMP_STRIDE, CMP_STRIDE, NSA_KV_HEADS, HEAD_DIM)
    return x.transpose(0, 3, 1, 2, 4).reshape(b, NSA_KV_HEADS, s // CMP_STRIDE, CMP_STRIDE * HEAD_DIM)


def _token_mixer(l, h16, proj, b, s, tabs, tabs_c, prm, vfirst):
    d_model = h16.shape[1]
    t = b * s
    proj3 = proj.reshape(b, s, NP_COLS)
    c, shi, slo = tabs
    q_r, ksa, kw_r, gates = _nsa_prep(proj3, c, shi, slo)
    n_units = s // CMP_STRIDE
    half = CMP_STRIDE * HEAD_DIM
    kvc = proj[:, OFF_KVC:OFF_KVC + 2 * NSA_KV_WIDTH]
    cmp_out = []
    for j in range(2):
        xu = _stride_units(kvc[:, NSA_KV_WIDTH * j:NSA_KV_WIDTH * (j + 1)], b, s)
        pe = prm["nsa_cmp_pe"][l, j].reshape(2, half)
        w1 = prm["nsa_cmp_w1"][l, j].astype(BF16)
        b1 = prm["nsa_cmp_b1"][l, j].reshape(1, CMP_HIDDEN)
        w2p = jnp.pad(prm["nsa_cmp_w2"][l, j], ((0, 0), (0, LANES - HEAD_DIM))).astype(BF16)
        cmp_out.append(_compress(xu, pe, w1, b1, w2p, *tabs_c, rope=(j == 0)))
    kc, vc = cmp_out
    y_a = _nsa_attention(q_r, gates, kc, vc, ksa, kw_r, proj3).reshape(t, NSA_WIDTH)
    rp = _rwkv_params(l, prm["rwkv_mu"], prm["rwkv_w0"], prm["rwkv_w2"], prm["rwkv_a0"], prm["rwkv_a2"],
                      prm["rwkv_g2"], prm["rwkv_k_k"], prm["rwkv_k_a"], prm["rwkv_r_k"], prm["rwkv_lnx_g"],
                      prm["rwkv_lnx_b"], prm["rwkv_v0"], prm["rwkv_v1"], prm["rwkv_v2"])
    y_b, v_l = _rwkv(proj3, vfirst, rp, has_vres=(l > 0))
    y_c = _pool(proj3, prm["pool_w"][l].astype(BF16), prm["pool_scale"][l])
    merged = _merge(y_a, y_b.reshape(t, RW_WIDTH), y_c.reshape(t, POOL_WIDTH), proj,
                    prm["w_branch"][l].astype(BF16))
    return merged, v_l


def kernel(x, mem, positions, w_in, nsa_cmp_pe, nsa_cmp_w1, nsa_cmp_b1, nsa_cmp_w2, rwkv_mu, rwkv_w0, rwkv_w2, rwkv_a0, rwkv_a2, rwkv_g2, rwkv_k_k, rwkv_k_a, rwkv_r_k, rwkv_lnx_g, rwkv_lnx_b, rwkv_v0, rwkv_v1, rwkv_v2, pool_w, pool_scale, w_branch, w_out, mem_ln_g, mem_ln_b, mem_wkv, xa_wq, xa_wo, ln_g, ln_b, ffn_w_gate, ffn_w_up, ffn_w_down, moe_router, moe_w_gate, moe_w_up, moe_w_down):
    prm = dict(nsa_cmp_pe=nsa_cmp_pe, nsa_cmp_w1=nsa_cmp_w1, nsa_cmp_b1=nsa_cmp_b1, nsa_cmp_w2=nsa_cmp_w2,
               rwkv_mu=rwkv_mu, rwkv_w0=rwkv_w0, rwkv_w2=rwkv_w2, rwkv_a0=rwkv_a0, rwkv_a2=rwkv_a2,
               rwkv_g2=rwkv_g2, rwkv_k_k=rwkv_k_k, rwkv_k_a=rwkv_k_a, rwkv_r_k=rwkv_r_k,
               rwkv_lnx_g=rwkv_lnx_g, rwkv_lnx_b=rwkv_lnx_b, rwkv_v0=rwkv_v0, rwkv_v1=rwkv_v1,
               rwkv_v2=rwkv_v2, pool_w=pool_w, pool_scale=pool_scale, w_branch=w_branch)
    b, s, d = x.shape
    t = b * s
    depth = w_in.shape[0]
    alpha = (2 * depth) ** 0.25
    n_mem = mem.shape[1]

    inv_freq = ROPE_THETA ** (-jnp.arange(ROPE_HALF, dtype=F32) / ROPE_HALF)
    dd = np.arange(LANES) % HEAD_DIM
    invf_lane = jnp.where(jnp.asarray(dd < ROPE_DIM), inv_freq[jnp.asarray(dd % ROPE_HALF)], 0.0).reshape(1, LANES)
    tabs = tuple(a.reshape(b, s, LANES) for a in _rope_tables(positions.reshape(t, 1), invf_lane, ts=512))
    n_units = s // CMP_STRIDE
    pos_c = positions[:, CMP_BLOCK - 1::CMP_STRIDE]
    pos_c = jnp.concatenate([pos_c, pos_c[:, -1:]], axis=1)
    tabs_c = tuple(a.reshape(b, n_units, LANES)
                   for a in _rope_tables(pos_c.reshape(b * n_units, 1), invf_lane, ts=n_units))

    memkv = _memkv(mem.reshape(b * n_mem, d), mem_ln_g, mem_ln_b, mem_wkv.astype(BF16))
    memkv = memkv.reshape(b, n_mem, 2 * XA_WIDTH)

    h32 = x.reshape(t, d)
    h16 = h32.astype(BF16)
    vfirst = None
    for l in range(depth):
        proj = _matmul(h16, _perm_w_in(w_in[l], d), tm=512, tn=1024, out_dtype=BF16)
        merged, v_l = _token_mixer(l, h16, proj, b, s, tabs, tabs_c, prm, vfirst)
        if l == 0:
            vfirst = v_l
        h32, h16 = _matmul_res_ln(merged, w_out[l].astype(BF16), h32, ln_g[l, 0], ln_b[l, 0], alpha=alpha)
        h32, h16 = _xattn(h16, h32, xa_wq[l].astype(BF16), memkv, xa_wo[l].astype(BF16),
                          ln_g[l, 1], ln_b[l, 1], alpha=alpha, seq=s)
        if l % 2 == 0:
            e = l // 2
            be = jnp.zeros((t // 512,), I32)
            h32, h16 = _ffn(be, h16, ffn_w_gate[e:e + 1].astype(BF16), ffn_w_up[e:e + 1].astype(BF16),
                            ffn_w_down[e:e + 1].astype(BF16), tm=512, tf=512,
                            res=h32, g=ln_g[l, 2], b=ln_b[l, 2], alpha=alpha)
        else:
            e = l // 2
            h32 = _moe(h16, h32, moe_router[e], moe_w_gate[e].astype(BF16), moe_w_up[e].astype(BF16),
                       moe_w_down[e].astype(BF16), ln_g[l, 2], ln_b[l, 2], alpha=alpha)
            h16 = h32.astype(BF16)
    return h32.reshape(b, s, d)
```

```python
import functools

import numpy as np
import jax
import jax.numpy as jnp
from jax import lax
from jax.experimental import pallas as pl
from jax.experimental.pallas import tpu as pltpu

F32 = jnp.float32
BF16 = jnp.bfloat16
I32 = jnp.int32

HEAD_DIM = 64
ROPE_DIM = HEAD_DIM // 4
ROPE_HALF = ROPE_DIM // 2
ROPE_THETA = 500000.0
NSA_HEADS = 16
NSA_KV_HEADS = 4
NSA_GROUP = NSA_HEADS // NSA_KV_HEADS
NSA_WIDTH = NSA_HEADS * HEAD_DIM
NSA_KV_WIDTH = NSA_KV_HEADS * HEAD_DIM
CMP_BLOCK = 32
CMP_STRIDE = 16
CMP_HIDDEN = 256
SLC_BLOCK = 64
SLC_TOPN = 8
WINDOW = 512
SEL_FORCE = 1e9
RW_HEADS = 16
RW_HEAD = 64
RW_WIDTH = RW_HEADS * RW_HEAD
RW_DECAY_LORA = 64
RW_A_LORA = 64
RW_V_LORA = 32
RW_GATE_LORA = 160
RW_LNX_EPS = 64e-5
POOL_WINDOWS = (2, 4, 8, 16)
POOL_GROUP = 256
POOL_WIDTH = POOL_GROUP * len(POOL_WINDOWS)
XA_HEADS = 4
XA_HEAD_DIM = 128
XA_WIDTH = XA_HEADS * XA_HEAD_DIM
N_EXPERTS = 8
TOP_K = 2
N_BRANCH = 3
LN_EPS = 1e-5
NEG_INF = -1e30
VERY_NEG = -3e38
MASK_BIG = 2.0 ** 100

LANES = 128
VMEM_LIMIT = 56 * 1024 * 1024

NP_COLS = 13312
OFF_GATES = 0
OFF_Q = 6144
OFF_R = 7168
OFF_K = 8192
OFF_V = 9216
OFF_POOL = 10240
OFF_KVC = 11264
OFF_KVS = 11776
OFF_KVW = 12288
OFF_SMALL = 12800
SMALL_W = 512
SM_WL = 128
SM_AL = 192
SM_GL = 256

NSA_TQ = 128
NSA_TK = 256
RW_CHUNK = 64
MOE_TM = 512


def _cp(sem, vmem=VMEM_LIMIT):
    return pltpu.CompilerParams(dimension_semantics=sem, vmem_limit_bytes=vmem)


def _dot(a, b):
    return jnp.dot(a, b, preferred_element_type=F32)


def _dot_nt(a, b):
    return lax.dot_general(a, b, (((1,), (1,)), ((), ())), preferred_element_type=F32)


def _layer_norm_rows(y, g, b):
    mu = jnp.mean(y, axis=-1, keepdims=True)
    d = y - mu
    var = jnp.mean(d * d, axis=-1, keepdims=True)
    return d * lax.rsqrt(var + LN_EPS) * g + b


def _mm_kernel(x_ref, w_ref, o_ref):
    o_ref[...] = _dot(x_ref[...], w_ref[...]).astype(o_ref.dtype)


def _matmul(x, w, *, tm, tn, out_dtype):
    m, k = x.shape
    n = w.shape[1]
    return pl.pallas_call(
        _mm_kernel,
        grid=(n // tn, m // tm),
        in_specs=[pl.BlockSpec((tm, k), lambda j, i: (i, 0)),
                  pl.BlockSpec((k, tn), lambda j, i: (0, j))],
        out_specs=pl.BlockSpec((tm, tn), lambda j, i: (i, j)),
        out_shape=jax.ShapeDtypeStruct((m, n), out_dtype),
        compiler_params=_cp(("parallel", "parallel")),
        name="matmul",
    )(x, w)


def _mm_ln_kernel(x_ref, w_ref, res_ref, g_ref, b_ref, o32_ref, o16_ref, *, alpha):
    y = alpha * res_ref[...] + _dot(x_ref[...], w_ref[...])
    out = _layer_norm_rows(y, g_ref[...], b_ref[...])
    o32_ref[...] = out
    o16_ref[...] = out.astype(BF16)


def _matmul_res_ln(x, w, res, g, b, *, alpha, tm=256):
    m, k = x.shape
    n = w.shape[1]
    row = lambda i: (i, 0)
    fix = lambda i: (0, 0)
    return pl.pallas_call(
        functools.partial(_mm_ln_kernel, alpha=alpha),
        grid=(m // tm,),
        in_specs=[pl.BlockSpec((tm, k), row), pl.BlockSpec((k, n), fix), pl.BlockSpec((tm, n), row),
                  pl.BlockSpec((1, n), fix), pl.BlockSpec((1, n), fix)],
        out_specs=[pl.BlockSpec((tm, n), row), pl.BlockSpec((tm, n), row)],
        out_shape=[jax.ShapeDtypeStruct((m, n), F32), jax.ShapeDtypeStruct((m, n), BF16)],
        compiler_params=_cp(("parallel",)),
        name="matmul_res_ln",
    )(x, w, res, g.reshape(1, n), b.reshape(1, n))


def _memkv_kernel(m_ref, g_ref, b_ref, w_ref, o_ref):
    xn = _layer_norm_rows(m_ref[...], g_ref[...], b_ref[...])
    o_ref[...] = _dot(xn.astype(BF16), w_ref[...]).astype(o_ref.dtype)


def _memkv(mem2d, g, b, w, *, tm=256):
    m, d = mem2d.shape
    n = w.shape[1]
    row = lambda i: (i, 0)
    fix = lambda i: (0, 0)
    return pl.pallas_call(
        _memkv_kernel,
        grid=(m // tm,),
        in_specs=[pl.BlockSpec((tm, d), row), pl.BlockSpec((1, d), fix), pl.BlockSpec((1, d), fix),
                  pl.BlockSpec((d, n), fix)],
        out_specs=pl.BlockSpec((tm, n), row),
        out_shape=jax.ShapeDtypeStruct((m, n), BF16),
        compiler_params=_cp(("parallel",)),
        name="memkv",
    )(mem2d, g.reshape(1, d), b.reshape(1, d), w)


def _rope_table_kernel(pos_ref, invf_ref, c_ref, shi_ref, slo_ref):
    ang = pos_ref[...].astype(F32) * invf_ref[...]
    d = lax.broadcasted_iota(I32, ang.shape, 1) % HEAD_DIM
    cs = jnp.cos(ang)
    sn = jnp.sin(ang)
    c_ref[...] = jnp.where(d < ROPE_DIM, cs, 1.0)
    shi_ref[...] = jnp.where((d >= ROPE_HALF) & (d < ROPE_DIM), sn, 0.0)
    slo_ref[...] = jnp.where(d < ROPE_HALF, -sn, 0.0)


def _rope_tables(pos_col, invf_lane, *, ts):
    n = pos_col.shape[0]
    row = lambda i: (i, 0)
    out = jax.ShapeDtypeStruct((n, LANES), F32)
    return pl.pallas_call(
        _rope_table_kernel,
        grid=(n // ts,),
        in_specs=[pl.BlockSpec((ts, 1), row), pl.BlockSpec((1, LANES), lambda i: (0, 0))],
        out_specs=[pl.BlockSpec((ts, LANES), row)] * 3,
        out_shape=[out, out, out],
        compiler_params=_cp(("parallel",)),
        name="rope_tables",
    )(pos_col, invf_lane)


def _rope128(x, c, shi, slo):
    return x * c + pltpu.roll(x, ROPE_HALF, 1) * shi + pltpu.roll(x, LANES - ROPE_HALF, 1) * slo


def _nsa_prep_kernel(q_ref, ks_ref, vs_ref, kw_ref, vw_ref, sm_ref, c_ref, shi_ref, slo_ref,
                     qt_ref, ksa_ref, vst_ref, kwh_ref, vwt_ref, gt_ref):
    c = c_ref[...]
    shi = shi_ref[...]
    slo = slo_ref[...]
    ts = c.shape[0]
    scale = HEAD_DIM ** -0.5
    for j in range(NSA_WIDTH // LANES):
        sl = slice(LANES * j, LANES * (j + 1))
        qt_ref[sl, :] = (_rope128(q_ref[:, sl].astype(F32), c, shi, slo) * scale).T.astype(BF16)
    t = pl.program_id(1) * ts + lax.broadcasted_iota(I32, (ts, LANES), 0)
    lane = lax.broadcasted_iota(I32, (ts, LANES), 1)
    onehot = jnp.where(lane - HEAD_DIM == t // SLC_BLOCK, 1.0, 0.0)
    lo = lane < HEAD_DIM
    for j in range(NSA_KV_WIDTH // LANES):
        sl = slice(LANES * j, LANES * (j + 1))
        ks = _rope128(ks_ref[:, sl].astype(F32), c, shi, slo)
        kw = _rope128(kw_ref[:, sl].astype(F32), c, shi, slo).astype(BF16)
        ksa_ref[2 * j] = jnp.where(lo, ks, onehot).astype(BF16)
        ksa_ref[2 * j + 1] = jnp.where(lo, pltpu.roll(ks, HEAD_DIM, 1), onehot).astype(BF16)
        kwh_ref[2 * j] = kw[:, 0:HEAD_DIM]
        kwh_ref[2 * j + 1] = kw[:, HEAD_DIM:LANES]
    for k in range(ts // LANES):
        rs = slice(LANES * k, LANES * (k + 1))
        vst_ref[k] = vs_ref[rs, :].astype(F32).T.astype(BF16)
        vwt_ref[k] = vw_ref[rs, :].astype(F32).T.astype(BF16)
    gt_ref[...] = jax.nn.sigmoid(sm_ref[:, 0:LANES].astype(F32)).T


def _nsa_prep(proj3, c, shi, slo, *, ts=512):
    b, s, _ = proj3.shape
    kvw = NSA_KV_WIDTH
    tab = pl.BlockSpec((None, ts, LANES), lambda bi, i: (bi, i, 0))
    kv = lambda off: pl.BlockSpec((None, ts, kvw), lambda bi, i: (bi, i, off // kvw))
    vt_spec = pl.BlockSpec((None, ts // LANES, kvw, LANES), lambda bi, i: (bi, i, 0, 0))
    vt_shape = jax.ShapeDtypeStruct((b, s // LANES, kvw, LANES), BF16)
    return pl.pallas_call(
        _nsa_prep_kernel,
        grid=(b, s // ts),
        in_specs=[
            pl.BlockSpec((None, ts, NSA_WIDTH), lambda bi, i: (bi, i, OFF_Q // NSA_WIDTH)),
            kv(OFF_KVS), kv(OFF_KVS + kvw), kv(OFF_KVW), kv(OFF_KVW + kvw),
            pl.BlockSpec((None, ts, SMALL_W), lambda bi, i: (bi, i, OFF_SMALL // SMALL_W)),
            tab, tab, tab],
        out_specs=[
            pl.BlockSpec((None, NSA_WIDTH, ts), lambda bi, i: (bi, 0, i)),
            pl.BlockSpec((None, NSA_KV_HEADS, ts, LANES), lambda bi, i: (bi, 0, i, 0)),
            vt_spec,
            pl.BlockSpec((None, NSA_KV_HEADS, ts, HEAD_DIM), lambda bi, i: (bi, 0, i, 0)),
            vt_spec,
            pl.BlockSpec((None, LANES, ts), lambda bi, i: (bi, 0, i))],
        out_shape=[jax.ShapeDtypeStruct((b, NSA_WIDTH, s), BF16),
                   jax.ShapeDtypeStruct((b, NSA_KV_HEADS, s, LANES), BF16),
                   vt_shape,
                   jax.ShapeDtypeStruct((b, NSA_KV_HEADS, s, HEAD_DIM), BF16),
                   vt_shape,
                   jax.ShapeDtypeStruct((b, LANES, s), F32)],
        compiler_params=_cp(("parallel", "parallel")),
        name="nsa_prep",
    )(proj3, proj3, proj3, proj3, proj3, proj3, c, shi, slo)


def _gelu_tanh(x):
    return 0.5 * x * (1.0 + jnp.tanh(0.7978845608028654 * (x + 0.044715 * x * x * x)))


def _cmp_kernel(x_ref, pe_ref, w1_ref, b1_ref, w2_ref, c_ref, shi_ref, slo_ref, o_ref, *, is_key):
    x = x_ref[...].astype(F32)
    n = x.shape[0]
    half = x.shape[1]
    xa = (x + pe_ref[0:1, :]).astype(BF16)
    xb = (x + pe_ref[1:2, :]).astype(BF16)
    ha = _dot(xa, w1_ref[0:half, :])
    hb = _dot(xb, w1_ref[half:2 * half, :])
    hid = ha + pltpu.roll(hb, n - 1, 0) + b1_ref[...]
    out = _dot(_gelu_tanh(hid).astype(BF16), w2_ref[...])
    if is_key:
        out = _rope128(out, c_ref[...], shi_ref[...], slo_ref[...])
        o_ref[...] = out[:, 0:HEAD_DIM].astype(BF16)
    else:
        o_ref[...] = out.T[0:HEAD_DIM, :].astype(BF16)


def _compress(xu, pe, w1, b1, w2p, c, shi, slo, *, is_key):
    b, nh, n, w = xu.shape
    fix2 = lambda bi, h: (0, 0)
    tab = pl.BlockSpec((None, n, LANES), lambda bi, h: (bi, 0, 0))
    oshape = (n, HEAD_DIM) if is_key else (HEAD_DIM, n)
    return pl.pallas_call(
        functools.partial(_cmp_kernel, is_key=is_key),
        grid=(b, nh),
        in_specs=[pl.BlockSpec((None, None, n, w), lambda bi, h: (bi, h, 0, 0)),
                  pl.BlockSpec((2, w), fix2), pl.BlockSpec((2 * w, CMP_HIDDEN), fix2),
                  pl.BlockSpec((1, CMP_HIDDEN), fix2), pl.BlockSpec((CMP_HIDDEN, LANES), fix2),
                  tab, tab, tab],
        out_specs=pl.BlockSpec((None, None) + oshape, lambda bi, h: (bi, h, 0, 0)),
        out_shape=jax.ShapeDtypeStruct((b, nh) + oshape, BF16),
        compiler_params=_cp(("parallel", "parallel")),
        name="nsa_compress",
    )(xu, pe, w1, b1, w2p, c, shi, slo)


def _nsa_kernel(qt_ref, gt_ref, kc_ref, vct_ref, mt_ref, ksa_ref, vst_ref, kwh_ref, vwt_ref, o_ref):
    tq = qt_ref.shape[1]
    cols = NSA_GROUP * tq
    n_cmp = kc_ref.shape[1]
    n_slc = mt_ref.shape[0]
    i = pl.program_id(1)
    t0 = i * tq
    loc = lax.broadcasted_iota(I32, (1, tq), 1)
    loc_c = jnp.concatenate([loc] * NSA_GROUP, axis=1)
    tok = t0 + loc
    tok_c = t0 + loc_c
    cmp_end = lax.broadcasted_iota(I32, (n_cmp, 1), 0) * CMP_STRIDE + (CMP_BLOCK - 1)
    cmp_mask = cmp_end <= tok_c
    blk = lax.broadcasted_iota(I32, (n_slc, 1), 0)
    blk_f = blk.astype(F32)
    cur = tok // SLC_BLOCK
    forced = (blk == 0) | (blk == cur) | (blk == cur - 1)
    allowed = blk <= cur
    tk = NSA_TK
    n_full = t0 // tk
    last_mask = (n_full * tk + lax.broadcasted_iota(I32, (tk, 1), 0)) <= tok_c
    n_win = WINDOW + tq
    ws = pl.multiple_of(jnp.maximum(t0 - WINDOW, 0), tq)
    wt0 = ws // tq
    dwin = tok_c - (ws + lax.broadcasted_iota(I32, (n_win, 1), 0))
    win_mask = (dwin >= 0) & (dwin < WINDOW)
    cmp_to_slc_t = mt_ref[...]

    kvh = range(NSA_KV_HEADS)
    hss = [slice(HEAD_DIM * h, HEAD_DIM * (h + 1)) for h in kvh]
    qhs = [jnp.concatenate(
        [qt_ref[HEAD_DIM * (NSA_GROUP * h + g):HEAD_DIM * (NSA_GROUP * h + g + 1), :]
         for g in range(NSA_GROUP)], axis=1) for h in kvh]
    ss = [jnp.where(cmp_mask, _dot(kc_ref[h], qhs[h]), NEG_INF) for h in kvh]
    es = [jnp.where(cmp_mask, jnp.exp(s - jnp.max(s, axis=0, keepdims=True)), 0.0) for s in ss]
    ps = [e / jnp.maximum(jnp.sum(e, axis=0, keepdims=True), 1e-30) for e in es]
    o_cs = [_dot(vct_ref[h], ps[h].astype(BF16)) for h in kvh]
    p4s = [sum(p[:, g * tq:(g + 1) * tq] for g in range(1, NSA_GROUP)) + p[:, 0:tq] for p in ps]
    p_his = [p4.astype(BF16) for p4 in p4s]
    p_los = [(p4 - p_hi.astype(F32)).astype(BF16) for p4, p_hi in zip(p4s, p_his)]
    imps = [_dot(cmp_to_slc_t, p_hi) + _dot(cmp_to_slc_t, p_lo) for p_hi, p_lo in zip(p_his, p_los)]
    imps = [jnp.where(forced, SEL_FORCE, jnp.where(allowed, imp, -SEL_FORCE)) for imp in imps]
    sels = [jnp.zeros((n_slc, tq), F32) for _ in kvh]
    for _ in range(SLC_TOPN):
        bests = [jnp.max(imp, axis=0, keepdims=True) for imp in imps]
        idxs = [jnp.min(jnp.where(imp == best, blk_f, float(n_slc)), axis=0, keepdims=True)
                for imp, best in zip(imps, bests)]
        hits = [blk_f == idx for idx in idxs]
        sels = [jnp.where(hit, 1.0, sel) for hit, sel in zip(hits, sels)]
        imps = [jnp.where(hit, VERY_NEG, imp) for hit, imp in zip(hits, imps)]
    biases = [jnp.where(sel > 0.5, 0.0, -MASK_BIG).astype(BF16) for sel in sels]
    q_augs = [jnp.concatenate([qh, jnp.concatenate([bias] * NSA_GROUP, axis=1)], axis=0)
              for qh, bias in zip(qhs, biases)]

    def sel_step(j, carry, masked):
        k0 = pl.multiple_of(j * tk, tk)
        out = []
        for h in kvh:
            m_i, l_i, acc = carry[h]
            sc = _dot(ksa_ref[h, pl.ds(k0, tk), :], q_augs[h])
            if masked:
                sc = jnp.where(last_mask, sc, NEG_INF)
            m_new = jnp.maximum(m_i, jnp.max(sc, axis=0, keepdims=True))
            alpha = jnp.exp(m_i - m_new)
            pp = jnp.exp(sc - m_new)
            l_new = alpha * l_i + jnp.sum(pp, axis=0, keepdims=True)
            vt = jnp.concatenate([vst_ref[j * (tk // tq) + d, hss[h], :] for d in range(tk // tq)], axis=1)
            out.append((m_new, l_new, alpha * acc + _dot(vt, pp.astype(BF16))))
        return tuple(out)

    carry = tuple((jnp.full((1, cols), VERY_NEG, F32), jnp.zeros((1, cols), F32),
                   jnp.zeros((HEAD_DIM, cols), F32)) for _ in kvh)
    carry = lax.fori_loop(0, n_full, lambda j, cr: sel_step(j, cr, False), carry)
    carry = sel_step(n_full, carry, True)
    o_ss = [acc / l for _, l, acc in carry]
    sws = [jnp.where(win_mask, _dot(kwh_ref[h, pl.ds(ws, n_win), :], qhs[h]), NEG_INF) for h in kvh]
    ews = [jnp.exp(sw - jnp.max(sw, axis=0, keepdims=True)) for sw in sws]
    vws = [jnp.concatenate([vwt_ref[wt0 + d, hss[h], :] for d in range(n_win // tq)], axis=1) for h in kvh]
    o_ws = [_dot(vw, ew.astype(BF16)) / jnp.sum(ew, axis=0, keepdims=True) for vw, ew in zip(vws, ews)]
    out_rows = []
    for h in kvh:
        for g in range(NSA_GROUP):
            head = NSA_GROUP * h + g
            cs = slice(g * tq, (g + 1) * tq)
            out_rows.append(gt_ref[head:head + 1, :] * o_cs[h][:, cs]
                            + gt_ref[NSA_HEADS + head:NSA_HEADS + head + 1, :] * o_ss[h][:, cs]
                            + gt_ref[2 * NSA_HEADS + head:2 * NSA_HEADS + head + 1, :] * o_ws[h][:, cs])
    o_ref[...] = jnp.concatenate(out_rows, axis=0).T.astype(BF16)


def _cmp_slc_weights(n_cmp_pad, n_cmp, n_slc_pad):
    c0 = np.arange(n_cmp_pad)[:, None] * CMP_STRIDE
    s0 = np.arange(n_slc_pad)[None, :] * SLC_BLOCK
    shared = np.minimum(c0 + CMP_BLOCK, s0 + SLC_BLOCK) - np.maximum(c0, s0)
    w = np.maximum(shared, 0) / CMP_STRIDE
    w[n_cmp:] = 0.0
    return w.astype(np.float32)


def _nsa_attention(q_t, g_t, kc, vc_t, ksa, vs_t, kwh, vw_t):
    b, _, s = q_t.shape
    tq = NSA_TQ
    n_cmp_pad = kc.shape[2]
    n_slc = HEAD_DIM
    assert SLC_TOPN <= s // SLC_BLOCK <= n_slc and s >= WINDOW + tq and tq == LANES
    assert s % NSA_TK == 0 and NSA_TK % tq == 0
    m_t = jnp.asarray(_cmp_slc_weights(n_cmp_pad, s // CMP_STRIDE - 1, n_slc).T, BF16)
    per_b4 = lambda bi, i: (bi, 0, 0, 0)
    return pl.pallas_call(
        _nsa_kernel,
        grid=(b, s // tq),
        in_specs=[
            pl.BlockSpec((None, NSA_WIDTH, tq), lambda bi, i: (bi, 0, i)),
            pl.BlockSpec((None, LANES, tq), lambda bi, i: (bi, 0, i)),
            pl.BlockSpec((None, NSA_KV_HEADS, n_cmp_pad, HEAD_DIM), per_b4),
            pl.BlockSpec((None, NSA_KV_HEADS, HEAD_DIM, n_cmp_pad), per_b4),
            pl.BlockSpec((n_slc, n_cmp_pad), lambda bi, i: (0, 0)),
            pl.BlockSpec((None, NSA_KV_HEADS, s, LANES), per_b4),
            pl.BlockSpec((None, s // LANES, NSA_KV_WIDTH, LANES), per_b4),
            pl.BlockSpec((None, NSA_KV_HEADS, s, HEAD_DIM), per_b4),
            pl.BlockSpec((None, s // LANES, NSA_KV_WIDTH, LANES), per_b4)],
        out_specs=pl.BlockSpec((None, tq, NSA_WIDTH), lambda bi, i: (bi, i, 0)),
        out_shape=jax.ShapeDtypeStruct((b, s, NSA_WIDTH), BF16),
        compiler_params=_cp(("parallel", "arbitrary")),
        name="nsa_attention",
    )(q_t, g_t, kc, vc_t, m_t, ksa, vs_t, kwh, vw_t)


def _softplus(x):
    return jnp.maximum(x, 0.0) + jnp.log(1.0 + jnp.exp(-jnp.abs(x)))


def _split3(x):
    h1 = x.astype(BF16)
    r1 = x - h1.astype(F32)
    h2 = r1.astype(BF16)
    h3 = (r1 - h2.astype(F32)).astype(BF16)
    return h1, h2, h3


def _seg_sum(x, bd):
    cols = []
    for j in range(x.shape[1] // LANES):
        xs = x[:, LANES * j:LANES * (j + 1)]
        h1, h2, h3 = _split3(xs)
        cols.append(_dot(h1, bd) + _dot(h2, bd) + _dot(h3, bd))
    return jnp.concatenate(cols, axis=1)


def _rwkv_kernel(*refs, has_vres):
    it = iter(refs)
    r_ref, k_ref, v_ref, sm_ref = next(it), next(it), next(it), next(it)
    vf_ref = next(it) if has_vres else None
    mu_ref, musm_ref, w0_ref, w2_ref, a0_ref, a2_ref, g2_ref = (next(it) for _ in range(7))
    kk_ref, ka_ref, rk_ref, lng_ref, lnb_ref = (next(it) for _ in range(5))
    if has_vres:
        v0_ref, v1_ref, v2_ref = next(it), next(it), next(it)
    tri_ref, bd_ref = next(it), next(it)
    y_ref, vo_ref = next(it), next(it)
    z_ref, prev_ref, prevsm_ref = next(it), next(it), next(it)

    c = r_ref.shape[0]
    width = r_ref.shape[1]

    @pl.when(pl.program_id(1) == 0)
    def _():
        z_ref[...] = jnp.zeros(z_ref.shape, F32)
        prev_ref[...] = jnp.zeros(prev_ref.shape, F32)
        prevsm_ref[...] = jnp.zeros(prevsm_ref.shape, F32)

    row = lax.broadcasted_iota(I32, (c, 1), 0)

    def shift_mix(x, pref, idx, mu):
        prev = pref[idx, 0:1, :]
        xs = jnp.where(row == 0, prev, pltpu.roll(x, 1, 0))
        pref[idx, 0:1, :] = x[c - 1:c, :]
        return x + (xs - x) * mu

    xr = shift_mix(r_ref[...].astype(F32), prev_ref, 0, mu_ref[0:1, :])
    xk = shift_mix(k_ref[...].astype(F32), prev_ref, 1, mu_ref[1:2, :])
    xv = shift_mix(v_ref[...].astype(F32), prev_ref, 2, mu_ref[2:3, :])
    sm = shift_mix(sm_ref[...].astype(F32), prevsm_ref, 0, musm_ref[...])

    wpre = w0_ref[...] + _dot(jnp.tanh(sm).astype(BF16), w2_ref[...])
    lw = -jnp.exp(-_softplus(-wpre) - 0.5)
    a = jax.nn.sigmoid(a0_ref[...] + _dot(sm.astype(BF16), a2_ref[...]))
    gate = _dot(jax.nn.sigmoid(sm).astype(BF16), g2_ref[...])
    if has_vres:
        lo = _dot(_dot(xv.astype(BF16), v1_ref[...]).astype(BF16), v2_ref[...])
        xv = xv + (vf_ref[...].astype(F32) - xv) * jax.nn.sigmoid(v0_ref[...] + lo)
    vo_ref[...] = xv.astype(BF16)

    bd = bd_ref[...]
    kk = xk * kk_ref[...]
    kk = kk / jnp.maximum(jnp.sqrt(_seg_sum(kk * kk, bd)), 1e-12)
    kmod = xk * (1.0 + (a - 1.0) * ka_ref[...])
    ah = -kk
    bh = kk * a

    tri = tri_ref[...]
    l1, l2, l3 = _split3(lw)
    cum = _dot(tri, l1) + _dot(tri, l2) + _dot(tri, l3)
    cumx = cum - lw
    cmid = cum[c // 2 - 1:c // 2, :]
    cend = cum[c - 1:c, :]
    e_inv = jnp.exp(cmid - cum)
    e_end = jnp.exp(cend - cum)
    a_t = (ah * jnp.exp(cumx - cmid)).astype(BF16)
    r_t = (xr * jnp.exp(cum - cmid)).astype(BF16)
    b_t = (bh * e_inv).astype(BF16)
    k_t = (kmod * e_inv).astype(BF16)
    a_0 = (ah * jnp.exp(cumx)).astype(BF16)
    r_0 = (xr * jnp.exp(cum)).astype(BF16)
    b_e = (bh * e_end).astype(BF16)
    k_e = (kmod * e_end).astype(BF16)
    w_end = jnp.exp(cend)
    bonus_in = xr * kmod * rk_ref[...]

    ri = lax.broadcasted_iota(I32, (c, c), 0)
    ci = lax.broadcasted_iota(I32, (c, c), 1)
    strict = ci < ri
    incl = ci <= ri
    n_levels = int(np.log2(c))
    assert 2 ** n_levels == c

    heads = range(width // RW_HEAD)
    sls = [slice(RW_HEAD * h, RW_HEAD * (h + 1)) for h in heads]
    zts = [z_ref[h] for h in heads]
    ztbs = [zt.astype(BF16) for zt in zts]
    vhs = [xv[:, sl] for sl in sls]
    vhbs = [vh.astype(BF16) for vh in vhs]
    m1s = [_dot_nt(jnp.concatenate([a_t[:, sl], r_t[:, sl]], axis=0),
                   jnp.concatenate([b_t[:, sl], k_t[:, sl]], axis=0)) for sl in sls]
    npows = [jnp.where(strict, m1[0:c, 0:c], 0.0).astype(BF16) for m1 in m1s]
    a_aks = [jnp.where(strict, m1[0:c, c:2 * c], 0.0).astype(BF16) for m1 in m1s]
    pqs = [jnp.concatenate([jnp.where(incl, m1[c:2 * c, 0:c], 0.0),
                            jnp.where(incl, m1[c:2 * c, c:2 * c], 0.0)], axis=1).astype(BF16) for m1 in m1s]
    us = [_dot_nt(a_0[:, sl], ztb) + _dot(a_ak, vhb)
          for sl, ztb, a_ak, vhb in zip(sls, ztbs, a_aks, vhbs)]
    y0s = [_dot_nt(r_0[:, sl], ztb) for sl, ztb in zip(sls, ztbs)]
    for lvl in range(n_levels):
        us = [u + _dot(npow, u.astype(BF16)) for u, npow in zip(us, npows)]
        if lvl + 1 < n_levels:
            npows = [_dot(npow, npow).astype(BF16) for npow in npows]
    uvs = [jnp.concatenate([u, vh], axis=0) for u, vh in zip(us, vhs)]
    ys = [y0 + _dot(pq, uv.astype(BF16)) for y0, pq, uv in zip(y0s, pqs, uvs)]
    for h in heads:
        sl = sls[h]
        bke = jnp.concatenate([b_e[:, sl], k_e[:, sl]], axis=0)
        z_ref[h] = zts[h] * w_end[:, sl] + _dot(uvs[h].T.astype(BF16), bke)
    for h in heads:
        sl = sls[h]
        y = ys[h]
        ym = jnp.mean(y, axis=1, keepdims=True)
        yd = y - ym
        yv = jnp.mean(yd * yd, axis=1, keepdims=True)
        yn = yd * lax.rsqrt(yv + RW_LNX_EPS) * lng_ref[:, sl] + lnb_ref[:, sl]
        bonus = jnp.sum(bonus_in[:, sl], axis=1, keepdims=True) * vhs[h]
        y_ref[:, sl] = ((yn + bonus) * gate[:, sl]).astype(BF16)


def _rwkv(proj3, vfirst, p, *, has_vres):
    b, s, _ = proj3.shape
    c = RW_CHUNK
    w = RW_WIDTH
    cb = lambda off: off // w
    fix = lambda bi, i: (0, 0)
    blk = lambda off: pl.BlockSpec((None, c, w), lambda bi, i: (bi, i, cb(off)))
    row_w = pl.BlockSpec((1, w), fix)
    in_specs = [blk(OFF_R), blk(OFF_K), blk(OFF_V),
                pl.BlockSpec((None, c, SMALL_W), lambda bi, i: (bi, i, OFF_SMALL // SMALL_W))]
    args = [proj3, proj3, proj3, proj3]
    if has_vres:
        in_specs.append(pl.BlockSpec((None, c, w), lambda bi, i: (bi, i, 0)))
        args.append(vfirst)
    in_specs += [pl.BlockSpec((3, w), fix), pl.BlockSpec((1, SMALL_W), fix), row_w,
                 pl.BlockSpec((SMALL_W, w), fix), row_w, pl.BlockSpec((SMALL_W, w), fix),
                 pl.BlockSpec((SMALL_W, w), fix), row_w, row_w, row_w, row_w, row_w]
    args += [p["mu"], p["mu_sm"], p["w0"], p["w2"], p["a0"], p["a2"], p["g2"],
             p["k_k"], p["k_a"], p["r_k"], p["lnx_g"], p["lnx_b"]]
    if has_vres:
        in_specs += [row_w, pl.BlockSpec((w, LANES), fix), pl.BlockSpec((LANES, w), fix)]
        args += [p["v0"], p["v1"], p["v2"]]
    tri = jnp.asarray(np.tril(np.ones((c, c), np.float32)), BF16)
    lane_head = np.arange(LANES) // RW_HEAD
    bd = jnp.asarray((lane_head[:, None] == lane_head[None, :]).astype(np.float32), BF16)
    in_specs += [pl.BlockSpec((c, c), fix), pl.BlockSpec((LANES, LANES), fix)]
    args += [tri, bd]
    out_blk = pl.BlockSpec((None, c, w), lambda bi, i: (bi, i, 0))
    return pl.pallas_call(
        functools.partial(_rwkv_kernel, has_vres=has_vres),
        grid=(b, s // c),
        in_specs=in_specs,
        out_specs=[out_blk, out_blk],
        out_shape=[jax.ShapeDtypeStruct((b, s, w), BF16), jax.ShapeDtypeStruct((b, s, w), BF16)],
        scratch_shapes=[pltpu.VMEM((w // RW_HEAD, RW_HEAD, RW_HEAD), F32),
                        pltpu.VMEM((3, 8, w), F32), pltpu.VMEM((1, 8, SMALL_W), F32)],
        compiler_params=_cp(("parallel", "arbitrary")),
        name="rwkv7",
    )(*args)


POOL_HALO = 16


def _pool_kernel(cur_ref, halo_ref, w_ref, sc_ref, o_ref):
    i = pl.program_id(1)
    tp = cur_ref.shape[0]
    cur = cur_ref[...].astype(F32)
    halo = jnp.where(i > 0, halo_ref[...].astype(F32), 0.0)
    ext = jnp.concatenate([halo, cur], axis=0)
    sums = {1: ext}
    win = 1
    while win < max(POOL_WINDOWS):
        sums[2 * win] = sums[win] + pltpu.roll(sums[win], win, 0)
        win *= 2
    count = (i * tp + 1 + lax.broadcasted_iota(I32, (tp, 1), 0)).astype(F32)
    for gi, win in enumerate(POOL_WINDOWS):
        sl = slice(POOL_GROUP * gi, POOL_GROUP * (gi + 1))
        mean = sums[win][POOL_HALO:, sl] / jnp.minimum(count, float(win))
        y = _dot((mean - cur[:, sl]).astype(BF16), w_ref[gi])
        o_ref[:, sl] = (y * sc_ref[:, sl]).astype(BF16)


def _pool(proj3, pool_w, pool_scale, *, tp=256):
    b, s, _ = proj3.shape
    w = POOL_WIDTH
    cb = OFF_POOL // w
    per = tp // POOL_HALO
    return pl.pallas_call(
        _pool_kernel,
        grid=(b, s // tp),
        in_specs=[pl.BlockSpec((None, tp, w), lambda bi, i: (bi, i, cb)),
                  pl.BlockSpec((None, POOL_HALO, w), lambda bi, i: (bi, jnp.maximum(i * per - 1, 0), cb)),
                  pl.BlockSpec((len(POOL_WINDOWS), POOL_GROUP, POOL_GROUP), lambda bi, i: (0, 0, 0)),
                  pl.BlockSpec((1, w), lambda bi, i: (0, 0))],
        out_specs=pl.BlockSpec((None, tp, w), lambda bi, i: (bi, i, 0)),
        out_shape=jax.ShapeDtypeStruct((b, s, w), BF16),
        compiler_params=_cp(("parallel", "parallel")),
        name="pool",
    )(proj3, proj3, pool_w, pool_scale.reshape(1, w))


def _merge_kernel(ya_ref, yb_ref, yc_ref, ga_ref, gb_ref, gc_ref, wb_ref, o_ref):
    acc = jax.nn.sigmoid(ga_ref[...].astype(F32)) * _dot(ya_ref[...], wb_ref[0])
    acc = acc + jax.nn.sigmoid(gb_ref[...].astype(F32)) * _dot(yb_ref[...], wb_ref[1])
    acc = acc + jax.nn.sigmoid(gc_ref[...].astype(F32)) * _dot(yc_ref[...], wb_ref[2])
    o_ref[...] = acc.astype(BF16)


def _merge(ya, yb, yc, proj, wb, *, tm=256):
    t, w = ya.shape
    d = wb.shape[2]
    row = lambda i: (i, 0)
    gate = lambda k: pl.BlockSpec((tm, d), lambda i: (i, k))
    return pl.pallas_call(
        _merge_kernel,
        grid=(t // tm,),
        in_specs=[pl.BlockSpec((tm, w), row)] * 3 + [gate(0), gate(1), gate(2),
                                                      pl.BlockSpec((N_BRANCH, w, d), lambda i: (0, 0, 0))],
        out_specs=pl.BlockSpec((tm, d), row),
        out_shape=jax.ShapeDtypeStruct((t, d), BF16),
        compiler_params=_cp(("parallel",)),
        name="merge",
    )(ya, yb, yc, proj, proj, proj, wb)


def _xattn_kernel(h16_ref, h32_ref, wq_ref, mkv_ref, wo_ref, g_ref, b_ref, o32_ref, o16_ref, *, alpha):
    q = _dot(h16_ref[...], wq_ref[...])
    scale = XA_HEAD_DIM ** -0.5
    outs = []
    for hd in range(XA_HEADS):
        sl = slice(XA_HEAD_DIM * hd, XA_HEAD_DIM * (hd + 1))
        slv = slice(XA_WIDTH + XA_HEAD_DIM * hd, XA_WIDTH + XA_HEAD_DIM * (hd + 1))
        s = _dot_nt(q[:, sl].astype(BF16), mkv_ref[:, sl]) * scale
        e = jnp.exp(s - jnp.max(s, axis=1, keepdims=True))
        p = e / jnp.sum(e, axis=1, keepdims=True)
        outs.append(_dot(p.astype(BF16), mkv_ref[:, slv]))
    o = jnp.concatenate(outs, axis=1).astype(BF16)
    y = alpha * h32_ref[...] + _dot(o, wo_ref[...])
    out = _layer_norm_rows(y, g_ref[...], b_ref[...])
    o32_ref[...] = out
    o16_ref[...] = out.astype(BF16)


def _xattn(h16, h32, wq, memkv, wo, g, b, *, alpha, seq, tm=256):
    t, d = h16.shape
    per_b = seq // tm
    nm = memkv.shape[1]
    row = lambda i: (i, 0)
    fix = lambda i: (0, 0)
    return pl.pallas_call(
        functools.partial(_xattn_kernel, alpha=alpha),
        grid=(t // tm,),
        in_specs=[pl.BlockSpec((tm, d), row), pl.BlockSpec((tm, d), row),
                  pl.BlockSpec((d, XA_WIDTH), fix),
                  pl.BlockSpec((None, nm, 2 * XA_WIDTH), lambda i: (i // per_b, 0, 0)),
                  pl.BlockSpec((XA_WIDTH, d), fix), pl.BlockSpec((1, d), fix), pl.BlockSpec((1, d), fix)],
        out_specs=[pl.BlockSpec((tm, d), row), pl.BlockSpec((tm, d), row)],
        out_shape=[jax.ShapeDtypeStruct((t, d), F32), jax.ShapeDtypeStruct((t, d), BF16)],
        compiler_params=_cp(("parallel",)),
        name="xattn",
    )(h16, h32, wq, memkv, wo, g.reshape(1, d), b.reshape(1, d))


def _ffn_kernel(*refs, ln, alpha):
    if ln:
        _, x_ref, wg_ref, wu_ref, wd_ref, res_ref, g_ref, b_ref, o32_ref, o16_ref, acc_ref = refs
    else:
        _, x_ref, wg_ref, wu_ref, wd_ref, o_ref, acc_ref = refs
    f = pl.program_id(1)

    @pl.when(f == 0)
    def _():
        acc_ref[...] = jnp.zeros(acc_ref.shape, F32)

    x = x_ref[...].astype(BF16)
    gt = _dot(x, wg_ref[0])
    up = _dot(x, wu_ref[0])
    act = (gt * jax.nn.sigmoid(gt) * up).astype(BF16)
    acc_ref[...] += _dot(act, wd_ref[0])

    @pl.when(f == pl.num_programs(1) - 1)
    def _():
        if ln:
            out = _layer_norm_rows(alpha * res_ref[...] + acc_ref[...], g_ref[...], b_ref[...])
            o32_ref[...] = out
            o16_ref[...] = out.astype(BF16)
        else:
            o_ref[...] = acc_ref[...]


def _ffn(block_expert, x, wg, wu, wd, *, tm, tf, res=None, g=None, b=None, alpha=1.0):
    n, d = x.shape
    ff = wg.shape[2]
    ln = res is not None
    row = lambda i, f, be: (i, 0)
    fix = lambda i, f, be: (0, 0)
    in_specs = [pl.BlockSpec((tm, d), row),
                pl.BlockSpec((1, d, tf), lambda i, f, be: (be[i], 0, f)),
                pl.BlockSpec((1, d, tf), lambda i, f, be: (be[i], 0, f)),
                pl.BlockSpec((1, tf, d), lambda i, f, be: (be[i], f, 0))]
    args = [x, wg, wu, wd]
    if ln:
        in_specs += [pl.BlockSpec((tm, d), row), pl.BlockSpec((1, d), fix), pl.BlockSpec((1, d), fix)]
        args += [res, g.reshape(1, d), b.reshape(1, d)]
        out_specs = [pl.BlockSpec((tm, d), row), pl.BlockSpec((tm, d), row)]
        out_shape = [jax.ShapeDtypeStruct((n, d), F32), jax.ShapeDtypeStruct((n, d), BF16)]
    else:
        out_specs = pl.BlockSpec((tm, d), row)
        out_shape = jax.ShapeDtypeStruct((n, d), F32)
    return pl.pallas_call(
        functools.partial(_ffn_kernel, ln=ln, alpha=alpha),
        grid_spec=pltpu.PrefetchScalarGridSpec(
            num_scalar_prefetch=1, grid=(n // tm, ff // tf),
            in_specs=in_specs, out_specs=out_specs,
            scratch_shapes=[pltpu.VMEM((tm, d), F32)]),
        out_shape=out_shape,
        compiler_params=_cp(("parallel", "arbitrary")),
        name="swiglu",
    )(block_expert, *args)


def _router_kernel(x_ref, w_ref, o_ref):
    logits = _dot(x_ref[...], w_ref[...])
    lane = lax.broadcasted_iota(I32, logits.shape, 1)
    lane_f = lane.astype(F32)
    l1 = jnp.where(lane < N_EXPERTS, logits, VERY_NEG)
    m1 = jnp.max(l1, axis=1, keepdims=True)
    i1 = jnp.min(jnp.where(l1 == m1, lane_f, float(LANES)), axis=1, keepdims=True)
    l2 = jnp.where(lane_f == i1, VERY_NEG, l1)
    m2 = jnp.max(l2, axis=1, keepdims=True)
    i2 = jnp.min(jnp.where(l2 == m2, lane_f, float(LANES)), axis=1, keepdims=True)
    e = jnp.exp(m2 - m1)
    g1 = 1.0 / (1.0 + e)
    g2 = e / (1.0 + e)
    o_ref[...] = jnp.where(lane == 0, i1, jnp.where(lane == 1, i2, jnp.where(lane == 2, g1,
                           jnp.where(lane == 3, g2, 0.0))))


def _router(h16, wr, *, tm=512):
    t, d = h16.shape
    return pl.pallas_call(
        _router_kernel,
        grid=(t // tm,),
        in_specs=[pl.BlockSpec((tm, d), lambda i: (i, 0)), pl.BlockSpec((d, LANES), lambda i: (0, 0))],
        out_specs=pl.BlockSpec((tm, LANES), lambda i: (i, 0)),
        out_shape=jax.ShapeDtypeStruct((t, LANES), F32),
        compiler_params=_cp(("parallel",)),
        name="router",
    )(h16, wr)


GATHER_ROWS = 256


def _gather_kernel(idx_ref, src_ref, o_ref, sem):
    n = o_ref.shape[0]

    def row_copy(r, src_row):
        return pltpu.make_async_copy(src_ref.at[pl.ds(src_row, 1)], o_ref.at[pl.ds(r, 1)], sem)

    def start(r, carry):
        row_copy(r, idx_ref[0, r]).start()
        return carry

    def wait(r, carry):
        row_copy(r, 0).wait()
        return carry

    lax.fori_loop(0, n, start, 0)
    lax.fori_loop(0, n, wait, 0)


def _gather_rows(src, idx):
    n = idx.shape[0]
    d = src.shape[1]
    rows = GATHER_ROWS
    return pl.pallas_call(
        _gather_kernel,
        grid=(n // rows,),
        in_specs=[pl.BlockSpec((None, 1, rows), lambda i: (i, 0, 0), memory_space=pltpu.SMEM),
                  pl.BlockSpec(memory_space=pl.ANY)],
        out_specs=pl.BlockSpec((rows, d), lambda i: (i, 0)),
        out_shape=jax.ShapeDtypeStruct((n, d), src.dtype),
        scratch_shapes=[pltpu.SemaphoreType.DMA(())],
        compiler_params=_cp(("arbitrary",)),
        name="gather_rows",
    )(idx.reshape(n // rows, 1, rows), src)


def _combine_kernel(y_ref, rt_ref, res_ref, g_ref, b_ref, o_ref, *, alpha):
    d = res_ref.shape[1]
    f = y_ref[:, 0:d] * rt_ref[:, 2:3] + y_ref[:, d:2 * d] * rt_ref[:, 3:4]
    o_ref[...] = _layer_norm_rows(alpha * res_ref[...] + f, g_ref[...], b_ref[...])


def _combine_ln(y2, rt, res, g, b, *, alpha, tm=256):
    t, d = res.shape
    row = lambda i: (i, 0)
    fix = lambda i: (0, 0)
    return pl.pallas_call(
        functools.partial(_combine_kernel, alpha=alpha),
        grid=(t // tm,),
        in_specs=[pl.BlockSpec((tm, 2 * d), row), pl.BlockSpec((tm, LANES), row), pl.BlockSpec((tm, d), row),
                  pl.BlockSpec((1, d), fix), pl.BlockSpec((1, d), fix)],
        out_specs=pl.BlockSpec((tm, d), row),
        out_shape=jax.ShapeDtypeStruct((t, d), F32),
        compiler_params=_cp(("parallel",)),
        name="moe_combine_ln",
    )(y2, rt, res, g.reshape(1, d), b.reshape(1, d))


def _moe(h16, h32, router, w_gate, w_up, w_down, g, b, *, alpha):
    t, d = h32.shape
    tm = MOE_TM
    wr = jnp.pad(router, ((0, 0), (0, LANES - N_EXPERTS))).astype(BF16)
    rt = _router(h16, wr)
    expert = rt[:, 0:TOP_K].astype(I32).reshape(-1)
    n_assign = t * TOP_K
    onehot = (expert[:, None] == jnp.arange(N_EXPERTS, dtype=I32)[None, :]).astype(I32)
    csum = jnp.cumsum(onehot, axis=0)
    rank = jnp.sum(csum * onehot, axis=1) - 1
    counts = csum[-1]
    padded = (counts + tm - 1) // tm * tm
    pad_end = jnp.cumsum(padded)
    pad_start = pad_end - padded
    dest = (pad_start[expert] + rank).astype(I32)
    n_pad = n_assign + N_EXPERTS * tm
    n_blocks = n_pad // tm
    tok_pad = jnp.zeros((n_pad,), I32).at[dest].set(jnp.arange(n_assign, dtype=I32) // TOP_K)
    block_expert = jnp.minimum(
        jnp.searchsorted(pad_end, jnp.arange(n_blocks, dtype=I32) * tm, side="right"),
        N_EXPERTS - 1).astype(I32)
    x_sorted = _gather_rows(h32, tok_pad)
    y_sorted = _ffn(block_expert, x_sorted, w_gate, w_up, w_down, tm=tm, tf=512)
    y_tok = _gather_rows(y_sorted, dest)
    return _combine_ln(y_tok.reshape(t, TOP_K * d), rt, h32, g, b, alpha=alpha)


def _perm_w_in(w, d_model):
    sizes = ((NSA_WIDTH,) + (NSA_KV_WIDTH,) * 6 + (N_BRANCH * NSA_HEADS,)
             + (RW_WIDTH, RW_WIDTH, RW_WIDTH, RW_DECAY_LORA, RW_A_LORA, RW_GATE_LORA)
             + (POOL_WIDTH,) + (d_model,) * N_BRANCH)
    offs = np.concatenate([[0], np.cumsum(sizes)])
    seg = [w[:, int(offs[i]):int(offs[i + 1])] for i in range(len(sizes))]
    (q, kc, vc, ks, vs, kw, vw, ng, r, k, v, wl, al, gl, pool, ga, gb, gc) = seg
    zeros = lambda n: jnp.zeros((w.shape[0], n), w.dtype)
    cols = [ga, gb, gc, q, r, k, v, pool, kc, vc, ks, vs, kw, vw,
            ng, zeros(SM_WL - N_BRANCH * NSA_HEADS), wl, al, gl, zeros(SMALL_W - SM_GL - RW_GATE_LORA)]
    out = jnp.concatenate(cols, axis=1).astype(BF16)
    assert out.shape[1] == NP_COLS and d_model == 2048
    return out


def _rwkv_params(l, mu, w0, w2, a0, a2, g2, k_k, k_a, r_k, lnx_g, lnx_b, v0, v1, v2):
    w = RW_WIDTH
    rowv = lambda a: a.reshape(1, w).astype(F32)
    m = mu[l]
    o = np.cumsum((0, w, w, w, RW_DECAY_LORA, RW_A_LORA, RW_GATE_LORA))
    mu_sm = jnp.zeros((1, SMALL_W), F32)
    mu_sm = mu_sm.at[0, SM_WL:SM_WL + RW_DECAY_LORA].set(m[o[3]:o[4]])
    mu_sm = mu_sm.at[0, SM_AL:SM_AL + RW_A_LORA].set(m[o[4]:o[5]])
    mu_sm = mu_sm.at[0, SM_GL:SM_GL + RW_GATE_LORA].set(m[o[5]:o[6]])

    def pad_rows(a, off):
        return jnp.zeros((SMALL_W, w), F32).at[off:off + a.shape[0]].set(a).astype(BF16)

    p = dict(mu=m[0:3 * w].reshape(3, w), mu_sm=mu_sm, w0=rowv(w0[l]), w2=pad_rows(w2[l], SM_WL),
             a0=rowv(a0[l]), a2=pad_rows(a2[l], SM_AL), g2=pad_rows(g2[l], SM_GL),
             k_k=rowv(k_k[l]), k_a=rowv(k_a[l]), r_k=rowv(r_k[l]), lnx_g=rowv(lnx_g[l]), lnx_b=rowv(lnx_b[l]))
    if l > 0:
        p["v0"] = rowv(v0[l - 1])
        p["v1"] = jnp.pad(v1[l - 1], ((0, 0), (0, LANES - RW_V_LORA))).astype(BF16)
        p["v2"] = jnp.pad(v2[l - 1], ((0, LANES - RW_V_LORA), (0, 0))).astype(BF16)
    return p


def _stride_units(x2d, b, s):
    x = x2d.reshape(b, s // CMP_STRIDE, CMP_STRIDE, NSA_KV_HEADS, HEAD_DIM)
    return x.transpose(0, 3, 1, 2, 4).reshape(b, NSA_KV_HEADS, s // CMP_STRIDE, CMP_STRIDE * HEAD_DIM)


def _token_mixer(l, h16, proj, b, s, tabs, tabs_c, prm, vfirst):
    d_model = h16.shape[1]
    t = b * s
    proj3 = proj.reshape(b, s, NP_COLS)
    c, shi, slo = tabs
    q_t, ksa, vs_t, kwh, vw_t, g_t = _nsa_prep(proj3, c, shi, slo)
    n_units = s // CMP_STRIDE
    half = CMP_STRIDE * HEAD_DIM
    kvc = proj[:, OFF_KVC:OFF_KVC + 2 * NSA_KV_WIDTH]
    cmp_out = []
    for j in range(2):
        xu = _stride_units(kvc[:, NSA_KV_WIDTH * j:NSA_KV_WIDTH * (j + 1)], b, s)
        pe = prm["nsa_cmp_pe"][l, j].reshape(2, half)
        w1 = prm["nsa_cmp_w1"][l, j].astype(BF16)
        b1 = prm["nsa_cmp_b1"][l, j].reshape(1, CMP_HIDDEN)
        w2p = jnp.pad(prm["nsa_cmp_w2"][l, j], ((0, 0), (0, LANES - HEAD_DIM))).astype(BF16)
        cmp_out.append(_compress(xu, pe, w1, b1, w2p, *tabs_c, is_key=(j == 0)))
    kc, vc_t = cmp_out
    y_a = _nsa_attention(q_t, g_t, kc, vc_t, ksa, vs_t, kwh, vw_t).reshape(t, NSA_WIDTH)
    rp = _rwkv_params(l, prm["rwkv_mu"], prm["rwkv_w0"], prm["rwkv_w2"], prm["rwkv_a0"], prm["rwkv_a2"],
                      prm["rwkv_g2"], prm["rwkv_k_k"], prm["rwkv_k_a"], prm["rwkv_r_k"], prm["rwkv_lnx_g"],
                      prm["rwkv_lnx_b"], prm["rwkv_v0"], prm["rwkv_v1"], prm["rwkv_v2"])
    y_b, v_l = _rwkv(proj3, vfirst, rp, has_vres=(l > 0))
    y_c = _pool(proj3, prm["pool_w"][l].astype(BF16), prm["pool_scale"][l])
    merged = _merge(y_a, y_b.reshape(t, RW_WIDTH), y_c.reshape(t, POOL_WIDTH), proj,
                    prm["w_branch"][l].astype(BF16))
    return merged, v_l


def kernel(x, mem, positions, w_in, nsa_cmp_pe, nsa_cmp_w1, nsa_cmp_b1, nsa_cmp_w2, rwkv_mu, rwkv_w0, rwkv_w2, rwkv_a0, rwkv_a2, rwkv_g2, rwkv_k_k, rwkv_k_a, rwkv_r_k, rwkv_lnx_g, rwkv_lnx_b, rwkv_v0, rwkv_v1, rwkv_v2, pool_w, pool_scale, w_branch, w_out, mem_ln_g, mem_ln_b, mem_wkv, xa_wq, xa_wo, ln_g, ln_b, ffn_w_gate, ffn_w_up, ffn_w_down, moe_router, moe_w_gate, moe_w_up, moe_w_down):
    prm = dict(nsa_cmp_pe=nsa_cmp_pe, nsa_cmp_w1=nsa_cmp_w1, nsa_cmp_b1=nsa_cmp_b1, nsa_cmp_w2=nsa_cmp_w2,
               rwkv_mu=rwkv_mu, rwkv_w0=rwkv_w0, rwkv_w2=rwkv_w2, rwkv_a0=rwkv_a0, rwkv_a2=rwkv_a2,
               rwkv_g2=rwkv_g2, rwkv_k_k=rwkv_k_k, rwkv_k_a=rwkv_k_a, rwkv_r_k=rwkv_r_k,
               rwkv_lnx_g=rwkv_lnx_g, rwkv_lnx_b=rwkv_lnx_b, rwkv_v0=rwkv_v0, rwkv_v1=rwkv_v1,
               rwkv_v2=rwkv_v2, pool_w=pool_w, pool_scale=pool_scale, w_branch=w_branch)
    b, s, d = x.shape
    t = b * s
    depth = w_in.shape[0]
    alpha = (2 * depth) ** 0.25
    n_mem = mem.shape[1]

    inv_freq = ROPE_THETA ** (-jnp.arange(ROPE_HALF, dtype=F32) / ROPE_HALF)
    dd = np.arange(LANES) % HEAD_DIM
    invf_lane = jnp.where(jnp.asarray(dd < ROPE_DIM), inv_freq[jnp.asarray(dd % ROPE_HALF)], 0.0).reshape(1, LANES)
    tabs = tuple(a.reshape(b, s, LANES) for a in _rope_tables(positions.reshape(t, 1), invf_lane, ts=512))
    n_units = s // CMP_STRIDE
    pos_c = positions[:, CMP_BLOCK - 1::CMP_STRIDE]
    pos_c = jnp.concatenate([pos_c, pos_c[:, -1:]], axis=1)
    tabs_c = tuple(a.reshape(b, n_units, LANES)
                   for a in _rope_tables(pos_c.reshape(b * n_units, 1), invf_lane, ts=n_units))

    memkv = _memkv(mem.reshape(b * n_mem, d), mem_ln_g, mem_ln_b, mem_wkv.astype(BF16))
    memkv = memkv.reshape(b, n_mem, 2 * XA_WIDTH)

    h32 = x.reshape(t, d)
    h16 = h32.astype(BF16)
    vfirst = None
    for l in range(depth):
        proj = _matmul(h16, _perm_w_in(w_in[l], d), tm=512, tn=1024, out_dtype=BF16)
        merged, v_l = _token_mixer(l, h16, proj, b, s, tabs, tabs_c, prm, vfirst)
        if l == 0:
            vfirst = v_l
        h32, h16 = _matmul_res_ln(merged, w_out[l].astype(BF16), h32, ln_g[l, 0], ln_b[l, 0], alpha=alpha)
        h32, h16 = _xattn(h16, h32, xa_wq[l].astype(BF16), memkv, xa_wo[l].astype(BF16),
                          ln_g[l, 1], ln_b[l, 1], alpha=alpha, seq=s)
        if l % 2 == 0:
            e = l // 2
            be = jnp.zeros((t // 512,), I32)
            h32, h16 = _ffn(be, h16, ffn_w_gate[e:e + 1].astype(BF16), ffn_w_up[e:e + 1].astype(BF16),
                            ffn_w_down[e:e + 1].astype(BF16), tm=512, tf=512,
                            res=h32, g=ln_g[l, 2], b=ln_b[l, 2], alpha=alpha)
        else:
            e = l // 2
            h32 = _moe(h16, h32, moe_router[e], moe_w_gate[e].astype(BF16), moe_w_up[e].astype(BF16),
                       moe_w_down[e].astype(BF16), ln_g[l, 2], ln_b[l, 2], alpha=alpha)
            h16 = h32.astype(BF16)
    return h32.reshape(b, s, d)
```

```python
import functools

import numpy as np
import jax
import jax.numpy as jnp
from jax import lax
from jax.experimental import pallas as pl
from jax.experimental.pallas import tpu as pltpu

F32 = jnp.float32
BF16 = jnp.bfloat16
I32 = jnp.int32

HEAD_DIM = 64
ROPE_DIM = HEAD_DIM // 4
ROPE_HALF = ROPE_DIM // 2
ROPE_THETA = 500000.0
NSA_HEADS = 16
NSA_KV_HEADS = 4
NSA_GROUP = NSA_HEADS // NSA_KV_HEADS
NSA_WIDTH = NSA_HEADS * HEAD_DIM
NSA_KV_WIDTH = NSA_KV_HEADS * HEAD_DIM
CMP_BLOCK = 32
CMP_STRIDE = 16
CMP_HIDDEN = 256
SLC_BLOCK = 64
SLC_TOPN = 8
WINDOW = 512
SEL_FORCE = 1e9
RW_HEADS = 16
RW_HEAD = 64
RW_WIDTH = RW_HEADS * RW_HEAD
RW_DECAY_LORA = 64
RW_A_LORA = 64
RW_V_LORA = 32
RW_GATE_LORA = 160
RW_LNX_EPS = 64e-5
POOL_WINDOWS = (2, 4, 8, 16)
POOL_GROUP = 256
POOL_WIDTH = POOL_GROUP * len(POOL_WINDOWS)
XA_HEADS = 4
XA_HEAD_DIM = 128
XA_WIDTH = XA_HEADS * XA_HEAD_DIM
N_EXPERTS = 8
TOP_K = 2
N_BRANCH = 3
LN_EPS = 1e-5
NEG_INF = -1e30
VERY_NEG = -3e38
MASK_BIG = 2.0 ** 100
LOG2_E = 1.4426950408889634

LANES = 128
VMEM_LIMIT = 56 * 1024 * 1024

NP_COLS = 13312
OFF_GATES = 0
OFF_Q = 6144
OFF_R = 7168
OFF_K = 8192
OFF_V = 9216
OFF_POOL = 10240
OFF_KVC = 11264
OFF_KVS = 11776
OFF_KVW = 12288
OFF_SMALL = 12800
SMALL_W = 512
SM_WL = 128
SM_AL = 192
SM_GL = 256

NSA_TQ = 128
NSA_TK = 256
ONES_ROWS = 16
RW_CHUNK = 64
MOE_TM = 512
MOE_NF = 8


def _cp(sem, vmem=VMEM_LIMIT):
    return pltpu.CompilerParams(dimension_semantics=sem, vmem_limit_bytes=vmem)


def _dot(a, b):
    return jnp.dot(a, b, preferred_element_type=F32)


def _dot_nt(a, b):
    return lax.dot_general(a, b, (((1,), (1,)), ((), ())), preferred_element_type=F32)


def _layer_norm_rows(y, g, b):
    mu = jnp.mean(y, axis=-1, keepdims=True)
    d = y - mu
    var = jnp.mean(d * d, axis=-1, keepdims=True)
    return d * lax.rsqrt(var + LN_EPS) * g + b


def _mm_kernel(x_ref, w_ref, o_ref):
    o_ref[...] = _dot(x_ref[...], w_ref[...]).astype(o_ref.dtype)


def _matmul(x, w, *, tm, tn, out_dtype):
    m, k = x.shape
    n = w.shape[1]
    return pl.pallas_call(
        _mm_kernel,
        grid=(n // tn, m // tm),
        in_specs=[pl.BlockSpec((tm, k), lambda j, i: (i, 0)),
                  pl.BlockSpec((k, tn), lambda j, i: (0, j))],
        out_specs=pl.BlockSpec((tm, tn), lambda j, i: (i, j)),
        out_shape=jax.ShapeDtypeStruct((m, n), out_dtype),
        compiler_params=_cp(("parallel", "parallel")),
        name="matmul",
    )(x, w)


def _mm_ln_kernel(x_ref, w_ref, res_ref, g_ref, b_ref, o32_ref, o16_ref, *, alpha):
    y = alpha * res_ref[...] + _dot(x_ref[...], w_ref[...])
    out = _layer_norm_rows(y, g_ref[...], b_ref[...])
    o32_ref[...] = out
    o16_ref[...] = out.astype(BF16)


def _matmul_res_ln(x, w, res, g, b, *, alpha, tm=256):
    m, k = x.shape
    n = w.shape[1]
    row = lambda i: (i, 0)
    fix = lambda i: (0, 0)
    return pl.pallas_call(
        functools.partial(_mm_ln_kernel, alpha=alpha),
        grid=(m // tm,),
        in_specs=[pl.BlockSpec((tm, k), row), pl.BlockSpec((k, n), fix), pl.BlockSpec((tm, n), row),
                  pl.BlockSpec((1, n), fix), pl.BlockSpec((1, n), fix)],
        out_specs=[pl.BlockSpec((tm, n), row), pl.BlockSpec((tm, n), row)],
        out_shape=[jax.ShapeDtypeStruct((m, n), F32), jax.ShapeDtypeStruct((m, n), BF16)],
        compiler_params=_cp(("parallel",)),
        name="matmul_res_ln",
    )(x, w, res, g.reshape(1, n), b.reshape(1, n))


def _memkv_kernel(m_ref, g_ref, b_ref, w_ref, o_ref):
    xn = _layer_norm_rows(m_ref[...], g_ref[...], b_ref[...])
    o_ref[...] = _dot(xn.astype(BF16), w_ref[...]).astype(o_ref.dtype)


def _memkv(mem2d, g, b, w, *, tm=256):
    m, d = mem2d.shape
    n = w.shape[1]
    row = lambda i: (i, 0)
    fix = lambda i: (0, 0)
    return pl.pallas_call(
        _memkv_kernel,
        grid=(m // tm,),
        in_specs=[pl.BlockSpec((tm, d), row), pl.BlockSpec((1, d), fix), pl.BlockSpec((1, d), fix),
                  pl.BlockSpec((d, n), fix)],
        out_specs=pl.BlockSpec((tm, n), row),
        out_shape=jax.ShapeDtypeStruct((m, n), BF16),
        compiler_params=_cp(("parallel",)),
        name="memkv",
    )(mem2d, g.reshape(1, d), b.reshape(1, d), w)


def _rope_table_kernel(pos_ref, invf_ref, c_ref, shi_ref, slo_ref):
    ang = pos_ref[...].astype(F32) * invf_ref[...]
    d = lax.broadcasted_iota(I32, ang.shape, 1) % HEAD_DIM
    cs = jnp.cos(ang)
    sn = jnp.sin(ang)
    c_ref[...] = jnp.where(d < ROPE_DIM, cs, 1.0)
    shi_ref[...] = jnp.where((d >= ROPE_HALF) & (d < ROPE_DIM), sn, 0.0)
    slo_ref[...] = jnp.where(d < ROPE_HALF, -sn, 0.0)


def _rope_tables(pos_col, invf_lane, *, ts):
    n = pos_col.shape[0]
    row = lambda i: (i, 0)
    out = jax.ShapeDtypeStruct((n, LANES), F32)
    return pl.pallas_call(
        _rope_table_kernel,
        grid=(n // ts,),
        in_specs=[pl.BlockSpec((ts, 1), row), pl.BlockSpec((1, LANES), lambda i: (0, 0))],
        out_specs=[pl.BlockSpec((ts, LANES), row)] * 3,
        out_shape=[out, out, out],
        compiler_params=_cp(("parallel",)),
        name="rope_tables",
    )(pos_col, invf_lane)


def _rope128(x, c, shi, slo):
    return x * c + pltpu.roll(x, ROPE_HALF, 1) * shi + pltpu.roll(x, LANES - ROPE_HALF, 1) * slo


def _nsa_prep_kernel(q_ref, ks_ref, vs_ref, kw_ref, vw_ref, sm_ref, c_ref, shi_ref, slo_ref,
                     qt_ref, ksa_ref, vst_ref, kwh_ref, vwt_ref, gt_ref):
    c = c_ref[...]
    shi = shi_ref[...]
    slo = slo_ref[...]
    ts = c.shape[0]
    scale = HEAD_DIM ** -0.5 * LOG2_E
    for j in range(NSA_WIDTH // LANES):
        sl = slice(LANES * j, LANES * (j + 1))
        qt_ref[sl, :] = (_rope128(q_ref[:, sl].astype(F32), c, shi, slo) * scale).T.astype(BF16)
    t = pl.program_id(1) * ts + lax.broadcasted_iota(I32, (ts, LANES), 0)
    lane = lax.broadcasted_iota(I32, (ts, LANES), 1)
    onehot = jnp.where(lane - HEAD_DIM == t // SLC_BLOCK, 1.0, 0.0)
    lo = lane < HEAD_DIM
    for j in range(NSA_KV_WIDTH // LANES):
        sl = slice(LANES * j, LANES * (j + 1))
        ks = _rope128(ks_ref[:, sl].astype(F32), c, shi, slo)
        kw = _rope128(kw_ref[:, sl].astype(F32), c, shi, slo).astype(BF16)
        ksa_ref[2 * j] = jnp.where(lo, ks, onehot).astype(BF16)
        ksa_ref[2 * j + 1] = jnp.where(lo, pltpu.roll(ks, HEAD_DIM, 1), onehot).astype(BF16)
        kwh_ref[2 * j] = kw[:, 0:HEAD_DIM]
        kwh_ref[2 * j + 1] = kw[:, HEAD_DIM:LANES]
    for k in range(ts // LANES):
        rs = slice(LANES * k, LANES * (k + 1))
        vst_ref[k] = vs_ref[rs, :].astype(F32).T.astype(BF16)
        vwt_ref[k] = vw_ref[rs, :].astype(F32).T.astype(BF16)
    gt_ref[...] = jax.nn.sigmoid(sm_ref[:, 0:LANES].astype(F32)).T


def _nsa_prep(proj3, c, shi, slo, *, ts=512):
    b, s, _ = proj3.shape
    kvw = NSA_KV_WIDTH
    tab = pl.BlockSpec((None, ts, LANES), lambda bi, i: (bi, i, 0))
    kv = lambda off: pl.BlockSpec((None, ts, kvw), lambda bi, i: (bi, i, off // kvw))
    vt_spec = pl.BlockSpec((None, ts // LANES, kvw, LANES), lambda bi, i: (bi, i, 0, 0))
    vt_shape = jax.ShapeDtypeStruct((b, s // LANES, kvw, LANES), BF16)
    return pl.pallas_call(
        _nsa_prep_kernel,
        grid=(b, s // ts),
        in_specs=[
            pl.BlockSpec((None, ts, NSA_WIDTH), lambda bi, i: (bi, i, OFF_Q // NSA_WIDTH)),
            kv(OFF_KVS), kv(OFF_KVS + kvw), kv(OFF_KVW), kv(OFF_KVW + kvw),
            pl.BlockSpec((None, ts, SMALL_W), lambda bi, i: (bi, i, OFF_SMALL // SMALL_W)),
            tab, tab, tab],
        out_specs=[
            pl.BlockSpec((None, NSA_WIDTH, ts), lambda bi, i: (bi, 0, i)),
            pl.BlockSpec((None, NSA_KV_HEADS, ts, LANES), lambda bi, i: (bi, 0, i, 0)),
            vt_spec,
            pl.BlockSpec((None, NSA_KV_HEADS, ts, HEAD_DIM), lambda bi, i: (bi, 0, i, 0)),
            vt_spec,
            pl.BlockSpec((None, LANES, ts), lambda bi, i: (bi, 0, i))],
        out_shape=[jax.ShapeDtypeStruct((b, NSA_WIDTH, s), BF16),
                   jax.ShapeDtypeStruct((b, NSA_KV_HEADS, s, LANES), BF16),
                   vt_shape,
                   jax.ShapeDtypeStruct((b, NSA_KV_HEADS, s, HEAD_DIM), BF16),
                   vt_shape,
                   jax.ShapeDtypeStruct((b, LANES, s), F32)],
        compiler_params=_cp(("parallel", "parallel")),
        name="nsa_prep",
    )(proj3, proj3, proj3, proj3, proj3, proj3, c, shi, slo)


def _gelu_tanh(x):
    return 0.5 * x * (1.0 + jnp.tanh(0.7978845608028654 * (x + 0.044715 * x * x * x)))


def _cmp_kernel(x_ref, pe_ref, w1_ref, b1_ref, w2_ref, c_ref, shi_ref, slo_ref, o_ref, *, is_key):
    x = x_ref[...].astype(F32)
    n = x.shape[0]
    half = x.shape[1]
    xa = (x + pe_ref[0:1, :]).astype(BF16)
    xb = (x + pe_ref[1:2, :]).astype(BF16)
    ha = _dot(xa, w1_ref[0:half, :])
    hb = _dot(xb, w1_ref[half:2 * half, :])
    hid = ha + pltpu.roll(hb, n - 1, 0) + b1_ref[...]
    out = _dot(_gelu_tanh(hid).astype(BF16), w2_ref[...])
    if is_key:
        out = _rope128(out, c_ref[...], shi_ref[...], slo_ref[...])
        o_ref[...] = out[:, 0:HEAD_DIM].astype(BF16)
    else:
        o_ref[...] = out.T[0:HEAD_DIM, :].astype(BF16)


def _compress(xu, pe, w1, b1, w2p, c, shi, slo, *, is_key):
    b, nh, n, w = xu.shape
    fix2 = lambda bi, h: (0, 0)
    tab = pl.BlockSpec((None, n, LANES), lambda bi, h: (bi, 0, 0))
    oshape = (n, HEAD_DIM) if is_key else (HEAD_DIM, n)
    return pl.pallas_call(
        functools.partial(_cmp_kernel, is_key=is_key),
        grid=(b, nh),
        in_specs=[pl.BlockSpec((None, None, n, w), lambda bi, h: (bi, h, 0, 0)),
                  pl.BlockSpec((2, w), fix2), pl.BlockSpec((2 * w, CMP_HIDDEN), fix2),
                  pl.BlockSpec((1, CMP_HIDDEN), fix2), pl.BlockSpec((CMP_HIDDEN, LANES), fix2),
                  tab, tab, tab],
        out_specs=pl.BlockSpec((None, None) + oshape, lambda bi, h: (bi, h, 0, 0)),
        out_shape=jax.ShapeDtypeStruct((b, nh) + oshape, BF16),
        compiler_params=_cp(("parallel", "parallel")),
        name="nsa_compress",
    )(xu, pe, w1, b1, w2p, c, shi, slo)


def _nsa_kernel(qt_ref, gt_ref, kc_ref, vct_ref, mt_ref, ksa_ref, vst_ref, kwh_ref, vwt_ref, o_ref):
    tq = qt_ref.shape[1]
    cols = NSA_GROUP * tq
    n_cmp = kc_ref.shape[1]
    n_slc = mt_ref.shape[0]
    i = pl.program_id(1)
    t0 = i * tq
    loc = lax.broadcasted_iota(I32, (1, tq), 1)
    loc_c = jnp.concatenate([loc] * NSA_GROUP, axis=1)
    tok = t0 + loc
    tok_c = t0 + loc_c
    cmp_end = lax.broadcasted_iota(I32, (n_cmp, 1), 0) * CMP_STRIDE + (CMP_BLOCK - 1)
    cmp_mask = cmp_end <= tok_c
    blk = lax.broadcasted_iota(I32, (n_slc, 1), 0)
    blk_f = blk.astype(F32)
    cur = tok // SLC_BLOCK
    forced = (blk == 0) | (blk == cur) | (blk == cur - 1)
    allowed = blk <= cur
    tk = NSA_TK
    n_full = t0 // tk
    last_mask = (n_full * tk + lax.broadcasted_iota(I32, (tk, 1), 0)) <= tok_c
    n_win = WINDOW + tq
    ws = pl.multiple_of(jnp.maximum(t0 - WINDOW, 0), tq)
    wt0 = ws // tq
    dwin = tok_c - (ws + lax.broadcasted_iota(I32, (n_win, 1), 0))
    win_mask = (dwin >= 0) & (dwin < WINDOW)
    cmp_to_slc_t = mt_ref[...]
    ones_rows = jnp.ones((ONES_ROWS, n_win), BF16)

    kvh = range(NSA_KV_HEADS)
    hss = [slice(HEAD_DIM * h, HEAD_DIM * (h + 1)) for h in kvh]
    qhs = [jnp.concatenate(
        [qt_ref[HEAD_DIM * (NSA_GROUP * h + g):HEAD_DIM * (NSA_GROUP * h + g + 1), :]
         for g in range(NSA_GROUP)], axis=1) for h in kvh]
    ss = [jnp.where(cmp_mask, _dot(kc_ref[h], qhs[h]), NEG_INF) for h in kvh]
    es = [jnp.where(cmp_mask, jnp.exp2(s - jnp.max(s, axis=0, keepdims=True)), 0.0) for s in ss]
    ps = [e / jnp.maximum(jnp.sum(e, axis=0, keepdims=True), 1e-30) for e in es]
    o_cs = [_dot(vct_ref[h], ps[h].astype(BF16)) for h in kvh]
    p4s = [sum(p[:, g * tq:(g + 1) * tq] for g in range(1, NSA_GROUP)) + p[:, 0:tq] for p in ps]
    p_his = [p4.astype(BF16) for p4 in p4s]
    p_los = [(p4 - p_hi.astype(F32)).astype(BF16) for p4, p_hi in zip(p4s, p_his)]
    imps = [_dot(cmp_to_slc_t, p_hi) + _dot(cmp_to_slc_t, p_lo) for p_hi, p_lo in zip(p_his, p_los)]
    imps = [jnp.where(forced, SEL_FORCE, jnp.where(allowed, imp, -SEL_FORCE)) for imp in imps]
    sels = [jnp.zeros((n_slc, tq), F32) for _ in kvh]
    for _ in range(SLC_TOPN):
        bests = [jnp.max(imp, axis=0, keepdims=True) for imp in imps]
        idxs = [jnp.min(jnp.where(imp == best, blk_f, float(n_slc)), axis=0, keepdims=True)
                for imp, best in zip(imps, bests)]
        hits = [blk_f == idx for idx in idxs]
        sels = [jnp.where(hit, 1.0, sel) for hit, sel in zip(hits, sels)]
        imps = [jnp.where(hit, VERY_NEG, imp) for hit, imp in zip(hits, imps)]
    biases = [jnp.where(sel > 0.5, 0.0, -MASK_BIG).astype(BF16) for sel in sels]
    q_augs = [jnp.concatenate([qh, jnp.concatenate([bias] * NSA_GROUP, axis=1)], axis=0)
              for qh, bias in zip(qhs, biases)]

    def sel_step(j, carry, masked):
        k0 = pl.multiple_of(j * tk, tk)
        scs = [_dot(ksa_ref[h, pl.ds(k0, tk), :], q_augs[h]) for h in kvh]
        if masked:
            scs = [jnp.where(last_mask, sc, NEG_INF) for sc in scs]
        m_news = [jnp.maximum(carry[h][0], jnp.max(scs[h], axis=0, keepdims=True)) for h in kvh]
        alphas = [jnp.exp2(carry[h][0] - m_news[h]) for h in kvh]
        pps = [jnp.exp2(scs[h] - m_news[h]).astype(BF16) for h in kvh]
        vts = [jnp.concatenate([vst_ref[j * (tk // tq) + d, hss[h], :] for d in range(tk // tq)], axis=1)
               for h in kvh]
        vts = [jnp.concatenate([vt, ones_rows[:, 0:tk]], axis=0) for vt in vts]
        accs = [alphas[h] * carry[h][1] + _dot(vts[h], pps[h]) for h in kvh]
        return tuple((m_news[h], accs[h]) for h in kvh)

    carry = tuple((jnp.full((1, cols), VERY_NEG, F32), jnp.zeros((HEAD_DIM + ONES_ROWS, cols), F32))
                  for _ in kvh)
    carry = lax.fori_loop(0, n_full, lambda j, cr: sel_step(j, cr, False), carry)
    carry = sel_step(n_full, carry, True)
    o_ss = [acc[0:HEAD_DIM] / acc[HEAD_DIM:HEAD_DIM + 1] for _, acc in carry]
    sws = [jnp.where(win_mask, _dot(kwh_ref[h, pl.ds(ws, n_win), :], qhs[h]), NEG_INF) for h in kvh]
    ews = [jnp.exp2(sw - jnp.max(sw, axis=0, keepdims=True)).astype(BF16) for sw in sws]
    vws = [jnp.concatenate([vwt_ref[wt0 + d, hss[h], :] for d in range(n_win // tq)], axis=1) for h in kvh]
    vws = [jnp.concatenate([vw, ones_rows], axis=0) for vw in vws]
    a_ws = [_dot(vw, ew) for vw, ew in zip(vws, ews)]
    o_ws = [a[0:HEAD_DIM] / a[HEAD_DIM:HEAD_DIM + 1] for a in a_ws]
    out_rows = []
    for h in kvh:
        for g in range(NSA_GROUP):
            head = NSA_GROUP * h + g
            cs = slice(g * tq, (g + 1) * tq)
            out_rows.append(gt_ref[head:head + 1, :] * o_cs[h][:, cs]
                            + gt_ref[NSA_HEADS + head:NSA_HEADS + head + 1, :] * o_ss[h][:, cs]
                            + gt_ref[2 * NSA_HEADS + head:2 * NSA_HEADS + head + 1, :] * o_ws[h][:, cs])
    o_ref[...] = jnp.concatenate(out_rows, axis=0).T.astype(BF16)


def _cmp_slc_weights(n_cmp_pad, n_cmp, n_slc_pad):
    c0 = np.arange(n_cmp_pad)[:, None] * CMP_STRIDE
    s0 = np.arange(n_slc_pad)[None, :] * SLC_BLOCK
    shared = np.minimum(c0 + CMP_BLOCK, s0 + SLC_BLOCK) - np.maximum(c0, s0)
    w = np.maximum(shared, 0) / CMP_STRIDE
    w[n_cmp:] = 0.0
    return w.astype(np.float32)


def _nsa_attention(q_t, g_t, kc, vc_t, ksa, vs_t, kwh, vw_t):
    b, _, s = q_t.shape
    tq = NSA_TQ
    n_cmp_pad = kc.shape[2]
    n_slc = HEAD_DIM
    assert SLC_TOPN <= s // SLC_BLOCK <= n_slc and s >= WINDOW + tq and tq == LANES
    assert s % NSA_TK == 0 and NSA_TK % tq == 0
    m_t = jnp.asarray(_cmp_slc_weights(n_cmp_pad, s // CMP_STRIDE - 1, n_slc).T, BF16)
    per_b4 = lambda bi, i: (bi, 0, 0, 0)
    return pl.pallas_call(
        _nsa_kernel,
        grid=(b, s // tq),
        in_specs=[
            pl.BlockSpec((None, NSA_WIDTH, tq), lambda bi, i: (bi, 0, i)),
            pl.BlockSpec((None, LANES, tq), lambda bi, i: (bi, 0, i)),
            pl.BlockSpec((None, NSA_KV_HEADS, n_cmp_pad, HEAD_DIM), per_b4),
            pl.BlockSpec((None, NSA_KV_HEADS, HEAD_DIM, n_cmp_pad), per_b4),
            pl.BlockSpec((n_slc, n_cmp_pad), lambda bi, i: (0, 0)),
            pl.BlockSpec((None, NSA_KV_HEADS, s, LANES), per_b4),
            pl.BlockSpec((None, s // LANES, NSA_KV_WIDTH, LANES), per_b4),
            pl.BlockSpec((None, NSA_KV_HEADS, s, HEAD_DIM), per_b4),
            pl.BlockSpec((None, s // LANES, NSA_KV_WIDTH, LANES), per_b4)],
        out_specs=pl.BlockSpec((None, tq, NSA_WIDTH), lambda bi, i: (bi, i, 0)),
        out_shape=jax.ShapeDtypeStruct((b, s, NSA_WIDTH), BF16),
        compiler_params=_cp(("parallel", "arbitrary")),
        name="nsa_attention",
    )(q_t, g_t, kc, vc_t, m_t, ksa, vs_t, kwh, vw_t)


def _softplus(x):
    return jnp.maximum(x, 0.0) + jnp.log(1.0 + jnp.exp(-jnp.abs(x)))


def _split3(x):
    h1 = x.astype(BF16)
    r1 = x - h1.astype(F32)
    h2 = r1.astype(BF16)
    h3 = (r1 - h2.astype(F32)).astype(BF16)
    return h1, h2, h3


def _seg_sum(x, bd):
    cols = []
    for j in range(x.shape[1] // LANES):
        xs = x[:, LANES * j:LANES * (j + 1)]
        h1, h2, h3 = _split3(xs)
        cols.append(_dot(h1, bd) + _dot(h2, bd) + _dot(h3, bd))
    return jnp.concatenate(cols, axis=1)


def _rwkv_kernel(*refs, has_vres):
    it = iter(refs)
    r_ref, k_ref, v_ref, sm_ref = next(it), next(it), next(it), next(it)
    vf_ref = next(it) if has_vres else None
    mu_ref, musm_ref, w0_ref, w2_ref, a0_ref, a2_ref, g2_ref = (next(it) for _ in range(7))
    kk_ref, ka_ref, rk_ref, lng_ref, lnb_ref = (next(it) for _ in range(5))
    if has_vres:
        v0_ref, v1_ref, v2_ref = next(it), next(it), next(it)
    tri_ref, bd_ref = next(it), next(it)
    y_ref, vo_ref = next(it), next(it)
    z_ref, prev_ref, prevsm_ref = next(it), next(it), next(it)

    c = r_ref.shape[0]
    width = r_ref.shape[1]

    @pl.when(pl.program_id(1) == 0)
    def _():
        z_ref[...] = jnp.zeros(z_ref.shape, F32)
        prev_ref[...] = jnp.zeros(prev_ref.shape, F32)
        prevsm_ref[...] = jnp.zeros(prevsm_ref.shape, F32)

    row = lax.broadcasted_iota(I32, (c, 1), 0)

    def shift_mix(x, pref, idx, mu):
        prev = pref[idx, 0:1, :]
        xs = jnp.where(row == 0, prev, pltpu.roll(x, 1, 0))
        pref[idx, 0:1, :] = x[c - 1:c, :]
        return x + (xs - x) * mu

    xr = shift_mix(r_ref[...].astype(F32), prev_ref, 0, mu_ref[0:1, :])
    xk = shift_mix(k_ref[...].astype(F32), prev_ref, 1, mu_ref[1:2, :])
    xv = shift_mix(v_ref[...].astype(F32), prev_ref, 2, mu_ref[2:3, :])
    sm = shift_mix(sm_ref[...].astype(F32), prevsm_ref, 0, musm_ref[...])

    wpre = w0_ref[...] + _dot(jnp.tanh(sm).astype(BF16), w2_ref[...])
    lw = -jnp.exp(-_softplus(-wpre) - 0.5)
    a = jax.nn.sigmoid(a0_ref[...] + _dot(sm.astype(BF16), a2_ref[...]))
    gate = _dot(jax.nn.sigmoid(sm).astype(BF16), g2_ref[...])
    if has_vres:
        lo = _dot(_dot(xv.astype(BF16), v1_ref[...]).astype(BF16), v2_ref[...])
        xv = xv + (vf_ref[...].astype(F32) - xv) * jax.nn.sigmoid(v0_ref[...] + lo)
    vo_ref[...] = xv.astype(BF16)

    bd = bd_ref[...]
    kk = xk * kk_ref[...]
    kk = kk / jnp.maximum(jnp.sqrt(_seg_sum(kk * kk, bd)), 1e-12)
    kmod = xk * (1.0 + (a - 1.0) * ka_ref[...])
    ah = -kk
    bh = kk * a

    tri = tri_ref[...]
    l1, l2, l3 = _split3(lw)
    cum = _dot(tri, l1) + _dot(tri, l2) + _dot(tri, l3)
    cumx = cum - lw
    cmid = cum[c // 2 - 1:c // 2, :]
    cend = cum[c - 1:c, :]
    e_inv = jnp.exp(cmid - cum)
    e_end = jnp.exp(cend - cum)
    a_t = (ah * jnp.exp(cumx - cmid)).astype(BF16)
    r_t = (xr * jnp.exp(cum - cmid)).astype(BF16)
    b_t = (bh * e_inv).astype(BF16)
    k_t = (kmod * e_inv).astype(BF16)
    a_0 = (ah * jnp.exp(cumx)).astype(BF16)
    r_0 = (xr * jnp.exp(cum)).astype(BF16)
    b_e = (bh * e_end).astype(BF16)
    k_e = (kmod * e_end).astype(BF16)
    w_end = jnp.exp(cend)
    bonus_in = xr * kmod * rk_ref[...]

    ri = lax.broadcasted_iota(I32, (c, c), 0)
    ci = lax.broadcasted_iota(I32, (c, c), 1)
    strict = ci < ri
    incl = ci <= ri
    n_levels = int(np.log2(c))
    assert 2 ** n_levels == c

    heads = range(width // RW_HEAD)
    sls = [slice(RW_HEAD * h, RW_HEAD * (h + 1)) for h in heads]
    zts = [z_ref[h] for h in heads]
    ztbs = [zt.astype(BF16) for zt in zts]
    vhs = [xv[:, sl] for sl in sls]
    vhbs = [vh.astype(BF16) for vh in vhs]
    m1s = [_dot_nt(jnp.concatenate([a_t[:, sl], r_t[:, sl]], axis=0),
                   jnp.concatenate([b_t[:, sl], k_t[:, sl]], axis=0)) for sl in sls]
    npows = [jnp.where(strict, m1[0:c, 0:c], 0.0).astype(BF16) for m1 in m1s]
    a_aks = [jnp.where(strict, m1[0:c, c:2 * c], 0.0).astype(BF16) for m1 in m1s]
    pqs = [jnp.concatenate([jnp.where(incl, m1[c:2 * c, 0:c], 0.0),
                            jnp.where(incl, m1[c:2 * c, c:2 * c], 0.0)], axis=1).astype(BF16) for m1 in m1s]
    us = [_dot_nt(a_0[:, sl], ztb) + _dot(a_ak, vhb)
          for sl, ztb, a_ak, vhb in zip(sls, ztbs, a_aks, vhbs)]
    y0s = [_dot_nt(r_0[:, sl], ztb) for sl, ztb in zip(sls, ztbs)]
    for lvl in range(n_levels):
        us = [u + _dot(npow, u.astype(BF16)) for u, npow in zip(us, npows)]
        if lvl + 1 < n_levels:
            npows = [_dot(npow, npow).astype(BF16) for npow in npows]
    uvs = [jnp.concatenate([u, vh], axis=0) for u, vh in zip(us, vhs)]
    ys = [y0 + _dot(pq, uv.astype(BF16)) for y0, pq, uv in zip(y0s, pqs, uvs)]
    for h in heads:
        sl = sls[h]
        bke = jnp.concatenate([b_e[:, sl], k_e[:, sl]], axis=0)
        z_ref[h] = zts[h] * w_end[:, sl] + _dot(uvs[h].T.astype(BF16), bke)
    for h in heads:
        sl = sls[h]
        y = ys[h]
        ym = jnp.mean(y, axis=1, keepdims=True)
        yd = y - ym
        yv = jnp.mean(yd * yd, axis=1, keepdims=True)
        yn = yd * lax.rsqrt(yv + RW_LNX_EPS) * lng_ref[:, sl] + lnb_ref[:, sl]
        bonus = jnp.sum(bonus_in[:, sl], axis=1, keepdims=True) * vhs[h]
        y_ref[:, sl] = ((yn + bonus) * gate[:, sl]).astype(BF16)


def _rwkv(proj3, vfirst, p, *, has_vres):
    b, s, _ = proj3.shape
    c = RW_CHUNK
    w = RW_WIDTH
    cb = lambda off: off // w
    fix = lambda bi, i: (0, 0)
    blk = lambda off: pl.BlockSpec((None, c, w), lambda bi, i: (bi, i, cb(off)))
    row_w = pl.BlockSpec((1, w), fix)
    in_specs = [blk(OFF_R), blk(OFF_K), blk(OFF_V),
                pl.BlockSpec((None, c, SMALL_W), lambda bi, i: (bi, i, OFF_SMALL // SMALL_W))]
    args = [proj3, proj3, proj3, proj3]
    if has_vres:
        in_specs.append(pl.BlockSpec((None, c, w), lambda bi, i: (bi, i, 0)))
        args.append(vfirst)
    in_specs += [pl.BlockSpec((3, w), fix), pl.BlockSpec((1, SMALL_W), fix), row_w,
                 pl.BlockSpec((SMALL_W, w), fix), row_w, pl.BlockSpec((SMALL_W, w), fix),
                 pl.BlockSpec((SMALL_W, w), fix), row_w, row_w, row_w, row_w, row_w]
    args += [p["mu"], p["mu_sm"], p["w0"], p["w2"], p["a0"], p["a2"], p["g2"],
             p["k_k"], p["k_a"], p["r_k"], p["lnx_g"], p["lnx_b"]]
    if has_vres:
        in_specs += [row_w, pl.BlockSpec((w, LANES), fix), pl.BlockSpec((LANES, w), fix)]
        args += [p["v0"], p["v1"], p["v2"]]
    tri = jnp.asarray(np.tril(np.ones((c, c), np.float32)), BF16)
    lane_head = np.arange(LANES) // RW_HEAD
    bd = jnp.asarray((lane_head[:, None] == lane_head[None, :]).astype(np.float32), BF16)
    in_specs += [pl.BlockSpec((c, c), fix), pl.BlockSpec((LANES, LANES), fix)]
    args += [tri, bd]
    out_blk = pl.BlockSpec((None, c, w), lambda bi, i: (bi, i, 0))
    return pl.pallas_call(
        functools.partial(_rwkv_kernel, has_vres=has_vres),
        grid=(b, s // c),
        in_specs=in_specs,
        out_specs=[out_blk, out_blk],
        out_shape=[jax.ShapeDtypeStruct((b, s, w), BF16), jax.ShapeDtypeStruct((b, s, w), BF16)],
        scratch_shapes=[pltpu.VMEM((w // RW_HEAD, RW_HEAD, RW_HEAD), F32),
                        pltpu.VMEM((3, 8, w), F32), pltpu.VMEM((1, 8, SMALL_W), F32)],
        compiler_params=_cp(("parallel", "arbitrary")),
        name="rwkv7",
    )(*args)


POOL_HALO = 16


def _pool_kernel(cur_ref, halo_ref, w_ref, sc_ref, o_ref):
    i = pl.program_id(1)
    tp = cur_ref.shape[0]
    cur = cur_ref[...].astype(F32)
    halo = jnp.where(i > 0, halo_ref[...].astype(F32), 0.0)
    ext = jnp.concatenate([halo, cur], axis=0)
    sums = {1: ext}
    win = 1
    while win < max(POOL_WINDOWS):
        sums[2 * win] = sums[win] + pltpu.roll(sums[win], win, 0)
        win *= 2
    count = (i * tp + 1 + lax.broadcasted_iota(I32, (tp, 1), 0)).astype(F32)
    for gi, win in enumerate(POOL_WINDOWS):
        sl = slice(POOL_GROUP * gi, POOL_GROUP * (gi + 1))
        mean = sums[win][POOL_HALO:, sl] / jnp.minimum(count, float(win))
        y = _dot((mean - cur[:, sl]).astype(BF16), w_ref[gi])
        o_ref[:, sl] = (y * sc_ref[:, sl]).astype(BF16)


def _pool(proj3, pool_w, pool_scale, *, tp=256):
    b, s, _ = proj3.shape
    w = POOL_WIDTH
    cb = OFF_POOL // w
    per = tp // POOL_HALO
    return pl.pallas_call(
        _pool_kernel,
        grid=(b, s // tp),
        in_specs=[pl.BlockSpec((None, tp, w), lambda bi, i: (bi, i, cb)),
                  pl.BlockSpec((None, POOL_HALO, w), lambda bi, i: (bi, jnp.maximum(i * per - 1, 0), cb)),
                  pl.BlockSpec((len(POOL_WINDOWS), POOL_GROUP, POOL_GROUP), lambda bi, i: (0, 0, 0)),
                  pl.BlockSpec((1, w), lambda bi, i: (0, 0))],
        out_specs=pl.BlockSpec((None, tp, w), lambda bi, i: (bi, i, 0)),
        out_shape=jax.ShapeDtypeStruct((b, s, w), BF16),
        compiler_params=_cp(("parallel", "parallel")),
        name="pool",
    )(proj3, proj3, pool_w, pool_scale.reshape(1, w))


def _merge_kernel(ya_ref, yb_ref, yc_ref, ga_ref, gb_ref, gc_ref, wb_ref, o_ref):
    acc = jax.nn.sigmoid(ga_ref[...].astype(F32)) * _dot(ya_ref[...], wb_ref[0])
    acc = acc + jax.nn.sigmoid(gb_ref[...].astype(F32)) * _dot(yb_ref[...], wb_ref[1])
    acc = acc + jax.nn.sigmoid(gc_ref[...].astype(F32)) * _dot(yc_ref[...], wb_ref[2])
    o_ref[...] = acc.astype(BF16)


def _merge(ya, yb, yc, proj, wb, *, tm=256):
    t, w = ya.shape
    d = wb.shape[2]
    row = lambda i: (i, 0)
    gate = lambda k: pl.BlockSpec((tm, d), lambda i: (i, k))
    return pl.pallas_call(
        _merge_kernel,
        grid=(t // tm,),
        in_specs=[pl.BlockSpec((tm, w), row)] * 3 + [gate(0), gate(1), gate(2),
                                                      pl.BlockSpec((N_BRANCH, w, d), lambda i: (0, 0, 0))],
        out_specs=pl.BlockSpec((tm, d), row),
        out_shape=jax.ShapeDtypeStruct((t, d), BF16),
        compiler_params=_cp(("parallel",)),
        name="merge",
    )(ya, yb, yc, proj, proj, proj, wb)


def _xattn_kernel(h16_ref, h32_ref, wq_ref, mkv_ref, wo_ref, g_ref, b_ref, o32_ref, o16_ref, *, alpha):
    q = _dot(h16_ref[...], wq_ref[...])
    scale = XA_HEAD_DIM ** -0.5
    outs = []
    for hd in range(XA_HEADS):
        sl = slice(XA_HEAD_DIM * hd, XA_HEAD_DIM * (hd + 1))
        slv = slice(XA_WIDTH + XA_HEAD_DIM * hd, XA_WIDTH + XA_HEAD_DIM * (hd + 1))
        s = _dot_nt(q[:, sl].astype(BF16), mkv_ref[:, sl]) * scale
        e = jnp.exp(s - jnp.max(s, axis=1, keepdims=True))
        p = e / jnp.sum(e, axis=1, keepdims=True)
        outs.append(_dot(p.astype(BF16), mkv_ref[:, slv]))
    o = jnp.concatenate(outs, axis=1).astype(BF16)
    y = alpha * h32_ref[...] + _dot(o, wo_ref[...])
    out = _layer_norm_rows(y, g_ref[...], b_ref[...])
    o32_ref[...] = out
    o16_ref[...] = out.astype(BF16)


def _xattn(h16, h32, wq, memkv, wo, g, b, *, alpha, seq, tm=256):
    t, d = h16.shape
    per_b = seq // tm
    nm = memkv.shape[1]
    row = lambda i: (i, 0)
    fix = lambda i: (0, 0)
    return pl.pallas_call(
        functools.partial(_xattn_kernel, alpha=alpha),
        grid=(t // tm,),
        in_specs=[pl.BlockSpec((tm, d), row), pl.BlockSpec((tm, d), row),
                  pl.BlockSpec((d, XA_WIDTH), fix),
                  pl.BlockSpec((None, nm, 2 * XA_WIDTH), lambda i: (i // per_b, 0, 0)),
                  pl.BlockSpec((XA_WIDTH, d), fix), pl.BlockSpec((1, d), fix), pl.BlockSpec((1, d), fix)],
        out_specs=[pl.BlockSpec((tm, d), row), pl.BlockSpec((tm, d), row)],
        out_shape=[jax.ShapeDtypeStruct((t, d), F32), jax.ShapeDtypeStruct((t, d), BF16)],
        compiler_params=_cp(("parallel",)),
        name="xattn",
    )(h16, h32, wq, memkv, wo, g.reshape(1, d), b.reshape(1, d))


def _swiglu_step(x, wg_ref, wu_ref, wd_ref, acc_ref):
    gt = _dot(x, wg_ref[0])
    up = _dot(x, wu_ref[0])
    act = (gt * jax.nn.sigmoid(gt) * up).astype(BF16)
    acc_ref[...] += _dot(act, wd_ref[0])


def _ffn_ln_kernel(x_ref, wg_ref, wu_ref, wd_ref, res_ref, g_ref, b_ref, o32_ref, o16_ref, acc_ref, *, alpha):
    f = pl.program_id(1)

    @pl.when(f == 0)
    def _():
        acc_ref[...] = jnp.zeros(acc_ref.shape, F32)

    _swiglu_step(x_ref[...], wg_ref, wu_ref, wd_ref, acc_ref)

    @pl.when(f == pl.num_programs(1) - 1)
    def _():
        out = _layer_norm_rows(alpha * res_ref[...] + acc_ref[...], g_ref[...], b_ref[...])
        o32_ref[...] = out
        o16_ref[...] = out.astype(BF16)


def _ffn_ln(x, wg, wu, wd, res, g, b, *, alpha, tm=512, tf=512):
    n, d = x.shape
    ff = wg.shape[2]
    row = lambda i, f: (i, 0)
    fix = lambda i, f: (0, 0)
    return pl.pallas_call(
        functools.partial(_ffn_ln_kernel, alpha=alpha),
        grid=(n // tm, ff // tf),
        in_specs=[pl.BlockSpec((tm, d), row),
                  pl.BlockSpec((1, d, tf), lambda i, f: (0, 0, f)),
                  pl.BlockSpec((1, d, tf), lambda i, f: (0, 0, f)),
                  pl.BlockSpec((1, tf, d), lambda i, f: (0, f, 0)),
                  pl.BlockSpec((tm, d), row), pl.BlockSpec((1, d), fix), pl.BlockSpec((1, d), fix)],
        out_specs=[pl.BlockSpec((tm, d), row), pl.BlockSpec((tm, d), row)],
        out_shape=[jax.ShapeDtypeStruct((n, d), F32), jax.ShapeDtypeStruct((n, d), BF16)],
        scratch_shapes=[pltpu.VMEM((tm, d), F32)],
        compiler_params=_cp(("parallel", "arbitrary")),
        name="swiglu_ln",
    )(x, wg, wu, wd, res, g.reshape(1, d), b.reshape(1, d))


def _moe_ffn_kernel(be_ref, idx_ref, idx_next_ref, src_ref, wg_ref, wu_ref, wd_ref, o_ref,
                    xbuf, x16, acc_ref, sem):
    i = pl.program_id(0)
    f = pl.program_id(1)
    n_blk = pl.num_programs(0)
    n_f = pl.num_programs(1)
    tm = o_ref.shape[0]
    per_step = tm // MOE_NF
    slot = i % 2

    def row_copy(ids, r, s):
        return pltpu.make_async_copy(src_ref.at[pl.ds(ids[0, r], 1)], xbuf.at[s, pl.ds(r, 1)], sem.at[s])

    def start_rows(ids, s, lo, n):
        def body(r, carry):
            row_copy(ids, r, s).start()
            return carry
        lax.fori_loop(lo, lo + n, body, 0)

    def wait_rows(s):
        def body(r, carry):
            row_copy(idx_ref, 0, s).wait()
            return carry
        lax.fori_loop(0, tm, body, 0)

    @pl.when((i == 0) & (f == 0))
    def _():
        start_rows(idx_ref, 0, 0, tm)

    @pl.when(f == 0)
    def _():
        wait_rows(slot)
        x16[...] = xbuf[slot].astype(BF16)
        acc_ref[...] = jnp.zeros(acc_ref.shape, F32)

    @pl.when(i + 1 < n_blk)
    def _():
        start_rows(idx_next_ref, 1 - slot, f * per_step, per_step)

    _swiglu_step(x16[...], wg_ref, wu_ref, wd_ref, acc_ref)

    @pl.when(f == n_f - 1)
    def _():
        o_ref[...] = acc_ref[...]


def _moe_ffn(block_expert, tok_pad, src, wg, wu, wd, *, tm):
    n = tok_pad.shape[0]
    d = src.shape[1]
    ff = wg.shape[2]
    tf = ff // MOE_NF
    assert ff % MOE_NF == 0 and tf % LANES == 0 and tm % MOE_NF == 0
    n_blk = n // tm
    ids = tok_pad.reshape(n_blk, 1, tm)
    return pl.pallas_call(
        _moe_ffn_kernel,
        grid_spec=pltpu.PrefetchScalarGridSpec(
            num_scalar_prefetch=1, grid=(n_blk, MOE_NF),
            in_specs=[pl.BlockSpec((None, 1, tm), lambda i, f, be: (i, 0, 0), memory_space=pltpu.SMEM),
                      pl.BlockSpec((None, 1, tm), lambda i, f, be: (jnp.minimum(i + 1, n_blk - 1), 0, 0),
                                   memory_space=pltpu.SMEM),
                      pl.BlockSpec(memory_space=pl.ANY),
                      pl.BlockSpec((1, d, tf), lambda i, f, be: (be[i], 0, f)),
                      pl.BlockSpec((1, d, tf), lambda i, f, be: (be[i], 0, f)),
                      pl.BlockSpec((1, tf, d), lambda i, f, be: (be[i], f, 0))],
            out_specs=pl.BlockSpec((tm, d), lambda i, f, be: (i, 0)),
            scratch_shapes=[pltpu.VMEM((2, tm, d), F32), pltpu.VMEM((tm, d), BF16), pltpu.VMEM((tm, d), F32),
                            pltpu.SemaphoreType.DMA((2,))]),
        out_shape=jax.ShapeDtypeStruct((n, d), F32),
        compiler_params=_cp(("arbitrary", "arbitrary")),
        name="moe_swiglu",
    )(block_expert, ids, ids, src, wg, wu, wd)


def _router_kernel(x_ref, w_ref, o_ref):
    logits = _dot(x_ref[...], w_ref[...])
    lane = lax.broadcasted_iota(I32, logits.shape, 1)
    lane_f = lane.astype(F32)
    l1 = jnp.where(lane < N_EXPERTS, logits, VERY_NEG)
    m1 = jnp.max(l1, axis=1, keepdims=True)
    i1 = jnp.min(jnp.where(l1 == m1, lane_f, float(LANES)), axis=1, keepdims=True)
    l2 = jnp.where(lane_f == i1, VERY_NEG, l1)
    m2 = jnp.max(l2, axis=1, keepdims=True)
    i2 = jnp.min(jnp.where(l2 == m2, lane_f, float(LANES)), axis=1, keepdims=True)
    e = jnp.exp(m2 - m1)
    g1 = 1.0 / (1.0 + e)
    g2 = e / (1.0 + e)
    o_ref[...] = jnp.where(lane == 0, i1, jnp.where(lane == 1, i2, jnp.where(lane == 2, g1,
                           jnp.where(lane == 3, g2, 0.0))))


def _router(h16, wr, *, tm=512):
    t, d = h16.shape
    return pl.pallas_call(
        _router_kernel,
        grid=(t // tm,),
        in_specs=[pl.BlockSpec((tm, d), lambda i: (i, 0)), pl.BlockSpec((d, LANES), lambda i: (0, 0))],
        out_specs=pl.BlockSpec((tm, LANES), lambda i: (i, 0)),
        out_shape=jax.ShapeDtypeStruct((t, LANES), F32),
        compiler_params=_cp(("parallel",)),
        name="router",
    )(h16, wr)


GATHER_ROWS = 256


def _gather_kernel(idx_ref, src_ref, o_ref, sem):
    n = o_ref.shape[0]

    def row_copy(r, src_row):
        return pltpu.make_async_copy(src_ref.at[pl.ds(src_row, 1)], o_ref.at[pl.ds(r, 1)], sem)

    def start(r, carry):
        row_copy(r, idx_ref[0, r]).start()
        return carry

    def wait(r, carry):
        row_copy(r, 0).wait()
        return carry

    lax.fori_loop(0, n, start, 0)
    lax.fori_loop(0, n, wait, 0)


def _gather_rows(src, idx):
    n = idx.shape[0]
    d = src.shape[1]
    rows = GATHER_ROWS
    return pl.pallas_call(
        _gather_kernel,
        grid=(n // rows,),
        in_specs=[pl.BlockSpec((None, 1, rows), lambda i: (i, 0, 0), memory_space=pltpu.SMEM),
                  pl.BlockSpec(memory_space=pl.ANY)],
        out_specs=pl.BlockSpec((rows, d), lambda i: (i, 0)),
        out_shape=jax.ShapeDtypeStruct((n, d), src.dtype),
        scratch_shapes=[pltpu.SemaphoreType.DMA(())],
        compiler_params=_cp(("arbitrary",)),
        name="gather_rows",
    )(idx.reshape(n // rows, 1, rows), src)


def _combine_kernel(y0_ref, y1_ref, rt_ref, res_ref, g_ref, b_ref, o_ref, *, alpha):
    f = y0_ref[...] * rt_ref[:, 2:3] + y1_ref[...] * rt_ref[:, 3:4]
    o_ref[...] = _layer_norm_rows(alpha * res_ref[...] + f, g_ref[...], b_ref[...])


def _combine_ln(y_k, rt, res, g, b, *, alpha, tm=256):
    t, d = res.shape
    row = lambda i: (i, 0)
    fix = lambda i: (0, 0)
    return pl.pallas_call(
        functools.partial(_combine_kernel, alpha=alpha),
        grid=(t // tm,),
        in_specs=[pl.BlockSpec((None, tm, d), lambda i: (0, i, 0)), pl.BlockSpec((None, tm, d), lambda i: (1, i, 0)),
                  pl.BlockSpec((tm, LANES), row), pl.BlockSpec((tm, d), row),
                  pl.BlockSpec((1, d), fix), pl.BlockSpec((1, d), fix)],
        out_specs=pl.BlockSpec((tm, d), row),
        out_shape=jax.ShapeDtypeStruct((t, d), F32),
        compiler_params=_cp(("parallel",)),
        name="moe_combine_ln",
    )(y_k, y_k, rt, res, g.reshape(1, d), b.reshape(1, d))


def _moe(h16, h32, router, w_gate, w_up, w_down, g, b, *, alpha):
    t, d = h32.shape
    tm = MOE_TM
    wr = jnp.pad(router, ((0, 0), (0, LANES - N_EXPERTS))).astype(BF16)
    rt = _router(h16, wr)
    expert = rt[:, 0:TOP_K].astype(I32).reshape(-1)
    n_assign = t * TOP_K
    onehot = (expert[:, None] == jnp.arange(N_EXPERTS, dtype=I32)[None, :]).astype(I32)
    csum = jnp.cumsum(onehot, axis=0)
    rank = jnp.sum(csum * onehot, axis=1) - 1
    counts = csum[-1]
    padded = (counts + tm - 1) // tm * tm
    pad_end = jnp.cumsum(padded)
    pad_start = pad_end - padded
    dest = (pad_start[expert] + rank).astype(I32)
    n_pad = n_assign + N_EXPERTS * tm
    n_blocks = n_pad // tm
    tok_pad = jnp.zeros((n_pad,), I32).at[dest].set(jnp.arange(n_assign, dtype=I32) // TOP_K)
    block_expert = jnp.minimum(
        jnp.searchsorted(pad_end, jnp.arange(n_blocks, dtype=I32) * tm, side="right"),
        N_EXPERTS - 1).astype(I32)
    y_sorted = _moe_ffn(block_expert, tok_pad, h32, w_gate, w_up, w_down, tm=tm)
    y_k = _gather_rows(y_sorted, dest.reshape(t, TOP_K).T.reshape(-1))
    return _combine_ln(y_k.reshape(TOP_K, t, d), rt, h32, g, b, alpha=alpha)


def _perm_w_in(w, d_model):
    sizes = ((NSA_WIDTH,) + (NSA_KV_WIDTH,) * 6 + (N_BRANCH * NSA_HEADS,)
             + (RW_WIDTH, RW_WIDTH, RW_WIDTH, RW_DECAY_LORA, RW_A_LORA, RW_GATE_LORA)
             + (POOL_WIDTH,) + (d_model,) * N_BRANCH)
    offs = np.concatenate([[0], np.cumsum(sizes)])
    seg = [w[:, int(offs[i]):int(offs[i + 1])] for i in range(len(sizes))]
    (q, kc, vc, ks, vs, kw, vw, ng, r, k, v, wl, al, gl, pool, ga, gb, gc) = seg
    zeros = lambda n: jnp.zeros((w.shape[0], n), w.dtype)
    cols = [ga, gb, gc, q, r, k, v, pool, kc, vc, ks, vs, kw, vw,
            ng, zeros(SM_WL - N_BRANCH * NSA_HEADS), wl, al, gl, zeros(SMALL_W - SM_GL - RW_GATE_LORA)]
    out = jnp.concatenate(cols, axis=1).astype(BF16)
    assert out.shape[1] == NP_COLS and d_model == 2048
    return out


def _rwkv_params(l, mu, w0, w2, a0, a2, g2, k_k, k_a, r_k, lnx_g, lnx_b, v0, v1, v2):
    w = RW_WIDTH
    rowv = lambda a: a.reshape(1, w).astype(F32)
    m = mu[l]
    o = np.cumsum((0, w, w, w, RW_DECAY_LORA, RW_A_LORA, RW_GATE_LORA))
    mu_sm = jnp.zeros((1, SMALL_W), F32)
    mu_sm = mu_sm.at[0, SM_WL:SM_WL + RW_DECAY_LORA].set(m[o[3]:o[4]])
    mu_sm = mu_sm.at[0, SM_AL:SM_AL + RW_A_LORA].set(m[o[4]:o[5]])
    mu_sm = mu_sm.at[0, SM_GL:SM_GL + RW_GATE_LORA].set(m[o[5]:o[6]])

    def pad_rows(a, off):
        return jnp.zeros((SMALL_W, w), F32).at[off:off + a.shape[0]].set(a).astype(BF16)

    p = dict(mu=m[0:3 * w].reshape(3, w), mu_sm=mu_sm, w0=rowv(w0[l]), w2=pad_rows(w2[l], SM_WL),
             a0=rowv(a0[l]), a2=pad_rows(a2[l], SM_AL), g2=pad_rows(g2[l], SM_GL),
             k_k=rowv(k_k[l]), k_a=rowv(k_a[l]), r_k=rowv(r_k[l]), lnx_g=rowv(lnx_g[l]), lnx_b=rowv(lnx_b[l]))
    if l > 0:
        p["v0"] = rowv(v0[l - 1])
        p["v1"] = jnp.pad(v1[l - 1], ((0, 0), (0, LANES - RW_V_LORA))).astype(BF16)
        p["v2"] = jnp.pad(v2[l - 1], ((0, LANES - RW_V_LORA), (0, 0))).astype(BF16)
    return p


def _stride_units(x2d, b, s):
    x = x2d.reshape(b, s // CMP_STRIDE, CMP_STRIDE, NSA_KV_HEADS, HEAD_DIM)
    return x.transpose(0, 3, 1, 2, 4).reshape(b, NSA_KV_HEADS, s // CMP_STRIDE, CMP_STRIDE * HEAD_DIM)


def _token_mixer(l, h16, proj, b, s, tabs, tabs_c, prm, vfirst):
    d_model = h16.shape[1]
    t = b * s
    proj3 = proj.reshape(b, s, NP_COLS)
    c, shi, slo = tabs
    q_t, ksa, vs_t, kwh, vw_t, g_t = _nsa_prep(proj3, c, shi, slo)
    n_units = s // CMP_STRIDE
    half = CMP_STRIDE * HEAD_DIM
    kvc = proj[:, OFF_KVC:OFF_KVC + 2 * NSA_KV_WIDTH]
    cmp_out = []
    for j in range(2):
        xu = _stride_units(kvc[:, NSA_KV_WIDTH * j:NSA_KV_WIDTH * (j + 1)], b, s)
        pe = prm["nsa_cmp_pe"][l, j].reshape(2, half)
        w1 = prm["nsa_cmp_w1"][l, j].astype(BF16)
        b1 = prm["nsa_cmp_b1"][l, j].reshape(1, CMP_HIDDEN)
        w2p = jnp.pad(prm["nsa_cmp_w2"][l, j], ((0, 0), (0, LANES - HEAD_DIM))).astype(BF16)
        cmp_out.append(_compress(xu, pe, w1, b1, w2p, *tabs_c, is_key=(j == 0)))
    kc, vc_t = cmp_out
    y_a = _nsa_attention(q_t, g_t, kc, vc_t, ksa, vs_t, kwh, vw_t).reshape(t, NSA_WIDTH)
    rp = _rwkv_params(l, prm["rwkv_mu"], prm["rwkv_w0"], prm["rwkv_w2"], prm["rwkv_a0"], prm["rwkv_a2"],
                      prm["rwkv_g2"], prm["rwkv_k_k"], prm["rwkv_k_a"], prm["rwkv_r_k"], prm["rwkv_lnx_g"],
                      prm["rwkv_lnx_b"], prm["rwkv_v0"], prm["rwkv_v1"], prm["rwkv_v2"])
    y_b, v_l = _rwkv(proj3, vfirst, rp, has_vres=(l > 0))
    y_c = _pool(proj3, prm["pool_w"][l].astype(BF16), prm["pool_scale"][l])
    merged = _merge(y_a, y_b.reshape(t, RW_WIDTH), y_c.reshape(t, POOL_WIDTH), proj,
                    prm["w_branch"][l].astype(BF16))
    return merged, v_l


def kernel(x, mem, positions, w_in, nsa_cmp_pe, nsa_cmp_w1, nsa_cmp_b1, nsa_cmp_w2, rwkv_mu, rwkv_w0, rwkv_w2, rwkv_a0, rwkv_a2, rwkv_g2, rwkv_k_k, rwkv_k_a, rwkv_r_k, rwkv_lnx_g, rwkv_lnx_b, rwkv_v0, rwkv_v1, rwkv_v2, pool_w, pool_scale, w_branch, w_out, mem_ln_g, mem_ln_b, mem_wkv, xa_wq, xa_wo, ln_g, ln_b, ffn_w_gate, ffn_w_up, ffn_w_down, moe_router, moe_w_gate, moe_w_up, moe_w_down):
    prm = dict(nsa_cmp_pe=nsa_cmp_pe, nsa_cmp_w1=nsa_cmp_w1, nsa_cmp_b1=nsa_cmp_b1, nsa_cmp_w2=nsa_cmp_w2,
               rwkv_mu=rwkv_mu, rwkv_w0=rwkv_w0, rwkv_w2=rwkv_w2, rwkv_a0=rwkv_a0, rwkv_a2=rwkv_a2,
               rwkv_g2=rwkv_g2, rwkv_k_k=rwkv_k_k, rwkv_k_a=rwkv_k_a, rwkv_r_k=rwkv_r_k,
               rwkv_lnx_g=rwkv_lnx_g, rwkv_lnx_b=rwkv_lnx_b, rwkv_v0=rwkv_v0, rwkv_v1=rwkv_v1,
               rwkv_v2=rwkv_v2, pool_w=pool_w, pool_scale=pool_scale, w_branch=w_branch)
    b, s, d = x.shape
    t = b * s
    depth = w_in.shape[0]
    alpha = (2 * depth) ** 0.25
    n_mem = mem.shape[1]

    inv_freq = ROPE_THETA ** (-jnp.arange(ROPE_HALF, dtype=F32) / ROPE_HALF)
    dd = np.arange(LANES) % HEAD_DIM
    invf_lane = jnp.where(jnp.asarray(dd < ROPE_DIM), inv_freq[jnp.asarray(dd % ROPE_HALF)], 0.0).reshape(1, LANES)
    tabs = tuple(a.reshape(b, s, LANES) for a in _rope_tables(positions.reshape(t, 1), invf_lane, ts=512))
    n_units = s // CMP_STRIDE
    pos_c = positions[:, CMP_BLOCK - 1::CMP_STRIDE]
    pos_c = jnp.concatenate([pos_c, pos_c[:, -1:]], axis=1)
    tabs_c = tuple(a.reshape(b, n_units, LANES)
                   for a in _rope_tables(pos_c.reshape(b * n_units, 1), invf_lane, ts=n_units))

    memkv = _memkv(mem.reshape(b * n_mem, d), mem_ln_g, mem_ln_b, mem_wkv.astype(BF16))
    memkv = memkv.reshape(b, n_mem, 2 * XA_WIDTH)

    h32 = x.reshape(t, d)
    h16 = h32.astype(BF16)
    vfirst = None
    for l in range(depth):
        proj = _matmul(h16, _perm_w_in(w_in[l], d), tm=512, tn=1024, out_dtype=BF16)
        merged, v_l = _token_mixer(l, h16, proj, b, s, tabs, tabs_c, prm, vfirst)
        if l == 0:
            vfirst = v_l
        h32, h16 = _matmul_res_ln(merged, w_out[l].astype(BF16), h32, ln_g[l, 0], ln_b[l, 0], alpha=alpha)
        h32, h16 = _xattn(h16, h32, xa_wq[l].astype(BF16), memkv, xa_wo[l].astype(BF16),
                          ln_g[l, 1], ln_b[l, 1], alpha=alpha, seq=s)
        if l % 2 == 0:
            e = l // 2
            h32, h16 = _ffn_ln(h16, ffn_w_gate[e:e + 1].astype(BF16), ffn_w_up[e:e + 1].astype(BF16),
                               ffn_w_down[e:e + 1].astype(BF16), h32, ln_g[l, 2], ln_b[l, 2], alpha=alpha)
        else:
            e = l // 2
            h32 = _moe(h16, h32, moe_router[e], moe_w_gate[e].astype(BF16), moe_w_up[e].astype(BF16),
                       moe_w_down[e].astype(BF16), ln_g[l, 2], ln_b[l, 2], alpha=alpha)
            h16 = h32.astype(BF16)
    return h32.reshape(b, s, d)
```

```python
import functools

import numpy as np
import jax
import jax.numpy as jnp
from jax import lax
from jax.experimental import pallas as pl
from jax.experimental.pallas import tpu as pltpu

F32 = jnp.float32
BF16 = jnp.bfloat16
I32 = jnp.int32

HEAD_DIM = 64
ROPE_DIM = HEAD_DIM // 4
ROPE_HALF = ROPE_DIM // 2
ROPE_THETA = 500000.0
NSA_HEADS = 16
NSA_KV_HEADS = 4
NSA_GROUP = NSA_HEADS // NSA_KV_HEADS
NSA_WIDTH = NSA_HEADS * HEAD_DIM
NSA_KV_WIDTH = NSA_KV_HEADS * HEAD_DIM
CMP_BLOCK = 32
CMP_STRIDE = 16
CMP_HIDDEN = 256
SLC_BLOCK = 64
SLC_TOPN = 8
WINDOW = 512
SEL_FORCE = 1e9
RW_HEADS = 16
RW_HEAD = 64
RW_WIDTH = RW_HEADS * RW_HEAD
RW_DECAY_LORA = 64
RW_A_LORA = 64
RW_V_LORA = 32
RW_GATE_LORA = 160
RW_LNX_EPS = 64e-5
POOL_WINDOWS = (2, 4, 8, 16)
POOL_GROUP = 256
POOL_WIDTH = POOL_GROUP * len(POOL_WINDOWS)
XA_HEADS = 4
XA_HEAD_DIM = 128
XA_WIDTH = XA_HEADS * XA_HEAD_DIM
N_EXPERTS = 8
TOP_K = 2
N_BRANCH = 3
LN_EPS = 1e-5
NEG_INF = -1e30
VERY_NEG = -3e38
MASK_BIG = 2.0 ** 100
LOG2_E = 1.4426950408889634

LANES = 128
VMEM_LIMIT = 56 * 1024 * 1024

NP_COLS = 13312
OFF_GATES = 0
OFF_Q = 6144
OFF_R = 7168
OFF_K = 8192
OFF_V = 9216
OFF_POOL = 10240
OFF_KVC = 11264
OFF_KVS = 11776
OFF_KVW = 12288
OFF_SMALL = 12800
SMALL_W = 512
SM_WL = 128
SM_AL = 192
SM_GL = 256

NSA_TQ = 128
NSA_TK = 256
ONES_ROWS = 16
RW_CHUNK = 64
MOE_TM = 512
MOE_NF = 8


def _cp(sem, vmem=VMEM_LIMIT):
    return pltpu.CompilerParams(dimension_semantics=sem, vmem_limit_bytes=vmem)


def _dot(a, b):
    return jnp.dot(a, b, preferred_element_type=F32)


def _dot_nt(a, b):
    return lax.dot_general(a, b, (((1,), (1,)), ((), ())), preferred_element_type=F32)


def _layer_norm_rows(y, g, b):
    mu = jnp.mean(y, axis=-1, keepdims=True)
    d = y - mu
    var = jnp.mean(d * d, axis=-1, keepdims=True)
    return d * lax.rsqrt(var + LN_EPS) * g + b


def _mm_kernel(x_ref, w_ref, o_ref):
    o_ref[...] = _dot(x_ref[...], w_ref[...]).astype(o_ref.dtype)


def _matmul(x, w, *, tm, tn, out_dtype):
    m, k = x.shape
    n = w.shape[1]
    return pl.pallas_call(
        _mm_kernel,
        grid=(n // tn, m // tm),
        in_specs=[pl.BlockSpec((tm, k), lambda j, i: (i, 0)),
                  pl.BlockSpec((k, tn), lambda j, i: (0, j))],
        out_specs=pl.BlockSpec((tm, tn), lambda j, i: (i, j)),
        out_shape=jax.ShapeDtypeStruct((m, n), out_dtype),
        compiler_params=_cp(("parallel", "parallel")),
        name="matmul",
    )(x, w)


def _mm_ln_kernel(x_ref, w_ref, res_ref, g_ref, b_ref, o32_ref, o16_ref, *, alpha):
    y = alpha * res_ref[...] + _dot(x_ref[...], w_ref[...])
    out = _layer_norm_rows(y, g_ref[...], b_ref[...])
    o32_ref[...] = out
    o16_ref[...] = out.astype(BF16)


def _matmul_res_ln(x, w, res, g, b, *, alpha, tm=256):
    m, k = x.shape
    n = w.shape[1]
    row = lambda i: (i, 0)
    fix = lambda i: (0, 0)
    return pl.pallas_call(
        functools.partial(_mm_ln_kernel, alpha=alpha),
        grid=(m // tm,),
        in_specs=[pl.BlockSpec((tm, k), row), pl.BlockSpec((k, n), fix), pl.BlockSpec((tm, n), row),
                  pl.BlockSpec((1, n), fix), pl.BlockSpec((1, n), fix)],
        out_specs=[pl.BlockSpec((tm, n), row), pl.BlockSpec((tm, n), row)],
        out_shape=[jax.ShapeDtypeStruct((m, n), F32), jax.ShapeDtypeStruct((m, n), BF16)],
        compiler_params=_cp(("parallel",)),
        name="matmul_res_ln",
    )(x, w, res, g.reshape(1, n), b.reshape(1, n))


def _memkv_kernel(m_ref, g_ref, b_ref, w_ref, o_ref):
    xn = _layer_norm_rows(m_ref[...], g_ref[...], b_ref[...])
    o_ref[...] = _dot(xn.astype(BF16), w_ref[...]).astype(o_ref.dtype)


def _memkv(mem2d, g, b, w, *, tm=256):
    m, d = mem2d.shape
    n = w.shape[1]
    row = lambda i: (i, 0)
    fix = lambda i: (0, 0)
    return pl.pallas_call(
        _memkv_kernel,
        grid=(m // tm,),
        in_specs=[pl.BlockSpec((tm, d), row), pl.BlockSpec((1, d), fix), pl.BlockSpec((1, d), fix),
                  pl.BlockSpec((d, n), fix)],
        out_specs=pl.BlockSpec((tm, n), row),
        out_shape=jax.ShapeDtypeStruct((m, n), BF16),
        compiler_params=_cp(("parallel",)),
        name="memkv",
    )(mem2d, g.reshape(1, d), b.reshape(1, d), w)


def _rope_table_kernel(pos_ref, invf_ref, c_ref, shi_ref, slo_ref):
    ang = pos_ref[...].astype(F32) * invf_ref[...]
    d = lax.broadcasted_iota(I32, ang.shape, 1) % HEAD_DIM
    cs = jnp.cos(ang)
    sn = jnp.sin(ang)
    c_ref[...] = jnp.where(d < ROPE_DIM, cs, 1.0)
    shi_ref[...] = jnp.where((d >= ROPE_HALF) & (d < ROPE_DIM), sn, 0.0)
    slo_ref[...] = jnp.where(d < ROPE_HALF, -sn, 0.0)


def _rope_tables(pos_col, invf_lane, *, ts):
    n = pos_col.shape[0]
    row = lambda i: (i, 0)
    out = jax.ShapeDtypeStruct((n, LANES), F32)
    return pl.pallas_call(
        _rope_table_kernel,
        grid=(n // ts,),
        in_specs=[pl.BlockSpec((ts, 1), row), pl.BlockSpec((1, LANES), lambda i: (0, 0))],
        out_specs=[pl.BlockSpec((ts, LANES), row)] * 3,
        out_shape=[out, out, out],
        compiler_params=_cp(("parallel",)),
        name="rope_tables",
    )(pos_col, invf_lane)


def _rope128(x, c, shi, slo):
    return x * c + pltpu.roll(x, ROPE_HALF, 1) * shi + pltpu.roll(x, LANES - ROPE_HALF, 1) * slo


def _nsa_prep_kernel(q_ref, ks_ref, vs_ref, kw_ref, vw_ref, sm_ref, c_ref, shi_ref, slo_ref,
                     qt_ref, ksa_ref, vst_ref, kwh_ref, vwt_ref, gt_ref):
    c = c_ref[...]
    shi = shi_ref[...]
    slo = slo_ref[...]
    ts = c.shape[0]
    scale = HEAD_DIM ** -0.5 * LOG2_E
    for j in range(NSA_WIDTH // LANES):
        sl = slice(LANES * j, LANES * (j + 1))
        qt_ref[sl, :] = (_rope128(q_ref[:, sl].astype(F32), c, shi, slo) * scale).T.astype(BF16)
    t = pl.program_id(1) * ts + lax.broadcasted_iota(I32, (ts, LANES), 0)
    lane = lax.broadcasted_iota(I32, (ts, LANES), 1)
    onehot = jnp.where(lane - HEAD_DIM == t // SLC_BLOCK, 1.0, 0.0)
    lo = lane < HEAD_DIM
    for j in range(NSA_KV_WIDTH // LANES):
        sl = slice(LANES * j, LANES * (j + 1))
        ks = _rope128(ks_ref[:, sl].astype(F32), c, shi, slo)
        kw = _rope128(kw_ref[:, sl].astype(F32), c, shi, slo).astype(BF16)
        ksa_ref[2 * j] = jnp.where(lo, ks, onehot).astype(BF16)
        ksa_ref[2 * j + 1] = jnp.where(lo, pltpu.roll(ks, HEAD_DIM, 1), onehot).astype(BF16)
        kwh_ref[2 * j] = kw[:, 0:HEAD_DIM]
        kwh_ref[2 * j + 1] = kw[:, HEAD_DIM:LANES]
    for k in range(ts // LANES):
        rs = slice(LANES * k, LANES * (k + 1))
        vst_ref[k] = vs_ref[rs, :].astype(F32).T.astype(BF16)
        vwt_ref[k] = vw_ref[rs, :].astype(F32).T.astype(BF16)
    gt_ref[...] = jax.nn.sigmoid(sm_ref[:, 0:LANES].astype(F32)).T


def _nsa_prep(proj3, c, shi, slo, *, ts=512):
    b, s, _ = proj3.shape
    kvw = NSA_KV_WIDTH
    tab = pl.BlockSpec((None, ts, LANES), lambda bi, i: (bi, i, 0))
    kv = lambda off: pl.BlockSpec((None, ts, kvw), lambda bi, i: (bi, i, off // kvw))
    vt_spec = pl.BlockSpec((None, ts // LANES, kvw, LANES), lambda bi, i: (bi, i, 0, 0))
    vt_shape = jax.ShapeDtypeStruct((b, s // LANES, kvw, LANES), BF16)
    return pl.pallas_call(
        _nsa_prep_kernel,
        grid=(b, s // ts),
        in_specs=[
            pl.BlockSpec((None, ts, NSA_WIDTH), lambda bi, i: (bi, i, OFF_Q // NSA_WIDTH)),
            kv(OFF_KVS), kv(OFF_KVS + kvw), kv(OFF_KVW), kv(OFF_KVW + kvw),
            pl.BlockSpec((None, ts, SMALL_W), lambda bi, i: (bi, i, OFF_SMALL // SMALL_W)),
            tab, tab, tab],
        out_specs=[
            pl.BlockSpec((None, NSA_WIDTH, ts), lambda bi, i: (bi, 0, i)),
            pl.BlockSpec((None, NSA_KV_HEADS, ts, LANES), lambda bi, i: (bi, 0, i, 0)),
            vt_spec,
            pl.BlockSpec((None, NSA_KV_HEADS, ts, HEAD_DIM), lambda bi, i: (bi, 0, i, 0)),
            vt_spec,
            pl.BlockSpec((None, LANES, ts), lambda bi, i: (bi, 0, i))],
        out_shape=[jax.ShapeDtypeStruct((b, NSA_WIDTH, s), BF16),
                   jax.ShapeDtypeStruct((b, NSA_KV_HEADS, s, LANES), BF16),
                   vt_shape,
                   jax.ShapeDtypeStruct((b, NSA_KV_HEADS, s, HEAD_DIM), BF16),
                   vt_shape,
                   jax.ShapeDtypeStruct((b, LANES, s), F32)],
        compiler_params=_cp(("parallel", "parallel")),
        name="nsa_prep",
    )(proj3, proj3, proj3, proj3, proj3, proj3, c, shi, slo)


def _gelu_tanh(x):
    return 0.5 * x * (1.0 + jnp.tanh(0.7978845608028654 * (x + 0.044715 * x * x * x)))


def _cmp_kernel(x_ref, pe_ref, w1_ref, b1_ref, w2_ref, c_ref, shi_ref, slo_ref, o_ref, *, is_key):
    x = x_ref[...].astype(F32)
    n = x.shape[0]
    half = x.shape[1]
    xa = (x + pe_ref[0:1, :]).astype(BF16)
    xb = (x + pe_ref[1:2, :]).astype(BF16)
    ha = _dot(xa, w1_ref[0:half, :])
    hb = _dot(xb, w1_ref[half:2 * half, :])
    hid = ha + pltpu.roll(hb, n - 1, 0) + b1_ref[...]
    out = _dot(_gelu_tanh(hid).astype(BF16), w2_ref[...])
    if is_key:
        out = _rope128(out, c_ref[...], shi_ref[...], slo_ref[...])
        o_ref[...] = out[:, 0:HEAD_DIM].astype(BF16)
    else:
        o_ref[...] = out.T[0:HEAD_DIM, :].astype(BF16)


def _compress(xu, pe, w1, b1, w2p, c, shi, slo, *, is_key):
    b, nh, n, w = xu.shape
    fix2 = lambda bi, h: (0, 0)
    tab = pl.BlockSpec((None, n, LANES), lambda bi, h: (bi, 0, 0))
    oshape = (n, HEAD_DIM) if is_key else (HEAD_DIM, n)
    return pl.pallas_call(
        functools.partial(_cmp_kernel, is_key=is_key),
        grid=(b, nh),
        in_specs=[pl.BlockSpec((None, None, n, w), lambda bi, h: (bi, h, 0, 0)),
                  pl.BlockSpec((2, w), fix2), pl.BlockSpec((2 * w, CMP_HIDDEN), fix2),
                  pl.BlockSpec((1, CMP_HIDDEN), fix2), pl.BlockSpec((CMP_HIDDEN, LANES), fix2),
                  tab, tab, tab],
        out_specs=pl.BlockSpec((None, None) + oshape, lambda bi, h: (bi, h, 0, 0)),
        out_shape=jax.ShapeDtypeStruct((b, nh) + oshape, BF16),
        compiler_params=_cp(("parallel", "parallel")),
        name="nsa_compress",
    )(xu, pe, w1, b1, w2p, c, shi, slo)


def _nsa_kernel(qt_ref, gt_ref, kc_ref, vct_ref, mt_ref, ksa_ref, vst_ref, kwh_ref, vwt_ref, o_ref):
    tq = qt_ref.shape[1]
    cols = NSA_GROUP * tq
    n_cmp = kc_ref.shape[1]
    n_slc = mt_ref.shape[0]
    i = pl.program_id(1)
    t0 = i * tq
    loc = lax.broadcasted_iota(I32, (1, tq), 1)
    loc_c = jnp.concatenate([loc] * NSA_GROUP, axis=1)
    tok = t0 + loc
    tok_c = t0 + loc_c
    cmp_end = lax.broadcasted_iota(I32, (n_cmp, 1), 0) * CMP_STRIDE + (CMP_BLOCK - 1)
    cmp_mask = cmp_end <= tok_c
    blk = lax.broadcasted_iota(I32, (n_slc, 1), 0)
    blk_f = blk.astype(F32)
    cur = tok // SLC_BLOCK
    forced = (blk == 0) | (blk == cur) | (blk == cur - 1)
    allowed = blk <= cur
    tk = NSA_TK
    n_full = t0 // tk
    last_mask = (n_full * tk + lax.broadcasted_iota(I32, (tk, 1), 0)) <= tok_c
    n_win = WINDOW + tq
    ws = pl.multiple_of(jnp.maximum(t0 - WINDOW, 0), tq)
    wt0 = ws // tq
    dwin = tok_c - (ws + lax.broadcasted_iota(I32, (n_win, 1), 0))
    win_mask = (dwin >= 0) & (dwin < WINDOW)
    cmp_to_slc_t = mt_ref[...]
    ones_rows = jnp.ones((ONES_ROWS, n_win), BF16)

    kvh = range(NSA_KV_HEADS)
    hss = [slice(HEAD_DIM * h, HEAD_DIM * (h + 1)) for h in kvh]
    qhs = [jnp.concatenate(
        [qt_ref[HEAD_DIM * (NSA_GROUP * h + g):HEAD_DIM * (NSA_GROUP * h + g + 1), :]
         for g in range(NSA_GROUP)], axis=1) for h in kvh]
    ss = [jnp.where(cmp_mask, _dot(kc_ref[h], qhs[h]), NEG_INF) for h in kvh]
    es = [jnp.where(cmp_mask, jnp.exp2(s - jnp.max(s, axis=0, keepdims=True)), 0.0) for s in ss]
    ps = [e / jnp.maximum(jnp.sum(e, axis=0, keepdims=True), 1e-30) for e in es]
    o_cs = [_dot(vct_ref[h], ps[h].astype(BF16)) for h in kvh]
    p4s = [sum(p[:, g * tq:(g + 1) * tq] for g in range(1, NSA_GROUP)) + p[:, 0:tq] for p in ps]
    p_his = [p4.astype(BF16) for p4 in p4s]
    p_los = [(p4 - p_hi.astype(F32)).astype(BF16) for p4, p_hi in zip(p4s, p_his)]
    imps = [_dot(cmp_to_slc_t, p_hi) + _dot(cmp_to_slc_t, p_lo) for p_hi, p_lo in zip(p_his, p_los)]
    imps = [jnp.where(forced, SEL_FORCE, jnp.where(allowed, imp, -SEL_FORCE)) for imp in imps]
    sels = [jnp.zeros((n_slc, tq), F32) for _ in kvh]
    for _ in range(SLC_TOPN):
        bests = [jnp.max(imp, axis=0, keepdims=True) for imp in imps]
        idxs = [jnp.min(jnp.where(imp == best, blk_f, float(n_slc)), axis=0, keepdims=True)
                for imp, best in zip(imps, bests)]
        hits = [blk_f == idx for idx in idxs]
        sels = [jnp.where(hit, 1.0, sel) for hit, sel in zip(hits, sels)]
        imps = [jnp.where(hit, VERY_NEG, imp) for hit, imp in zip(hits, imps)]
    biases = [jnp.where(sel > 0.5, 0.0, -MASK_BIG).astype(BF16) for sel in sels]
    q_augs = [jnp.concatenate([qh, jnp.concatenate([bias] * NSA_GROUP, axis=1)], axis=0)
              for qh, bias in zip(qhs, biases)]

    def sel_step(j, carry, masked):
        k0 = pl.multiple_of(j * tk, tk)
        scs = [_dot(ksa_ref[h, pl.ds(k0, tk), :], q_augs[h]) for h in kvh]
        if masked:
            scs = [jnp.where(last_mask, sc, NEG_INF) for sc in scs]
        m_news = [jnp.maximum(carry[h][0], jnp.max(scs[h], axis=0, keepdims=True)) for h in kvh]
        alphas = [jnp.exp2(carry[h][0] - m_news[h]) for h in kvh]
        pps = [jnp.exp2(scs[h] - m_news[h]).astype(BF16) for h in kvh]
        vts = [jnp.concatenate([vst_ref[j * (tk // tq) + d, hss[h], :] for d in range(tk // tq)], axis=1)
               for h in kvh]
        vts = [jnp.concatenate([vt, ones_rows[:, 0:tk]], axis=0) for vt in vts]
        accs = [alphas[h] * carry[h][1] + _dot(vts[h], pps[h]) for h in kvh]
        return tuple((m_news[h], accs[h]) for h in kvh)

    carry = tuple((jnp.full((1, cols), VERY_NEG, F32), jnp.zeros((HEAD_DIM + ONES_ROWS, cols), F32))
                  for _ in kvh)
    carry = lax.fori_loop(0, n_full, lambda j, cr: sel_step(j, cr, False), carry)
    carry = sel_step(n_full, carry, True)
    o_ss = [acc[0:HEAD_DIM] / acc[HEAD_DIM:HEAD_DIM + 1] for _, acc in carry]
    sws = [jnp.where(win_mask, _dot(kwh_ref[h, pl.ds(ws, n_win), :], qhs[h]), NEG_INF) for h in kvh]
    ews = [jnp.exp2(sw - jnp.max(sw, axis=0, keepdims=True)).astype(BF16) for sw in sws]
    vws = [jnp.concatenate([vwt_ref[wt0 + d, hss[h], :] for d in range(n_win // tq)], axis=1) for h in kvh]
    vws = [jnp.concatenate([vw, ones_rows], axis=0) for vw in vws]
    a_ws = [_dot(vw, ew) for vw, ew in zip(vws, ews)]
    o_ws = [a[0:HEAD_DIM] / a[HEAD_DIM:HEAD_DIM + 1] for a in a_ws]
    out_rows = []
    for h in kvh:
        for g in range(NSA_GROUP):
            head = NSA_GROUP * h + g
            cs = slice(g * tq, (g + 1) * tq)
            out_rows.append(gt_ref[head:head + 1, :] * o_cs[h][:, cs]
                            + gt_ref[NSA_HEADS + head:NSA_HEADS + head + 1, :] * o_ss[h][:, cs]
                            + gt_ref[2 * NSA_HEADS + head:2 * NSA_HEADS + head + 1, :] * o_ws[h][:, cs])
    o_ref[...] = jnp.concatenate(out_rows, axis=0).T.astype(BF16)


def _cmp_slc_weights(n_cmp_pad, n_cmp, n_slc_pad):
    c0 = np.arange(n_cmp_pad)[:, None] * CMP_STRIDE
    s0 = np.arange(n_slc_pad)[None, :] * SLC_BLOCK
    shared = np.minimum(c0 + CMP_BLOCK, s0 + SLC_BLOCK) - np.maximum(c0, s0)
    w = np.maximum(shared, 0) / CMP_STRIDE
    w[n_cmp:] = 0.0
    return w.astype(np.float32)


def _nsa_attention(q_t, g_t, kc, vc_t, ksa, vs_t, kwh, vw_t):
    b, _, s = q_t.shape
    tq = NSA_TQ
    n_cmp_pad = kc.shape[2]
    n_slc = HEAD_DIM
    assert SLC_TOPN <= s // SLC_BLOCK <= n_slc and s >= WINDOW + tq and tq == LANES
    assert s % NSA_TK == 0 and NSA_TK % tq == 0
    m_t = jnp.asarray(_cmp_slc_weights(n_cmp_pad, s // CMP_STRIDE - 1, n_slc).T, BF16)
    per_b4 = lambda bi, i: (bi, 0, 0, 0)
    return pl.pallas_call(
        _nsa_kernel,
        grid=(b, s // tq),
        in_specs=[
            pl.BlockSpec((None, NSA_WIDTH, tq), lambda bi, i: (bi, 0, i)),
            pl.BlockSpec((None, LANES, tq), lambda bi, i: (bi, 0, i)),
            pl.BlockSpec((None, NSA_KV_HEADS, n_cmp_pad, HEAD_DIM), per_b4),
            pl.BlockSpec((None, NSA_KV_HEADS, HEAD_DIM, n_cmp_pad), per_b4),
            pl.BlockSpec((n_slc, n_cmp_pad), lambda bi, i: (0, 0)),
            pl.BlockSpec((None, NSA_KV_HEADS, s, LANES), per_b4),
            pl.BlockSpec((None, s // LANES, NSA_KV_WIDTH, LANES), per_b4),
            pl.BlockSpec((None, NSA_KV_HEADS, s, HEAD_DIM), per_b4),
            pl.BlockSpec((None, s // LANES, NSA_KV_WIDTH, LANES), per_b4)],
        out_specs=pl.BlockSpec((None, tq, NSA_WIDTH), lambda bi, i: (bi, i, 0)),
        out_shape=jax.ShapeDtypeStruct((b, s, NSA_WIDTH), BF16),
        compiler_params=_cp(("parallel", "arbitrary")),
        name="nsa_attention",
    )(q_t, g_t, kc, vc_t, m_t, ksa, vs_t, kwh, vw_t)


def _softplus(x):
    return jnp.maximum(x, 0.0) + jnp.log(1.0 + jnp.exp(-jnp.abs(x)))


def _split3(x):
    h1 = x.astype(BF16)
    r1 = x - h1.astype(F32)
    h2 = r1.astype(BF16)
    h3 = (r1 - h2.astype(F32)).astype(BF16)
    return h1, h2, h3


def _seg_sum(x, bd):
    cols = []
    for j in range(x.shape[1] // LANES):
        xs = x[:, LANES * j:LANES * (j + 1)]
        h1, h2, h3 = _split3(xs)
        cols.append(_dot(h1, bd) + _dot(h2, bd) + _dot(h3, bd))
    return jnp.concatenate(cols, axis=1)


def _rwkv_kernel(*refs, has_vres):
    it = iter(refs)
    r_ref, k_ref, v_ref, sm_ref = next(it), next(it), next(it), next(it)
    vf_ref = next(it) if has_vres else None
    mu_ref, musm_ref, w0_ref, w2_ref, a0_ref, a2_ref, g2_ref = (next(it) for _ in range(7))
    kk_ref, ka_ref, rk_ref, lng_ref, lnb_ref = (next(it) for _ in range(5))
    if has_vres:
        v0_ref, v1_ref, v2_ref = next(it), next(it), next(it)
    tri_ref, bd_ref = next(it), next(it)
    y_ref, vo_ref = next(it), next(it)
    z_ref, prev_ref, prevsm_ref = next(it), next(it), next(it)

    c = r_ref.shape[0]
    width = r_ref.shape[1]

    @pl.when(pl.program_id(1) == 0)
    def _():
        z_ref[...] = jnp.zeros(z_ref.shape, F32)
        prev_ref[...] = jnp.zeros(prev_ref.shape, F32)
        prevsm_ref[...] = jnp.zeros(prevsm_ref.shape, F32)

    row = lax.broadcasted_iota(I32, (c, 1), 0)

    def shift_mix(x, pref, idx, mu):
        prev = pref[idx, 0:1, :]
        xs = jnp.where(row == 0, prev, pltpu.roll(x, 1, 0))
        pref[idx, 0:1, :] = x[c - 1:c, :]
        return x + (xs - x) * mu

    xr = shift_mix(r_ref[...].astype(F32), prev_ref, 0, mu_ref[0:1, :])
    xk = shift_mix(k_ref[...].astype(F32), prev_ref, 1, mu_ref[1:2, :])
    xv = shift_mix(v_ref[...].astype(F32), prev_ref, 2, mu_ref[2:3, :])
    sm = shift_mix(sm_ref[...].astype(F32), prevsm_ref, 0, musm_ref[...])

    wpre = w0_ref[...] + _dot(jnp.tanh(sm).astype(BF16), w2_ref[...])
    lw = -jnp.exp(-_softplus(-wpre) - 0.5)
    a = jax.nn.sigmoid(a0_ref[...] + _dot(sm.astype(BF16), a2_ref[...]))
    gate = _dot(jax.nn.sigmoid(sm).astype(BF16), g2_ref[...])
    if has_vres:
        lo = _dot(_dot(xv.astype(BF16), v1_ref[...]).astype(BF16), v2_ref[...])
        xv = xv + (vf_ref[...].astype(F32) - xv) * jax.nn.sigmoid(v0_ref[...] + lo)
    vo_ref[...] = xv.astype(BF16)

    bd = bd_ref[...]
    kk = xk * kk_ref[...]
    kk = kk / jnp.maximum(jnp.sqrt(_seg_sum(kk * kk, bd)), 1e-12)
    kmod = xk * (1.0 + (a - 1.0) * ka_ref[...])
    ah = -kk
    bh = kk * a

    tri = tri_ref[...]
    l1, l2, l3 = _split3(lw)
    cum = _dot(tri, l1) + _dot(tri, l2) + _dot(tri, l3)
    cumx = cum - lw
    cmid = cum[c // 2 - 1:c // 2, :]
    cend = cum[c - 1:c, :]
    e_inv = jnp.exp(cmid - cum)
    e_end = jnp.exp(cend - cum)
    a_t = (ah * jnp.exp(cumx - cmid)).astype(BF16)
    r_t = (xr * jnp.exp(cum - cmid)).astype(BF16)
    b_t = (bh * e_inv).astype(BF16)
    k_t = (kmod * e_inv).astype(BF16)
    a_0 = (ah * jnp.exp(cumx)).astype(BF16)
    r_0 = (xr * jnp.exp(cum)).astype(BF16)
    b_e = (bh * e_end).astype(BF16)
    k_e = (kmod * e_end).astype(BF16)
    w_end = jnp.exp(cend)
    bonus_in = xr * kmod * rk_ref[...]

    ri = lax.broadcasted_iota(I32, (c, c), 0)
    ci = lax.broadcasted_iota(I32, (c, c), 1)
    strict = ci < ri
    incl = ci <= ri
    n_levels = int(np.log2(c))
    assert 2 ** n_levels == c

    heads = range(width // RW_HEAD)
    sls = [slice(RW_HEAD * h, RW_HEAD * (h + 1)) for h in heads]
    zts = [z_ref[h] for h in heads]
    ztbs = [zt.astype(BF16) for zt in zts]
    vhs = [xv[:, sl] for sl in sls]
    vhbs = [vh.astype(BF16) for vh in vhs]
    m1s = [_dot_nt(jnp.concatenate([a_t[:, sl], r_t[:, sl]], axis=0),
                   jnp.concatenate([b_t[:, sl], k_t[:, sl]], axis=0)) for sl in sls]
    npows = [jnp.where(strict, m1[0:c, 0:c], 0.0).astype(BF16) for m1 in m1s]
    a_aks = [jnp.where(strict, m1[0:c, c:2 * c], 0.0).astype(BF16) for m1 in m1s]
    pqs = [jnp.concatenate([jnp.where(incl, m1[c:2 * c, 0:c], 0.0),
                            jnp.where(incl, m1[c:2 * c, c:2 * c], 0.0)], axis=1).astype(BF16) for m1 in m1s]
    us = [_dot_nt(a_0[:, sl], ztb) + _dot(a_ak, vhb)
          for sl, ztb, a_ak, vhb in zip(sls, ztbs, a_aks, vhbs)]
    y0s = [_dot_nt(r_0[:, sl], ztb) for sl, ztb in zip(sls, ztbs)]
    for lvl in range(n_levels):
        us = [u + _dot(npow, u.astype(BF16)) for u, npow in zip(us, npows)]
        if lvl + 1 < n_levels:
            npows = [_dot(npow, npow).astype(BF16) for npow in npows]
    uvs = [jnp.concatenate([u, vh], axis=0) for u, vh in zip(us, vhs)]
    ys = [y0 + _dot(pq, uv.astype(BF16)) for y0, pq, uv in zip(y0s, pqs, uvs)]
    for h in heads:
        sl = sls[h]
        bke = jnp.concatenate([b_e[:, sl], k_e[:, sl]], axis=0)
        z_ref[h] = zts[h] * w_end[:, sl] + _dot(uvs[h].T.astype(BF16), bke)
    for h in heads:
        sl = sls[h]
        y = ys[h]
        ym = jnp.mean(y, axis=1, keepdims=True)
        yd = y - ym
        yv = jnp.mean(yd * yd, axis=1, keepdims=True)
        yn = yd * lax.rsqrt(yv + RW_LNX_EPS) * lng_ref[:, sl] + lnb_ref[:, sl]
        bonus = jnp.sum(bonus_in[:, sl], axis=1, keepdims=True) * vhs[h]
        y_ref[:, sl] = ((yn + bonus) * gate[:, sl]).astype(BF16)


def _rwkv(proj3, vfirst, p, *, has_vres):
    b, s, _ = proj3.shape
    c = RW_CHUNK
    w = RW_WIDTH
    cb = lambda off: off // w
    fix = lambda bi, i: (0, 0)
    blk = lambda off: pl.BlockSpec((None, c, w), lambda bi, i: (bi, i, cb(off)))
    row_w = pl.BlockSpec((1, w), fix)
    in_specs = [blk(OFF_R), blk(OFF_K), blk(OFF_V),
                pl.BlockSpec((None, c, SMALL_W), lambda bi, i: (bi, i, OFF_SMALL // SMALL_W))]
    args = [proj3, proj3, proj3, proj3]
    if has_vres:
        in_specs.append(pl.BlockSpec((None, c, w), lambda bi, i: (bi, i, 0)))
        args.append(vfirst)
    in_specs += [pl.BlockSpec((3, w), fix), pl.BlockSpec((1, SMALL_W), fix), row_w,
                 pl.BlockSpec((SMALL_W, w), fix), row_w, pl.BlockSpec((SMALL_W, w), fix),
                 pl.BlockSpec((SMALL_W, w), fix), row_w, row_w, row_w, row_w, row_w]
    args += [p["mu"], p["mu_sm"], p["w0"], p["w2"], p["a0"], p["a2"], p["g2"],
             p["k_k"], p["k_a"], p["r_k"], p["lnx_g"], p["lnx_b"]]
    if has_vres:
        in_specs += [row_w, pl.BlockSpec((w, LANES), fix), pl.BlockSpec((LANES, w), fix)]
        args += [p["v0"], p["v1"], p["v2"]]
    tri = jnp.asarray(np.tril(np.ones((c, c), np.float32)), BF16)
    lane_head = np.arange(LANES) // RW_HEAD
    bd = jnp.asarray((lane_head[:, None] == lane_head[None, :]).astype(np.float32), BF16)
    in_specs += [pl.BlockSpec((c, c), fix), pl.BlockSpec((LANES, LANES), fix)]
    args += [tri, bd]
    out_blk = pl.BlockSpec((None, c, w), lambda bi, i: (bi, i, 0))
    return pl.pallas_call(
        functools.partial(_rwkv_kernel, has_vres=has_vres),
        grid=(b, s // c),
        in_specs=in_specs,
        out_specs=[out_blk, out_blk],
        out_shape=[jax.ShapeDtypeStruct((b, s, w), BF16), jax.ShapeDtypeStruct((b, s, w), BF16)],
        scratch_shapes=[pltpu.VMEM((w // RW_HEAD, RW_HEAD, RW_HEAD), F32),
                        pltpu.VMEM((3, 8, w), F32), pltpu.VMEM((1, 8, SMALL_W), F32)],
        compiler_params=_cp(("parallel", "arbitrary")),
        name="rwkv7",
    )(*args)


POOL_HALO = 16


def _pool_kernel(cur_ref, halo_ref, w_ref, sc_ref, o_ref):
    i = pl.program_id(1)
    tp = cur_ref.shape[0]
    cur = cur_ref[...].astype(F32)
    halo = jnp.where(i > 0, halo_ref[...].astype(F32), 0.0)
    ext = jnp.concatenate([halo, cur], axis=0)
    sums = {1: ext}
    win = 1
    while win < max(POOL_WINDOWS):
        sums[2 * win] = sums[win] + pltpu.roll(sums[win], win, 0)
        win *= 2
    count = (i * tp + 1 + lax.broadcasted_iota(I32, (tp, 1), 0)).astype(F32)
    for gi, win in enumerate(POOL_WINDOWS):
        sl = slice(POOL_GROUP * gi, POOL_GROUP * (gi + 1))
        mean = sums[win][POOL_HALO:, sl] / jnp.minimum(count, float(win))
        y = _dot((mean - cur[:, sl]).astype(BF16), w_ref[gi])
        o_ref[:, sl] = (y * sc_ref[:, sl]).astype(BF16)


def _pool(proj3, pool_w, pool_scale, *, tp=256):
    b, s, _ = proj3.shape
    w = POOL_WIDTH
    cb = OFF_POOL // w
    per = tp // POOL_HALO
    return pl.pallas_call(
        _pool_kernel,
        grid=(b, s // tp),
        in_specs=[pl.BlockSpec((None, tp, w), lambda bi, i: (bi, i, cb)),
                  pl.BlockSpec((None, POOL_HALO, w), lambda bi, i: (bi, jnp.maximum(i * per - 1, 0), cb)),
                  pl.BlockSpec((len(POOL_WINDOWS), POOL_GROUP, POOL_GROUP), lambda bi, i: (0, 0, 0)),
                  pl.BlockSpec((1, w), lambda bi, i: (0, 0))],
        out_specs=pl.BlockSpec((None, tp, w), lambda bi, i: (bi, i, 0)),
        out_shape=jax.ShapeDtypeStruct((b, s, w), BF16),
        compiler_params=_cp(("parallel", "parallel")),
        name="pool",
    )(proj3, proj3, pool_w, pool_scale.reshape(1, w))


def _merge_kernel(ya_ref, yb_ref, yc_ref, ga_ref, gb_ref, gc_ref, wb_ref, o_ref):
    acc = jax.nn.sigmoid(ga_ref[...].astype(F32)) * _dot(ya_ref[...], wb_ref[0])
    acc = acc + jax.nn.sigmoid(gb_ref[...].astype(F32)) * _dot(yb_ref[...], wb_ref[1])
    acc = acc + jax.nn.sigmoid(gc_ref[...].astype(F32)) * _dot(yc_ref[...], wb_ref[2])
    o_ref[...] = acc.astype(BF16)


def _merge(ya, yb, yc, proj, wb, *, tm=256):
    t, w = ya.shape
    d = wb.shape[2]
    row = lambda i: (i, 0)
    gate = lambda k: pl.BlockSpec((tm, d), lambda i: (i, k))
    return pl.pallas_call(
        _merge_kernel,
        grid=(t // tm,),
        in_specs=[pl.BlockSpec((tm, w), row)] * 3 + [gate(0), gate(1), gate(2),
                                                      pl.BlockSpec((N_BRANCH, w, d), lambda i: (0, 0, 0))],
        out_specs=pl.BlockSpec((tm, d), row),
        out_shape=jax.ShapeDtypeStruct((t, d), BF16),
        compiler_params=_cp(("parallel",)),
        name="merge",
    )(ya, yb, yc, proj, proj, proj, wb)


def _xattn_kernel(h16_ref, h32_ref, wq_ref, mkv_ref, wo_ref, g_ref, b_ref, o32_ref, o16_ref, *, alpha):
    q = _dot(h16_ref[...], wq_ref[...])
    scale = XA_HEAD_DIM ** -0.5
    outs = []
    for hd in range(XA_HEADS):
        sl = slice(XA_HEAD_DIM * hd, XA_HEAD_DIM * (hd + 1))
        slv = slice(XA_WIDTH + XA_HEAD_DIM * hd, XA_WIDTH + XA_HEAD_DIM * (hd + 1))
        s = _dot_nt(q[:, sl].astype(BF16), mkv_ref[:, sl]) * scale
        e = jnp.exp(s - jnp.max(s, axis=1, keepdims=True))
        p = e / jnp.sum(e, axis=1, keepdims=True)
        outs.append(_dot(p.astype(BF16), mkv_ref[:, slv]))
    o = jnp.concatenate(outs, axis=1).astype(BF16)
    y = alpha * h32_ref[...] + _dot(o, wo_ref[...])
    out = _layer_norm_rows(y, g_ref[...], b_ref[...])
    o32_ref[...] = out
    o16_ref[...] = out.astype(BF16)


def _xattn(h16, h32, wq, memkv, wo, g, b, *, alpha, seq, tm=256):
    t, d = h16.shape
    per_b = seq // tm
    nm = memkv.shape[1]
    row = lambda i: (i, 0)
    fix = lambda i: (0, 0)
    return pl.pallas_call(
        functools.partial(_xattn_kernel, alpha=alpha),
        grid=(t // tm,),
        in_specs=[pl.BlockSpec((tm, d), row), pl.BlockSpec((tm, d), row),
                  pl.BlockSpec((d, XA_WIDTH), fix),
                  pl.BlockSpec((None, nm, 2 * XA_WIDTH), lambda i: (i // per_b, 0, 0)),
                  pl.BlockSpec((XA_WIDTH, d), fix), pl.BlockSpec((1, d), fix), pl.BlockSpec((1, d), fix)],
        out_specs=[pl.BlockSpec((tm, d), row), pl.BlockSpec((tm, d), row)],
        out_shape=[jax.ShapeDtypeStruct((t, d), F32), jax.ShapeDtypeStruct((t, d), BF16)],
        compiler_params=_cp(("parallel",)),
        name="xattn",
    )(h16, h32, wq, memkv, wo, g.reshape(1, d), b.reshape(1, d))


def _swiglu_step(x, wg_ref, wu_ref, wd_ref, acc_ref):
    gt = _dot(x, wg_ref[0])
    up = _dot(x, wu_ref[0])
    act = (gt * jax.nn.sigmoid(gt) * up).astype(BF16)
    acc_ref[...] += _dot(act, wd_ref[0])


def _ffn_ln_kernel(x_ref, wg_ref, wu_ref, wd_ref, res_ref, g_ref, b_ref, o32_ref, o16_ref, acc_ref, *, alpha):
    f = pl.program_id(1)

    @pl.when(f == 0)
    def _():
        acc_ref[...] = jnp.zeros(acc_ref.shape, F32)

    _swiglu_step(x_ref[...], wg_ref, wu_ref, wd_ref, acc_ref)

    @pl.when(f == pl.num_programs(1) - 1)
    def _():
        out = _layer_norm_rows(alpha * res_ref[...] + acc_ref[...], g_ref[...], b_ref[...])
        o32_ref[...] = out
        o16_ref[...] = out.astype(BF16)


def _ffn_ln(x, wg, wu, wd, res, g, b, *, alpha, tm=512, tf=512):
    n, d = x.shape
    ff = wg.shape[2]
    row = lambda i, f: (i, 0)
    fix = lambda i, f: (0, 0)
    return pl.pallas_call(
        functools.partial(_ffn_ln_kernel, alpha=alpha),
        grid=(n // tm, ff // tf),
        in_specs=[pl.BlockSpec((tm, d), row),
                  pl.BlockSpec((1, d, tf), lambda i, f: (0, 0, f)),
                  pl.BlockSpec((1, d, tf), lambda i, f: (0, 0, f)),
                  pl.BlockSpec((1, tf, d), lambda i, f: (0, f, 0)),
                  pl.BlockSpec((tm, d), row), pl.BlockSpec((1, d), fix), pl.BlockSpec((1, d), fix)],
        out_specs=[pl.BlockSpec((tm, d), row), pl.BlockSpec((tm, d), row)],
        out_shape=[jax.ShapeDtypeStruct((n, d), F32), jax.ShapeDtypeStruct((n, d), BF16)],
        scratch_shapes=[pltpu.VMEM((tm, d), F32)],
        compiler_params=_cp(("parallel", "arbitrary")),
        name="swiglu_ln",
    )(x, wg, wu, wd, res, g.reshape(1, d), b.reshape(1, d))


def _moe_ffn_kernel(be_ref, idx_ref, idx_next_ref, dst_prev_ref, dst_ref, src_ref, wg_ref, wu_ref, wd_ref,
                    y_ref, xbuf, x16, acc_ref, ybuf, sem_in, sem_out):
    i = pl.program_id(0)
    f = pl.program_id(1)
    n_blk = pl.num_programs(0)
    n_f = pl.num_programs(1)
    tm = x16.shape[0]
    per_step = tm // MOE_NF
    slot = i % 2

    def gather_copy(ids, r, s):
        return pltpu.make_async_copy(src_ref.at[pl.ds(ids[0, r], 1)], xbuf.at[s, pl.ds(r, 1)], sem_in.at[s])

    def scatter_copy(ids, r, s):
        return pltpu.make_async_copy(ybuf.at[s, pl.ds(r, 1)], y_ref.at[pl.ds(ids[0, r], 1)], sem_out.at[s])

    def for_rows(fn):
        def body(r, carry):
            fn(r)
            return carry
        lax.fori_loop(0, tm, body, 0)

    @pl.when((i == 0) & (f == 0))
    def _():
        for_rows(lambda r: gather_copy(idx_ref, r, 0).start())
        ybuf[1] = jnp.zeros(ybuf.shape[1:], F32)

    @pl.when(f == 0)
    def _():
        for_rows(lambda r: gather_copy(idx_ref, 0, slot).wait())
        x16[...] = xbuf[slot].astype(BF16)
        acc_ref[...] = jnp.zeros(acc_ref.shape, F32)

    for k in range(per_step):
        r = f * per_step + k
        gather_copy(idx_next_ref, r, 1 - slot).start()
        scatter_copy(dst_prev_ref, r, 1 - slot).start()

    _swiglu_step(x16[...], wg_ref, wu_ref, wd_ref, acc_ref)

    @pl.when(f == n_f - 1)
    def _():
        @pl.when(i >= 1)
        def _():
            for_rows(lambda r: scatter_copy(dst_prev_ref, 0, slot).wait())
        ybuf[slot] = acc_ref[...]

    @pl.when((i == n_blk - 1) & (f == n_f - 1))
    def _():
        for_rows(lambda r: scatter_copy(dst_ref, r, slot).start())
        for_rows(lambda r: scatter_copy(dst_ref, 0, slot).wait())
        for_rows(lambda r: scatter_copy(dst_prev_ref, 0, 1 - slot).wait())
        for_rows(lambda r: gather_copy(idx_ref, 0, 1 - slot).wait())


def _moe_ffn(block_expert, tok_pad, dst_rows, src, wg, wu, wd, *, tm):
    n = tok_pad.shape[0]
    d = src.shape[1]
    ff = wg.shape[2]
    tf = ff // MOE_NF
    assert ff % MOE_NF == 0 and tf % LANES == 0 and tm % MOE_NF == 0
    n_blk = n // tm
    assert n_blk >= 2
    ids = tok_pad.reshape(n_blk, 1, tm)
    dst = jnp.concatenate([n + jnp.arange(tm, dtype=I32), dst_rows]).reshape(n_blk + 1, 1, tm)
    smem = lambda fn: pl.BlockSpec((None, 1, tm), fn, memory_space=pltpu.SMEM)
    return pl.pallas_call(
        _moe_ffn_kernel,
        grid_spec=pltpu.PrefetchScalarGridSpec(
            num_scalar_prefetch=1, grid=(n_blk, MOE_NF),
            in_specs=[smem(lambda i, f, be: (i, 0, 0)),
                      smem(lambda i, f, be: (jnp.minimum(i + 1, n_blk - 1), 0, 0)),
                      smem(lambda i, f, be: (i, 0, 0)),
                      smem(lambda i, f, be: (i + 1, 0, 0)),
                      pl.BlockSpec(memory_space=pl.ANY),
                      pl.BlockSpec((1, d, tf), lambda i, f, be: (be[i], 0, f)),
                      pl.BlockSpec((1, d, tf), lambda i, f, be: (be[i], 0, f)),
                      pl.BlockSpec((1, tf, d), lambda i, f, be: (be[i], f, 0))],
            out_specs=pl.BlockSpec(memory_space=pl.ANY),
            scratch_shapes=[pltpu.VMEM((2, tm, d), F32), pltpu.VMEM((tm, d), BF16), pltpu.VMEM((tm, d), F32),
                            pltpu.VMEM((2, tm, d), F32),
                            pltpu.SemaphoreType.DMA((2,)), pltpu.SemaphoreType.DMA((2,))]),
        out_shape=jax.ShapeDtypeStruct((n + tm, d), F32),
        compiler_params=_cp(("arbitrary", "arbitrary")),
        name="moe_swiglu",
    )(block_expert, ids, ids, dst, dst, src, wg, wu, wd)


def _router_kernel(x_ref, w_ref, o_ref):
    logits = _dot(x_ref[...], w_ref[...])
    lane = lax.broadcasted_iota(I32, logits.shape, 1)
    lane_f = lane.astype(F32)
    l1 = jnp.where(lane < N_EXPERTS, logits, VERY_NEG)
    m1 = jnp.max(l1, axis=1, keepdims=True)
    i1 = jnp.min(jnp.where(l1 == m1, lane_f, float(LANES)), axis=1, keepdims=True)
    l2 = jnp.where(lane_f == i1, VERY_NEG, l1)
    m2 = jnp.max(l2, axis=1, keepdims=True)
    i2 = jnp.min(jnp.where(l2 == m2, lane_f, float(LANES)), axis=1, keepdims=True)
    e = jnp.exp(m2 - m1)
    g1 = 1.0 / (1.0 + e)
    g2 = e / (1.0 + e)
    o_ref[...] = jnp.where(lane == 0, i1, jnp.where(lane == 1, i2, jnp.where(lane == 2, g1,
                           jnp.where(lane == 3, g2, 0.0))))


def _router(h16, wr, *, tm=512):
    t, d = h16.shape
    return pl.pallas_call(
        _router_kernel,
        grid=(t // tm,),
        in_specs=[pl.BlockSpec((tm, d), lambda i: (i, 0)), pl.BlockSpec((d, LANES), lambda i: (0, 0))],
        out_specs=pl.BlockSpec((tm, LANES), lambda i: (i, 0)),
        out_shape=jax.ShapeDtypeStruct((t, LANES), F32),
        compiler_params=_cp(("parallel",)),
        name="router",
    )(h16, wr)


def _combine_kernel(y0_ref, y1_ref, rt_ref, res_ref, g_ref, b_ref, o_ref, *, alpha):
    f = y0_ref[...] * rt_ref[:, 2:3] + y1_ref[...] * rt_ref[:, 3:4]
    o_ref[...] = _layer_norm_rows(alpha * res_ref[...] + f, g_ref[...], b_ref[...])


def _combine_ln(y, rt, res, g, b, *, alpha, tm=256):
    t, d = res.shape
    per_k = t // tm
    row = lambda i: (i, 0)
    fix = lambda i: (0, 0)
    return pl.pallas_call(
        functools.partial(_combine_kernel, alpha=alpha),
        grid=(t // tm,),
        in_specs=[pl.BlockSpec((tm, d), row), pl.BlockSpec((tm, d), lambda i: (per_k + i, 0)),
                  pl.BlockSpec((tm, LANES), row), pl.BlockSpec((tm, d), row),
                  pl.BlockSpec((1, d), fix), pl.BlockSpec((1, d), fix)],
        out_specs=pl.BlockSpec((tm, d), row),
        out_shape=jax.ShapeDtypeStruct((t, d), F32),
        compiler_params=_cp(("parallel",)),
        name="moe_combine_ln",
    )(y, y, rt, res, g.reshape(1, d), b.reshape(1, d))


def _moe(h16, h32, router, w_gate, w_up, w_down, g, b, *, alpha):
    t, d = h32.shape
    tm = MOE_TM
    wr = jnp.pad(router, ((0, 0), (0, LANES - N_EXPERTS))).astype(BF16)
    rt = _router(h16, wr)
    expert = rt[:, 0:TOP_K].astype(I32).reshape(-1)
    n_assign = t * TOP_K
    onehot = (expert[:, None] == jnp.arange(N_EXPERTS, dtype=I32)[None, :]).astype(I32)
    csum = jnp.cumsum(onehot, axis=0)
    rank = jnp.sum(csum * onehot, axis=1) - 1
    counts = csum[-1]
    padded = (counts + tm - 1) // tm * tm
    pad_end = jnp.cumsum(padded)
    pad_start = pad_end - padded
    dest = (pad_start[expert] + rank).astype(I32)
    n_pad = n_assign + N_EXPERTS * tm
    n_blocks = n_pad // tm
    tok_pad = jnp.zeros((n_pad,), I32).at[dest].set(jnp.arange(n_assign, dtype=I32) // TOP_K)
    block_expert = jnp.minimum(
        jnp.searchsorted(pad_end, jnp.arange(n_blocks, dtype=I32) * tm, side="right"),
        N_EXPERTS - 1).astype(I32)
    assign = jnp.arange(n_assign, dtype=I32)
    is_pad = jnp.ones((n_pad,), I32).at[dest].set(0)
    dst_rows = (n_assign + jnp.cumsum(is_pad) - 1).astype(I32).at[dest].set((assign % TOP_K) * t + assign // TOP_K)
    y = _moe_ffn(block_expert, tok_pad, dst_rows, h32, w_gate, w_up, w_down, tm=tm)
    return _combine_ln(y, rt, h32, g, b, alpha=alpha)


def _perm_w_in(w, d_model):
    sizes = ((NSA_WIDTH,) + (NSA_KV_WIDTH,) * 6 + (N_BRANCH * NSA_HEADS,)
             + (RW_WIDTH, RW_WIDTH, RW_WIDTH, RW_DECAY_LORA, RW_A_LORA, RW_GATE_LORA)
             + (POOL_WIDTH,) + (d_model,) * N_BRANCH)
    offs = np.concatenate([[0], np.cumsum(sizes)])
    seg = [w[:, int(offs[i]):int(offs[i + 1])] for i in range(len(sizes))]
    (q, kc, vc, ks, vs, kw, vw, ng, r, k, v, wl, al, gl, pool, ga, gb, gc) = seg
    zeros = lambda n: jnp.zeros((w.shape[0], n), w.dtype)
    cols = [ga, gb, gc, q, r, k, v, pool, kc, vc, ks, vs, kw, vw,
            ng, zeros(SM_WL - N_BRANCH * NSA_HEADS), wl, al, gl, zeros(SMALL_W - SM_GL - RW_GATE_LORA)]
    out = jnp.concatenate(cols, axis=1).astype(BF16)
    assert out.shape[1] == NP_COLS and d_model == 2048
    return out


def _rwkv_params(l, mu, w0, w2, a0, a2, g2, k_k, k_a, r_k, lnx_g, lnx_b, v0, v1, v2):
    w = RW_WIDTH
    rowv = lambda a: a.reshape(1, w).astype(F32)
    m = mu[l]
    o = np.cumsum((0, w, w, w, RW_DECAY_LORA, RW_A_LORA, RW_GATE_LORA))
    mu_sm = jnp.zeros((1, SMALL_W), F32)
    mu_sm = mu_sm.at[0, SM_WL:SM_WL + RW_DECAY_LORA].set(m[o[3]:o[4]])
    mu_sm = mu_sm.at[0, SM_AL:SM_AL + RW_A_LORA].set(m[o[4]:o[5]])
    mu_sm = mu_sm.at[0, SM_GL:SM_GL + RW_GATE_LORA].set(m[o[5]:o[6]])

    def pad_rows(a, off):
        return jnp.zeros((SMALL_W, w), F32).at[off:off + a.shape[0]].set(a).astype(BF16)

    p = dict(mu=m[0:3 * w].reshape(3, w), mu_sm=mu_sm, w0=rowv(w0[l]), w2=pad_rows(w2[l], SM_WL),
             a0=rowv(a0[l]), a2=pad_rows(a2[l], SM_AL), g2=pad_rows(g2[l], SM_GL),
             k_k=rowv(k_k[l]), k_a=rowv(k_a[l]), r_k=rowv(r_k[l]), lnx_g=rowv(lnx_g[l]), lnx_b=rowv(lnx_b[l]))
    if l > 0:
        p["v0"] = rowv(v0[l - 1])
        p["v1"] = jnp.pad(v1[l - 1], ((0, 0), (0, LANES - RW_V_LORA))).astype(BF16)
        p["v2"] = jnp.pad(v2[l - 1], ((0, LANES - RW_V_LORA), (0, 0))).astype(BF16)
    return p


def _stride_units(x2d, b, s):
    x = x2d.reshape(b, s // CMP_STRIDE, CMP_STRIDE, NSA_KV_HEADS, HEAD_DIM)
    return x.transpose(0, 3, 1, 2, 4).reshape(b, NSA_KV_HEADS, s // CMP_STRIDE, CMP_STRIDE * HEAD_DIM)


def _token_mixer(l, h16, proj, b, s, tabs, tabs_c, prm, vfirst):
    d_model = h16.shape[1]
    t = b * s
    proj3 = proj.reshape(b, s, NP_COLS)
    c, shi, slo = tabs
    q_t, ksa, vs_t, kwh, vw_t, g_t = _nsa_prep(proj3, c, shi, slo)
    n_units = s // CMP_STRIDE
    half = CMP_STRIDE * HEAD_DIM
    kvc = proj[:, OFF_KVC:OFF_KVC + 2 * NSA_KV_WIDTH]
    cmp_out = []
    for j in range(2):
        xu = _stride_units(kvc[:, NSA_KV_WIDTH * j:NSA_KV_WIDTH * (j + 1)], b, s)
        pe = prm["nsa_cmp_pe"][l, j].reshape(2, half)
        w1 = prm["nsa_cmp_w1"][l, j].astype(BF16)
        b1 = prm["nsa_cmp_b1"][l, j].reshape(1, CMP_HIDDEN)
        w2p = jnp.pad(prm["nsa_cmp_w2"][l, j], ((0, 0), (0, LANES - HEAD_DIM))).astype(BF16)
        cmp_out.append(_compress(xu, pe, w1, b1, w2p, *tabs_c, is_key=(j == 0)))
    kc, vc_t = cmp_out
    y_a = _nsa_attention(q_t, g_t, kc, vc_t, ksa, vs_t, kwh, vw_t).reshape(t, NSA_WIDTH)
    rp = _rwkv_params(l, prm["rwkv_mu"], prm["rwkv_w0"], prm["rwkv_w2"], prm["rwkv_a0"], prm["rwkv_a2"],
                      prm["rwkv_g2"], prm["rwkv_k_k"], prm["rwkv_k_a"], prm["rwkv_r_k"], prm["rwkv_lnx_g"],
                      prm["rwkv_lnx_b"], prm["rwkv_v0"], prm["rwkv_v1"], prm["rwkv_v2"])
    y_b, v_l = _rwkv(proj3, vfirst, rp, has_vres=(l > 0))
    y_c = _pool(proj3, prm["pool_w"][l].astype(BF16), prm["pool_scale"][l])
    merged = _merge(y_a, y_b.reshape(t, RW_WIDTH), y_c.reshape(t, POOL_WIDTH), proj,
                    prm["w_branch"][l].astype(BF16))
    return merged, v_l


def kernel(x, mem, positions, w_in, nsa_cmp_pe, nsa_cmp_w1, nsa_cmp_b1, nsa_cmp_w2, rwkv_mu, rwkv_w0, rwkv_w2, rwkv_a0, rwkv_a2, rwkv_g2, rwkv_k_k, rwkv_k_a, rwkv_r_k, rwkv_lnx_g, rwkv_lnx_b, rwkv_v0, rwkv_v1, rwkv_v2, pool_w, pool_scale, w_branch, w_out, mem_ln_g, mem_ln_b, mem_wkv, xa_wq, xa_wo, ln_g, ln_b, ffn_w_gate, ffn_w_up, ffn_w_down, moe_router, moe_w_gate, moe_w_up, moe_w_down):
    prm = dict(nsa_cmp_pe=nsa_cmp_pe, nsa_cmp_w1=nsa_cmp_w1, nsa_cmp_b1=nsa_cmp_b1, nsa_cmp_w2=nsa_cmp_w2,
               rwkv_mu=rwkv_mu, rwkv_w0=rwkv_w0, rwkv_w2=rwkv_w2, rwkv_a0=rwkv_a0, rwkv_a2=rwkv_a2,
               rwkv_g2=rwkv_g2, rwkv_k_k=rwkv_k_k, rwkv_k_a=rwkv_k_a, rwkv_r_k=rwkv_r_k,
               rwkv_lnx_g=rwkv_lnx_g, rwkv_lnx_b=rwkv_lnx_b, rwkv_v0=rwkv_v0, rwkv_v1=rwkv_v1,
               rwkv_v2=rwkv_v2, pool_w=pool_w, pool_scale=pool_scale, w_branch=w_branch)
    b, s, d = x.shape
    t = b * s
    depth = w_in.shape[0]
    alpha = (2 * depth) ** 0.25
    n_mem = mem.shape[1]

    inv_freq = ROPE_THETA ** (-jnp.arange(ROPE_HALF, dtype=F32) / ROPE_HALF)
    dd = np.arange(LANES) % HEAD_DIM
    invf_lane = jnp.where(jnp.asarray(dd < ROPE_DIM), inv_freq[jnp.asarray(dd % ROPE_HALF)], 0.0).reshape(1, LANES)
    tabs = tuple(a.reshape(b, s, LANES) for a in _rope_tables(positions.reshape(t, 1), invf_lane, ts=512))
    n_units = s // CMP_STRIDE
    pos_c = positions[:, CMP_BLOCK - 1::CMP_STRIDE]
    pos_c = jnp.concatenate([pos_c, pos_c[:, -1:]], axis=1)
    tabs_c = tuple(a.reshape(b, n_units, LANES)
                   for a in _rope_tables(pos_c.reshape(b * n_units, 1), invf_lane, ts=n_units))

    memkv = _memkv(mem.reshape(b * n_mem, d), mem_ln_g, mem_ln_b, mem_wkv.astype(BF16))
    memkv = memkv.reshape(b, n_mem, 2 * XA_WIDTH)

    h32 = x.reshape(t, d)
    h16 = h32.astype(BF16)
    vfirst = None
    for l in range(depth):
        proj = _matmul(h16, _perm_w_in(w_in[l], d), tm=512, tn=1024, out_dtype=BF16)
        merged, v_l = _token_mixer(l, h16, proj, b, s, tabs, tabs_c, prm, vfirst)
        if l == 0:
            vfirst = v_l
        h32, h16 = _matmul_res_ln(merged, w_out[l].astype(BF16), h32, ln_g[l, 0], ln_b[l, 0], alpha=alpha)
        h32, h16 = _xattn(h16, h32, xa_wq[l].astype(BF16), memkv, xa_wo[l].astype(BF16),
                          ln_g[l, 1], ln_b[l, 1], alpha=alpha, seq=s)
        if l % 2 == 0:
            e = l // 2
            h32, h16 = _ffn_ln(h16, ffn_w_gate[e:e + 1].astype(BF16), ffn_w_up[e:e + 1].astype(BF16),
                               ffn_w_down[e:e + 1].astype(BF16), h32, ln_g[l, 2], ln_b[l, 2], alpha=alpha)
        else:
            e = l // 2
            h32 = _moe(h16, h32, moe_router[e], moe_w_gate[e].astype(BF16), moe_w_up[e].astype(BF16),
                       moe_w_down[e].astype(BF16), ln_g[l, 2], ln_b[l, 2], alpha=alpha)
            h16 = h32.astype(BF16)
    return h32.reshape(b, s, d)
```

```python
import functools

import numpy as np
import jax
import jax.numpy as jnp
from jax import lax
from jax.experimental import pallas as pl
from jax.experimental.pallas import tpu as pltpu

F32 = jnp.float32
BF16 = jnp.bfloat16
I32 = jnp.int32

HEAD_DIM = 64
ROPE_DIM = HEAD_DIM // 4
ROPE_HALF = ROPE_DIM // 2
ROPE_THETA = 500000.0
NSA_HEADS = 16
NSA_KV_HEADS = 4
NSA_GROUP = NSA_HEADS // NSA_KV_HEADS
NSA_WIDTH = NSA_HEADS * HEAD_DIM
NSA_KV_WIDTH = NSA_KV_HEADS * HEAD_DIM
CMP_BLOCK = 32
CMP_STRIDE = 16
CMP_HIDDEN = 256
SLC_BLOCK = 64
SLC_TOPN = 8
WINDOW = 512
SEL_FORCE = 1e9
RW_HEADS = 16
RW_HEAD = 64
RW_WIDTH = RW_HEADS * RW_HEAD
RW_DECAY_LORA = 64
RW_A_LORA = 64
RW_V_LORA = 32
RW_GATE_LORA = 160
RW_LNX_EPS = 64e-5
POOL_WINDOWS = (2, 4, 8, 16)
POOL_GROUP = 256
POOL_WIDTH = POOL_GROUP * len(POOL_WINDOWS)
XA_HEADS = 4
XA_HEAD_DIM = 128
XA_WIDTH = XA_HEADS * XA_HEAD_DIM
N_EXPERTS = 8
TOP_K = 2
N_BRANCH = 3
LN_EPS = 1e-5
NEG_INF = -1e30
VERY_NEG = -3e38
MASK_BIG = 2.0 ** 100
LOG2_E = 1.4426950408889634

LANES = 128
MXU_DIM = 256
VMEM_LIMIT = 56 * 1024 * 1024

NP_COLS = 13312
OFF_GATES = 0
OFF_Q = 6144
OFF_R = 7168
OFF_K = 8192
OFF_V = 9216
OFF_POOL = 10240
OFF_KVC = 11264
OFF_KVS = 11776
OFF_KVW = 12288
OFF_SMALL = 12800
SMALL_W = 512
SM_WL = 128
SM_AL = 192
SM_GL = 256

NSA_TQ = 128
NSA_TK = 256
ONES_ROWS = 16
RW_CHUNK = 64
MOE_TM = 512
MOE_NF = 7


def _cp(sem, vmem=VMEM_LIMIT):
    return pltpu.CompilerParams(dimension_semantics=sem, vmem_limit_bytes=vmem)


def _dot(a, b):
    return jnp.dot(a, b, preferred_element_type=F32)


def _dot_nt(a, b):
    return lax.dot_general(a, b, (((1,), (1,)), ((), ())), preferred_element_type=F32)


def _layer_norm_rows(y, g, b):
    mu = jnp.mean(y, axis=-1, keepdims=True)
    d = y - mu
    var = jnp.mean(d * d, axis=-1, keepdims=True)
    return d * lax.rsqrt(var + LN_EPS) * g + b


def _mm_kernel(x_ref, w_ref, o_ref):
    o_ref[...] = _dot(x_ref[...], w_ref[...]).astype(o_ref.dtype)


def _matmul(x, w, *, tm, tn, out_dtype):
    m, k = x.shape
    n = w.shape[1]
    return pl.pallas_call(
        _mm_kernel,
        grid=(n // tn, m // tm),
        in_specs=[pl.BlockSpec((tm, k), lambda j, i: (i, 0)),
                  pl.BlockSpec((k, tn), lambda j, i: (0, j))],
        out_specs=pl.BlockSpec((tm, tn), lambda j, i: (i, j)),
        out_shape=jax.ShapeDtypeStruct((m, n), out_dtype),
        compiler_params=_cp(("parallel", "parallel")),
        name="matmul",
    )(x, w)


def _mm_ln_kernel(x_ref, w_ref, res_ref, g_ref, b_ref, o32_ref, o16_ref, *, alpha):
    y = alpha * res_ref[...] + _dot(x_ref[...], w_ref[...])
    out = _layer_norm_rows(y, g_ref[...], b_ref[...])
    o32_ref[...] = out
    o16_ref[...] = out.astype(BF16)


def _matmul_res_ln(x, w, res, g, b, *, alpha, tm=256):
    m, k = x.shape
    n = w.shape[1]
    row = lambda i: (i, 0)
    fix = lambda i: (0, 0)
    return pl.pallas_call(
        functools.partial(_mm_ln_kernel, alpha=alpha),
        grid=(m // tm,),
        in_specs=[pl.BlockSpec((tm, k), row), pl.BlockSpec((k, n), fix), pl.BlockSpec((tm, n), row),
                  pl.BlockSpec((1, n), fix), pl.BlockSpec((1, n), fix)],
        out_specs=[pl.BlockSpec((tm, n), row), pl.BlockSpec((tm, n), row)],
        out_shape=[jax.ShapeDtypeStruct((m, n), F32), jax.ShapeDtypeStruct((m, n), BF16)],
        compiler_params=_cp(("parallel",)),
        name="matmul_res_ln",
    )(x, w, res, g.reshape(1, n), b.reshape(1, n))


def _memkv_kernel(m_ref, g_ref, b_ref, w_ref, o_ref):
    xn = _layer_norm_rows(m_ref[...], g_ref[...], b_ref[...])
    o_ref[...] = _dot(xn.astype(BF16), w_ref[...]).astype(o_ref.dtype)


def _memkv(mem2d, g, b, w, *, tm=256):
    m, d = mem2d.shape
    n = w.shape[1]
    row = lambda i: (i, 0)
    fix = lambda i: (0, 0)
    return pl.pallas_call(
        _memkv_kernel,
        grid=(m // tm,),
        in_specs=[pl.BlockSpec((tm, d), row), pl.BlockSpec((1, d), fix), pl.BlockSpec((1, d), fix),
                  pl.BlockSpec((d, n), fix)],
        out_specs=pl.BlockSpec((tm, n), row),
        out_shape=jax.ShapeDtypeStruct((m, n), BF16),
        compiler_params=_cp(("parallel",)),
        name="memkv",
    )(mem2d, g.reshape(1, d), b.reshape(1, d), w)


def _rope_table_kernel(pos_ref, invf_ref, c_ref, shi_ref, slo_ref):
    ang = pos_ref[...].astype(F32) * invf_ref[...]
    d = lax.broadcasted_iota(I32, ang.shape, 1) % HEAD_DIM
    cs = jnp.cos(ang)
    sn = jnp.sin(ang)
    c_ref[...] = jnp.where(d < ROPE_DIM, cs, 1.0)
    shi_ref[...] = jnp.where((d >= ROPE_HALF) & (d < ROPE_DIM), sn, 0.0)
    slo_ref[...] = jnp.where(d < ROPE_HALF, -sn, 0.0)


def _rope_tables(pos_col, invf_lane, *, ts):
    n = pos_col.shape[0]
    row = lambda i: (i, 0)
    out = jax.ShapeDtypeStruct((n, LANES), F32)
    return pl.pallas_call(
        _rope_table_kernel,
        grid=(n // ts,),
        in_specs=[pl.BlockSpec((ts, 1), row), pl.BlockSpec((1, LANES), lambda i: (0, 0))],
        out_specs=[pl.BlockSpec((ts, LANES), row)] * 3,
        out_shape=[out, out, out],
        compiler_params=_cp(("parallel",)),
        name="rope_tables",
    )(pos_col, invf_lane)


def _rope128(x, c, shi, slo):
    return x * c + pltpu.roll(x, ROPE_HALF, 1) * shi + pltpu.roll(x, LANES - ROPE_HALF, 1) * slo


def _nsa_prep_kernel(q_ref, ks_ref, vs_ref, kw_ref, vw_ref, sm_ref, c_ref, shi_ref, slo_ref,
                     qt_ref, ksa_ref, vst_ref, kwh_ref, vwt_ref, gt_ref):
    c = c_ref[...]
    shi = shi_ref[...]
    slo = slo_ref[...]
    ts = c.shape[0]
    scale = HEAD_DIM ** -0.5 * LOG2_E
    for j in range(NSA_WIDTH // LANES):
        sl = slice(LANES * j, LANES * (j + 1))
        qt_ref[sl, :] = (_rope128(q_ref[:, sl].astype(F32), c, shi, slo) * scale).T.astype(BF16)
    t = pl.program_id(1) * ts + lax.broadcasted_iota(I32, (ts, LANES), 0)
    lane = lax.broadcasted_iota(I32, (ts, LANES), 1)
    onehot = jnp.where(lane - HEAD_DIM == t // SLC_BLOCK, 1.0, 0.0)
    lo = lane < HEAD_DIM
    for j in range(NSA_KV_WIDTH // LANES):
        sl = slice(LANES * j, LANES * (j + 1))
        ks = _rope128(ks_ref[:, sl].astype(F32), c, shi, slo)
        kw = _rope128(kw_ref[:, sl].astype(F32), c, shi, slo).astype(BF16)
        ksa_ref[2 * j] = jnp.where(lo, ks, onehot).astype(BF16)
        ksa_ref[2 * j + 1] = jnp.where(lo, pltpu.roll(ks, HEAD_DIM, 1), onehot).astype(BF16)
        kwh_ref[2 * j] = kw[:, 0:HEAD_DIM]
        kwh_ref[2 * j + 1] = kw[:, HEAD_DIM:LANES]
    for k in range(ts // LANES):
        rs = slice(LANES * k, LANES * (k + 1))
        vst_ref[k] = vs_ref[rs, :].astype(F32).T.astype(BF16)
        vwt_ref[k] = vw_ref[rs, :].astype(F32).T.astype(BF16)
    gt_ref[...] = jax.nn.sigmoid(sm_ref[:, 0:LANES].astype(F32)).T


def _nsa_prep(proj3, c, shi, slo, *, ts=512):
    b, s, _ = proj3.shape
    kvw = NSA_KV_WIDTH
    tab = pl.BlockSpec((None, ts, LANES), lambda bi, i: (bi, i, 0))
    kv = lambda off: pl.BlockSpec((None, ts, kvw), lambda bi, i: (bi, i, off // kvw))
    vt_spec = pl.BlockSpec((None, ts // LANES, kvw, LANES), lambda bi, i: (bi, i, 0, 0))
    vt_shape = jax.ShapeDtypeStruct((b, s // LANES, kvw, LANES), BF16)
    return pl.pallas_call(
        _nsa_prep_kernel,
        grid=(b, s // ts),
        in_specs=[
            pl.BlockSpec((None, ts, NSA_WIDTH), lambda bi, i: (bi, i, OFF_Q // NSA_WIDTH)),
            kv(OFF_KVS), kv(OFF_KVS + kvw), kv(OFF_KVW), kv(OFF_KVW + kvw),
            pl.BlockSpec((None, ts, SMALL_W), lambda bi, i: (bi, i, OFF_SMALL // SMALL_W)),
            tab, tab, tab],
        out_specs=[
            pl.BlockSpec((None, NSA_WIDTH, ts), lambda bi, i: (bi, 0, i)),
            pl.BlockSpec((None, NSA_KV_HEADS, ts, LANES), lambda bi, i: (bi, 0, i, 0)),
            vt_spec,
            pl.BlockSpec((None, NSA_KV_HEADS, ts, HEAD_DIM), lambda bi, i: (bi, 0, i, 0)),
            vt_spec,
            pl.BlockSpec((None, LANES, ts), lambda bi, i: (bi, 0, i))],
        out_shape=[jax.ShapeDtypeStruct((b, NSA_WIDTH, s), BF16),
                   jax.ShapeDtypeStruct((b, NSA_KV_HEADS, s, LANES), BF16),
                   vt_shape,
                   jax.ShapeDtypeStruct((b, NSA_KV_HEADS, s, HEAD_DIM), BF16),
                   vt_shape,
                   jax.ShapeDtypeStruct((b, LANES, s), F32)],
        compiler_params=_cp(("parallel", "parallel")),
        name="nsa_prep",
    )(proj3, proj3, proj3, proj3, proj3, proj3, c, shi, slo)


def _gelu_tanh(x):
    return 0.5 * x * (1.0 + jnp.tanh(0.7978845608028654 * (x + 0.044715 * x * x * x)))


def _cmp_kernel(x_ref, pe_ref, w1_ref, b1_ref, w2_ref, c_ref, shi_ref, slo_ref, o_ref, *, is_key):
    x = x_ref[...].astype(F32)
    n = x.shape[0]
    half = x.shape[1]
    xa = (x + pe_ref[0:1, :]).astype(BF16)
    xb = (x + pe_ref[1:2, :]).astype(BF16)
    ha = _dot(xa, w1_ref[0:half, :])
    hb = _dot(xb, w1_ref[half:2 * half, :])
    hid = ha + pltpu.roll(hb, n - 1, 0) + b1_ref[...]
    out = _dot(_gelu_tanh(hid).astype(BF16), w2_ref[...])
    if is_key:
        out = _rope128(out, c_ref[...], shi_ref[...], slo_ref[...])
        o_ref[...] = out[:, 0:HEAD_DIM].astype(BF16)
    else:
        o_ref[...] = out.T[0:HEAD_DIM, :].astype(BF16)


def _compress(xu, pe, w1, b1, w2p, c, shi, slo, *, is_key):
    b, nh, n, w = xu.shape
    fix2 = lambda bi, h: (0, 0)
    tab = pl.BlockSpec((None, n, LANES), lambda bi, h: (bi, 0, 0))
    oshape = (n, HEAD_DIM) if is_key else (HEAD_DIM, n)
    return pl.pallas_call(
        functools.partial(_cmp_kernel, is_key=is_key),
        grid=(b, nh),
        in_specs=[pl.BlockSpec((None, None, n, w), lambda bi, h: (bi, h, 0, 0)),
                  pl.BlockSpec((2, w), fix2), pl.BlockSpec((2 * w, CMP_HIDDEN), fix2),
                  pl.BlockSpec((1, CMP_HIDDEN), fix2), pl.BlockSpec((CMP_HIDDEN, LANES), fix2),
                  tab, tab, tab],
        out_specs=pl.BlockSpec((None, None) + oshape, lambda bi, h: (bi, h, 0, 0)),
        out_shape=jax.ShapeDtypeStruct((b, nh) + oshape, BF16),
        compiler_params=_cp(("parallel", "parallel")),
        name="nsa_compress",
    )(xu, pe, w1, b1, w2p, c, shi, slo)


def _nsa_kernel(qt_ref, gt_ref, kc_ref, vct_ref, mt_ref, ksa_ref, vst_ref, kwh_ref, vwt_ref, o_ref):
    tq = qt_ref.shape[1]
    cols = NSA_GROUP * tq
    n_cmp = kc_ref.shape[1]
    n_slc = mt_ref.shape[0]
    i = pl.program_id(1)
    t0 = i * tq
    loc = lax.broadcasted_iota(I32, (1, tq), 1)
    loc_c = jnp.concatenate([loc] * NSA_GROUP, axis=1)
    tok = t0 + loc
    tok_c = t0 + loc_c
    cmp_end = lax.broadcasted_iota(I32, (n_cmp, 1), 0) * CMP_STRIDE + (CMP_BLOCK - 1)
    cmp_mask = cmp_end <= tok_c
    blk = lax.broadcasted_iota(I32, (n_slc, 1), 0)
    blk_f = blk.astype(F32)
    cur = tok // SLC_BLOCK
    forced = (blk == 0) | (blk == cur) | (blk == cur - 1)
    allowed = blk <= cur
    tk = NSA_TK
    n_full = t0 // tk
    last_mask = (n_full * tk + lax.broadcasted_iota(I32, (tk, 1), 0)) <= tok_c
    n_win = WINDOW + tq
    ws = pl.multiple_of(jnp.maximum(t0 - WINDOW, 0), tq)
    wt0 = ws // tq
    dwin = tok_c - (ws + lax.broadcasted_iota(I32, (n_win, 1), 0))
    win_mask = (dwin >= 0) & (dwin < WINDOW)
    cmp_to_slc_t = mt_ref[...]
    ones_rows = jnp.ones((ONES_ROWS, n_win), BF16)

    kvh = range(NSA_KV_HEADS)
    hss = [slice(HEAD_DIM * h, HEAD_DIM * (h + 1)) for h in kvh]
    qhs = [jnp.concatenate(
        [qt_ref[HEAD_DIM * (NSA_GROUP * h + g):HEAD_DIM * (NSA_GROUP * h + g + 1), :]
         for g in range(NSA_GROUP)], axis=1) for h in kvh]
    ss = [jnp.where(cmp_mask, _dot(kc_ref[h], qhs[h]), NEG_INF) for h in kvh]
    es = [jnp.where(cmp_mask, jnp.exp2(s - jnp.max(s, axis=0, keepdims=True)), 0.0) for s in ss]
    ps = [e / jnp.maximum(jnp.sum(e, axis=0, keepdims=True), 1e-30) for e in es]
    o_cs = [_dot(vct_ref[h], ps[h].astype(BF16)) for h in kvh]
    p4s = [sum(p[:, g * tq:(g + 1) * tq] for g in range(1, NSA_GROUP)) + p[:, 0:tq] for p in ps]
    p_his = [p4.astype(BF16) for p4 in p4s]
    p_los = [(p4 - p_hi.astype(F32)).astype(BF16) for p4, p_hi in zip(p4s, p_his)]
    imps = [_dot(cmp_to_slc_t, p_hi) + _dot(cmp_to_slc_t, p_lo) for p_hi, p_lo in zip(p_his, p_los)]
    imps = [jnp.where(forced, SEL_FORCE, jnp.where(allowed, imp, -SEL_FORCE)) for imp in imps]
    sels = [jnp.zeros((n_slc, tq), F32) for _ in kvh]
    for _ in range(SLC_TOPN):
        bests = [jnp.max(imp, axis=0, keepdims=True) for imp in imps]
        idxs = [jnp.min(jnp.where(imp == best, blk_f, float(n_slc)), axis=0, keepdims=True)
                for imp, best in zip(imps, bests)]
        hits = [blk_f == idx for idx in idxs]
        sels = [jnp.where(hit, 1.0, sel) for hit, sel in zip(hits, sels)]
        imps = [jnp.where(hit, VERY_NEG, imp) for hit, imp in zip(hits, imps)]
    biases = [jnp.where(sel > 0.5, 0.0, -MASK_BIG).astype(BF16) for sel in sels]
    q_augs = [jnp.concatenate([qh, jnp.concatenate([bias] * NSA_GROUP, axis=1)], axis=0)
              for qh, bias in zip(qhs, biases)]

    def sel_step(j, carry, masked):
        k0 = pl.multiple_of(j * tk, tk)
        scs = [_dot(ksa_ref[h, pl.ds(k0, tk), :], q_augs[h]) for h in kvh]
        if masked:
            scs = [jnp.where(last_mask, sc, NEG_INF) for sc in scs]
        m_news = [jnp.maximum(carry[h][0], jnp.max(scs[h], axis=0, keepdims=True)) for h in kvh]
        alphas = [jnp.exp2(carry[h][0] - m_news[h]) for h in kvh]
        pps = [jnp.exp2(scs[h] - m_news[h]).astype(BF16) for h in kvh]
        vts = [jnp.concatenate([vst_ref[j * (tk // tq) + d, hss[h], :] for d in range(tk // tq)], axis=1)
               for h in kvh]
        vts = [jnp.concatenate([vt, ones_rows[:, 0:tk]], axis=0) for vt in vts]
        accs = [alphas[h] * carry[h][1] + _dot(vts[h], pps[h]) for h in kvh]
        return tuple((m_news[h], accs[h]) for h in kvh)

    carry = tuple((jnp.full((1, cols), VERY_NEG, F32), jnp.zeros((HEAD_DIM + ONES_ROWS, cols), F32))
                  for _ in kvh)
    carry = lax.fori_loop(0, n_full, lambda j, cr: sel_step(j, cr, False), carry)
    carry = sel_step(n_full, carry, True)
    o_ss = [acc[0:HEAD_DIM] / acc[HEAD_DIM:HEAD_DIM + 1] for _, acc in carry]
    sws = [jnp.where(win_mask, _dot(kwh_ref[h, pl.ds(ws, n_win), :], qhs[h]), NEG_INF) for h in kvh]
    ews = [jnp.exp2(sw - jnp.max(sw, axis=0, keepdims=True)).astype(BF16) for sw in sws]
    vws = [jnp.concatenate([vwt_ref[wt0 + d, hss[h], :] for d in range(n_win // tq)], axis=1) for h in kvh]
    vws = [jnp.concatenate([vw, ones_rows], axis=0) for vw in vws]
    a_ws = [_dot(vw, ew) for vw, ew in zip(vws, ews)]
    o_ws = [a[0:HEAD_DIM] / a[HEAD_DIM:HEAD_DIM + 1] for a in a_ws]
    out_rows = []
    for h in kvh:
        for g in range(NSA_GROUP):
            head = NSA_GROUP * h + g
            cs = slice(g * tq, (g + 1) * tq)
            out_rows.append(gt_ref[head:head + 1, :] * o_cs[h][:, cs]
                            + gt_ref[NSA_HEADS + head:NSA_HEADS + head + 1, :] * o_ss[h][:, cs]
                            + gt_ref[2 * NSA_HEADS + head:2 * NSA_HEADS + head + 1, :] * o_ws[h][:, cs])
    o_ref[...] = jnp.concatenate(out_rows, axis=0).T.astype(BF16)


def _cmp_slc_weights(n_cmp_pad, n_cmp, n_slc_pad):
    c0 = np.arange(n_cmp_pad)[:, None] * CMP_STRIDE
    s0 = np.arange(n_slc_pad)[None, :] * SLC_BLOCK
    shared = np.minimum(c0 + CMP_BLOCK, s0 + SLC_BLOCK) - np.maximum(c0, s0)
    w = np.maximum(shared, 0) / CMP_STRIDE
    w[n_cmp:] = 0.0
    return w.astype(np.float32)


def _nsa_attention(q_t, g_t, kc, vc_t, ksa, vs_t, kwh, vw_t):
    b, _, s = q_t.shape
    tq = NSA_TQ
    n_cmp_pad = kc.shape[2]
    n_slc = HEAD_DIM
    assert SLC_TOPN <= s // SLC_BLOCK <= n_slc and s >= WINDOW + tq and tq == LANES
    assert s % NSA_TK == 0 and NSA_TK % tq == 0
    m_t = jnp.asarray(_cmp_slc_weights(n_cmp_pad, s // CMP_STRIDE - 1, n_slc).T, BF16)
    per_b4 = lambda bi, i: (bi, 0, 0, 0)
    return pl.pallas_call(
        _nsa_kernel,
        grid=(b, s // tq),
        in_specs=[
            pl.BlockSpec((None, NSA_WIDTH, tq), lambda bi, i: (bi, 0, i)),
            pl.BlockSpec((None, LANES, tq), lambda bi, i: (bi, 0, i)),
            pl.BlockSpec((None, NSA_KV_HEADS, n_cmp_pad, HEAD_DIM), per_b4),
            pl.BlockSpec((None, NSA_KV_HEADS, HEAD_DIM, n_cmp_pad), per_b4),
            pl.BlockSpec((n_slc, n_cmp_pad), lambda bi, i: (0, 0)),
            pl.BlockSpec((None, NSA_KV_HEADS, s, LANES), per_b4),
            pl.BlockSpec((None, s // LANES, NSA_KV_WIDTH, LANES), per_b4),
            pl.BlockSpec((None, NSA_KV_HEADS, s, HEAD_DIM), per_b4),
            pl.BlockSpec((None, s // LANES, NSA_KV_WIDTH, LANES), per_b4)],
        out_specs=pl.BlockSpec((None, tq, NSA_WIDTH), lambda bi, i: (bi, i, 0)),
        out_shape=jax.ShapeDtypeStruct((b, s, NSA_WIDTH), BF16),
        compiler_params=_cp(("parallel", "arbitrary")),
        name="nsa_attention",
    )(q_t, g_t, kc, vc_t, m_t, ksa, vs_t, kwh, vw_t)


def _softplus(x):
    return jnp.maximum(x, 0.0) + jnp.log(1.0 + jnp.exp(-jnp.abs(x)))


def _split3(x):
    h1 = x.astype(BF16)
    r1 = x - h1.astype(F32)
    h2 = r1.astype(BF16)
    h3 = (r1 - h2.astype(F32)).astype(BF16)
    return h1, h2, h3


def _seg_sum(x, bd):
    cols = []
    for j in range(x.shape[1] // LANES):
        xs = x[:, LANES * j:LANES * (j + 1)]
        h1, h2, h3 = _split3(xs)
        cols.append(_dot(h1, bd) + _dot(h2, bd) + _dot(h3, bd))
    return jnp.concatenate(cols, axis=1)


def _rwkv_kernel(*refs, has_vres):
    it = iter(refs)
    r_ref, k_ref, v_ref, sm_ref = next(it), next(it), next(it), next(it)
    vf_ref = next(it) if has_vres else None
    mu_ref, musm_ref, w0_ref, w2_ref, a0_ref, a2_ref, g2_ref = (next(it) for _ in range(7))
    kk_ref, ka_ref, rk_ref, lng_ref, lnb_ref = (next(it) for _ in range(5))
    if has_vres:
        v0_ref, v1_ref, v2_ref = next(it), next(it), next(it)
    tri_ref, bd_ref = next(it), next(it)
    y_ref, vo_ref = next(it), next(it)
    z_ref, prev_ref, prevsm_ref = next(it), next(it), next(it)

    c = r_ref.shape[0]
    width = r_ref.shape[1]

    @pl.when(pl.program_id(1) == 0)
    def _():
        z_ref[...] = jnp.zeros(z_ref.shape, F32)
        prev_ref[...] = jnp.zeros(prev_ref.shape, F32)
        prevsm_ref[...] = jnp.zeros(prevsm_ref.shape, F32)

    row = lax.broadcasted_iota(I32, (c, 1), 0)

    def shift_mix(x, pref, idx, mu):
        prev = pref[idx, 0:1, :]
        xs = jnp.where(row == 0, prev, pltpu.roll(x, 1, 0))
        pref[idx, 0:1, :] = x[c - 1:c, :]
        return x + (xs - x) * mu

    xr = shift_mix(r_ref[...].astype(F32), prev_ref, 0, mu_ref[0:1, :])
    xk = shift_mix(k_ref[...].astype(F32), prev_ref, 1, mu_ref[1:2, :])
    xv = shift_mix(v_ref[...].astype(F32), prev_ref, 2, mu_ref[2:3, :])
    sm = shift_mix(sm_ref[...].astype(F32), prevsm_ref, 0, musm_ref[...])

    wpre = w0_ref[...] + _dot(jnp.tanh(sm).astype(BF16), w2_ref[...])
    lw = -jnp.exp(-_softplus(-wpre) - 0.5)
    a = jax.nn.sigmoid(a0_ref[...] + _dot(sm.astype(BF16), a2_ref[...]))
    gate = _dot(jax.nn.sigmoid(sm).astype(BF16), g2_ref[...])
    if has_vres:
        lo = _dot(_dot(xv.astype(BF16), v1_ref[...]).astype(BF16), v2_ref[...])
        xv = xv + (vf_ref[...].astype(F32) - xv) * jax.nn.sigmoid(v0_ref[...] + lo)
    vo_ref[...] = xv.astype(BF16)

    bd = bd_ref[...]
    kk = xk * kk_ref[...]
    kk = kk / jnp.maximum(jnp.sqrt(_seg_sum(kk * kk, bd)), 1e-12)
    kmod = xk * (1.0 + (a - 1.0) * ka_ref[...])
    ah = -kk
    bh = kk * a

    tri = tri_ref[...]
    l1, l2, l3 = _split3(lw)
    cum = _dot(tri, l1) + _dot(tri, l2) + _dot(tri, l3)
    cumx = cum - lw
    cmid = cum[c // 2 - 1:c // 2, :]
    cend = cum[c - 1:c, :]
    e_inv = jnp.exp(cmid - cum)
    e_end = jnp.exp(cend - cum)
    a_t = (ah * jnp.exp(cumx - cmid)).astype(BF16)
    r_t = (xr * jnp.exp(cum - cmid)).astype(BF16)
    b_t = (bh * e_inv).astype(BF16)
    k_t = (kmod * e_inv).astype(BF16)
    a_0 = (ah * jnp.exp(cumx)).astype(BF16)
    r_0 = (xr * jnp.exp(cum)).astype(BF16)
    b_e = (bh * e_end).astype(BF16)
    k_e = (kmod * e_end).astype(BF16)
    w_end = jnp.exp(cend)
    bonus_in = xr * kmod * rk_ref[...]

    ri = lax.broadcasted_iota(I32, (c, c), 0)
    ci = lax.broadcasted_iota(I32, (c, c), 1)
    strict = ci < ri
    incl = ci <= ri
    n_levels = int(np.log2(c))
    assert 2 ** n_levels == c

    heads = range(width // RW_HEAD)
    sls = [slice(RW_HEAD * h, RW_HEAD * (h + 1)) for h in heads]
    zts = [z_ref[h] for h in heads]
    ztbs = [zt.astype(BF16) for zt in zts]
    vhs = [xv[:, sl] for sl in sls]
    vhbs = [vh.astype(BF16) for vh in vhs]
    m1s = [_dot_nt(jnp.concatenate([a_t[:, sl], r_t[:, sl]], axis=0),
                   jnp.concatenate([b_t[:, sl], k_t[:, sl]], axis=0)) for sl in sls]
    npows = [jnp.where(strict, m1[0:c, 0:c], 0.0).astype(BF16) for m1 in m1s]
    a_aks = [jnp.where(strict, m1[0:c, c:2 * c], 0.0).astype(BF16) for m1 in m1s]
    pqs = [jnp.concatenate([jnp.where(incl, m1[c:2 * c, 0:c], 0.0),
                            jnp.where(incl, m1[c:2 * c, c:2 * c], 0.0)], axis=1).astype(BF16) for m1 in m1s]
    us = [_dot_nt(a_0[:, sl], ztb) + _dot(a_ak, vhb)
          for sl, ztb, a_ak, vhb in zip(sls, ztbs, a_aks, vhbs)]
    y0s = [_dot_nt(r_0[:, sl], ztb) for sl, ztb in zip(sls, ztbs)]
    for lvl in range(n_levels):
        us = [u + _dot(npow, u.astype(BF16)) for u, npow in zip(us, npows)]
        if lvl + 1 < n_levels:
            npows = [_dot(npow, npow).astype(BF16) for npow in npows]
    uvs = [jnp.concatenate([u, vh], axis=0) for u, vh in zip(us, vhs)]
    ys = [y0 + _dot(pq, uv.astype(BF16)) for y0, pq, uv in zip(y0s, pqs, uvs)]
    for h in heads:
        sl = sls[h]
        bke = jnp.concatenate([b_e[:, sl], k_e[:, sl]], axis=0)
        z_ref[h] = zts[h] * w_end[:, sl] + _dot(uvs[h].T.astype(BF16), bke)
    for h in heads:
        sl = sls[h]
        y = ys[h]
        ym = jnp.mean(y, axis=1, keepdims=True)
        yd = y - ym
        yv = jnp.mean(yd * yd, axis=1, keepdims=True)
        yn = yd * lax.rsqrt(yv + RW_LNX_EPS) * lng_ref[:, sl] + lnb_ref[:, sl]
        bonus = jnp.sum(bonus_in[:, sl], axis=1, keepdims=True) * vhs[h]
        y_ref[:, sl] = ((yn + bonus) * gate[:, sl]).astype(BF16)


def _rwkv(proj3, vfirst, p, *, has_vres):
    b, s, _ = proj3.shape
    c = RW_CHUNK
    w = RW_WIDTH
    cb = lambda off: off // w
    fix = lambda bi, i: (0, 0)
    blk = lambda off: pl.BlockSpec((None, c, w), lambda bi, i: (bi, i, cb(off)))
    row_w = pl.BlockSpec((1, w), fix)
    in_specs = [blk(OFF_R), blk(OFF_K), blk(OFF_V),
                pl.BlockSpec((None, c, SMALL_W), lambda bi, i: (bi, i, OFF_SMALL // SMALL_W))]
    args = [proj3, proj3, proj3, proj3]
    if has_vres:
        in_specs.append(pl.BlockSpec((None, c, w), lambda bi, i: (bi, i, 0)))
        args.append(vfirst)
    in_specs += [pl.BlockSpec((3, w), fix), pl.BlockSpec((1, SMALL_W), fix), row_w,
                 pl.BlockSpec((SMALL_W, w), fix), row_w, pl.BlockSpec((SMALL_W, w), fix),
                 pl.BlockSpec((SMALL_W, w), fix), row_w, row_w, row_w, row_w, row_w]
    args += [p["mu"], p["mu_sm"], p["w0"], p["w2"], p["a0"], p["a2"], p["g2"],
             p["k_k"], p["k_a"], p["r_k"], p["lnx_g"], p["lnx_b"]]
    if has_vres:
        in_specs += [row_w, pl.BlockSpec((w, LANES), fix), pl.BlockSpec((LANES, w), fix)]
        args += [p["v0"], p["v1"], p["v2"]]
    tri = jnp.asarray(np.tril(np.ones((c, c), np.float32)), BF16)
    lane_head = np.arange(LANES) // RW_HEAD
    bd = jnp.asarray((lane_head[:, None] == lane_head[None, :]).astype(np.float32), BF16)
    in_specs += [pl.BlockSpec((c, c), fix), pl.BlockSpec((LANES, LANES), fix)]
    args += [tri, bd]
    out_blk = pl.BlockSpec((None, c, w), lambda bi, i: (bi, i, 0))
    return pl.pallas_call(
        functools.partial(_rwkv_kernel, has_vres=has_vres),
        grid=(b, s // c),
        in_specs=in_specs,
        out_specs=[out_blk, out_blk],
        out_shape=[jax.ShapeDtypeStruct((b, s, w), BF16), jax.ShapeDtypeStruct((b, s, w), BF16)],
        scratch_shapes=[pltpu.VMEM((w // RW_HEAD, RW_HEAD, RW_HEAD), F32),
                        pltpu.VMEM((3, 8, w), F32), pltpu.VMEM((1, 8, SMALL_W), F32)],
        compiler_params=_cp(("parallel", "arbitrary")),
        name="rwkv7",
    )(*args)


POOL_HALO = 16


def _pool_kernel(cur_ref, halo_ref, w_ref, sc_ref, o_ref):
    i = pl.program_id(1)
    tp = cur_ref.shape[0]
    cur = cur_ref[...].astype(F32)
    halo = jnp.where(i > 0, halo_ref[...].astype(F32), 0.0)
    ext = jnp.concatenate([halo, cur], axis=0)
    sums = {1: ext}
    win = 1
    while win < max(POOL_WINDOWS):
        sums[2 * win] = sums[win] + pltpu.roll(sums[win], win, 0)
        win *= 2
    count = (i * tp + 1 + lax.broadcasted_iota(I32, (tp, 1), 0)).astype(F32)
    for gi, win in enumerate(POOL_WINDOWS):
        sl = slice(POOL_GROUP * gi, POOL_GROUP * (gi + 1))
        mean = sums[win][POOL_HALO:, sl] / jnp.minimum(count, float(win))
        y = _dot((mean - cur[:, sl]).astype(BF16), w_ref[gi])
        o_ref[:, sl] = (y * sc_ref[:, sl]).astype(BF16)


def _pool(proj3, pool_w, pool_scale, *, tp=256):
    b, s, _ = proj3.shape
    w = POOL_WIDTH
    cb = OFF_POOL // w
    per = tp // POOL_HALO
    return pl.pallas_call(
        _pool_kernel,
        grid=(b, s // tp),
        in_specs=[pl.BlockSpec((None, tp, w), lambda bi, i: (bi, i, cb)),
                  pl.BlockSpec((None, POOL_HALO, w), lambda bi, i: (bi, jnp.maximum(i * per - 1, 0), cb)),
                  pl.BlockSpec((len(POOL_WINDOWS), POOL_GROUP, POOL_GROUP), lambda bi, i: (0, 0, 0)),
                  pl.BlockSpec((1, w), lambda bi, i: (0, 0))],
        out_specs=pl.BlockSpec((None, tp, w), lambda bi, i: (bi, i, 0)),
        out_shape=jax.ShapeDtypeStruct((b, s, w), BF16),
        compiler_params=_cp(("parallel", "parallel")),
        name="pool",
    )(proj3, proj3, pool_w, pool_scale.reshape(1, w))


def _merge_kernel(ya_ref, yb_ref, yc_ref, ga_ref, gb_ref, gc_ref, wb_ref, o_ref):
    acc = jax.nn.sigmoid(ga_ref[...].astype(F32)) * _dot(ya_ref[...], wb_ref[0])
    acc = acc + jax.nn.sigmoid(gb_ref[...].astype(F32)) * _dot(yb_ref[...], wb_ref[1])
    acc = acc + jax.nn.sigmoid(gc_ref[...].astype(F32)) * _dot(yc_ref[...], wb_ref[2])
    o_ref[...] = acc.astype(BF16)


def _merge(ya, yb, yc, proj, wb, *, tm=256):
    t, w = ya.shape
    d = wb.shape[2]
    row = lambda i: (i, 0)
    gate = lambda k: pl.BlockSpec((tm, d), lambda i: (i, k))
    return pl.pallas_call(
        _merge_kernel,
        grid=(t // tm,),
        in_specs=[pl.BlockSpec((tm, w), row)] * 3 + [gate(0), gate(1), gate(2),
                                                      pl.BlockSpec((N_BRANCH, w, d), lambda i: (0, 0, 0))],
        out_specs=pl.BlockSpec((tm, d), row),
        out_shape=jax.ShapeDtypeStruct((t, d), BF16),
        compiler_params=_cp(("parallel",)),
        name="merge",
    )(ya, yb, yc, proj, proj, proj, wb)


def _xattn_kernel(h16_ref, h32_ref, wq_ref, mkv_ref, wo_ref, g_ref, b_ref, o32_ref, o16_ref, *, alpha):
    q = _dot(h16_ref[...], wq_ref[...])
    scale = XA_HEAD_DIM ** -0.5
    outs = []
    for hd in range(XA_HEADS):
        sl = slice(XA_HEAD_DIM * hd, XA_HEAD_DIM * (hd + 1))
        slv = slice(XA_WIDTH + XA_HEAD_DIM * hd, XA_WIDTH + XA_HEAD_DIM * (hd + 1))
        s = _dot_nt(q[:, sl].astype(BF16), mkv_ref[:, sl]) * scale
        e = jnp.exp(s - jnp.max(s, axis=1, keepdims=True))
        p = e / jnp.sum(e, axis=1, keepdims=True)
        outs.append(_dot(p.astype(BF16), mkv_ref[:, slv]))
    o = jnp.concatenate(outs, axis=1).astype(BF16)
    y = alpha * h32_ref[...] + _dot(o, wo_ref[...])
    out = _layer_norm_rows(y, g_ref[...], b_ref[...])
    o32_ref[...] = out
    o16_ref[...] = out.astype(BF16)


def _xattn(h16, h32, wq, memkv, wo, g, b, *, alpha, seq, tm=256):
    t, d = h16.shape
    per_b = seq // tm
    nm = memkv.shape[1]
    row = lambda i: (i, 0)
    fix = lambda i: (0, 0)
    return pl.pallas_call(
        functools.partial(_xattn_kernel, alpha=alpha),
        grid=(t // tm,),
        in_specs=[pl.BlockSpec((tm, d), row), pl.BlockSpec((tm, d), row),
                  pl.BlockSpec((d, XA_WIDTH), fix),
                  pl.BlockSpec((None, nm, 2 * XA_WIDTH), lambda i: (i // per_b, 0, 0)),
                  pl.BlockSpec((XA_WIDTH, d), fix), pl.BlockSpec((1, d), fix), pl.BlockSpec((1, d), fix)],
        out_specs=[pl.BlockSpec((tm, d), row), pl.BlockSpec((tm, d), row)],
        out_shape=[jax.ShapeDtypeStruct((t, d), F32), jax.ShapeDtypeStruct((t, d), BF16)],
        compiler_params=_cp(("parallel",)),
        name="xattn",
    )(h16, h32, wq, memkv, wo, g.reshape(1, d), b.reshape(1, d))


def _swiglu_step(x, wg_ref, wu_ref, wd_ref, acc_ref):
    gt = _dot(x, wg_ref[0])
    up = _dot(x, wu_ref[0])
    act = (gt * jax.nn.sigmoid(gt) * up).astype(BF16)
    acc_ref[...] += _dot(act, wd_ref[0])


def _ffn_ln_kernel(x_ref, wg_ref, wu_ref, wd_ref, res_ref, g_ref, b_ref, o32_ref, o16_ref, acc_ref, *, alpha):
    f = pl.program_id(1)

    @pl.when(f == 0)
    def _():
        acc_ref[...] = jnp.zeros(acc_ref.shape, F32)

    _swiglu_step(x_ref[...], wg_ref, wu_ref, wd_ref, acc_ref)

    @pl.when(f == pl.num_programs(1) - 1)
    def _():
        out = _layer_norm_rows(alpha * res_ref[...] + acc_ref[...], g_ref[...], b_ref[...])
        o32_ref[...] = out
        o16_ref[...] = out.astype(BF16)


def _ffn_ln(x, wg, wu, wd, res, g, b, *, alpha, tm=512, tf=512):
    n, d = x.shape
    ff = wg.shape[2]
    row = lambda i, f: (i, 0)
    fix = lambda i, f: (0, 0)
    return pl.pallas_call(
        functools.partial(_ffn_ln_kernel, alpha=alpha),
        grid=(n // tm, ff // tf),
        in_specs=[pl.BlockSpec((tm, d), row),
                  pl.BlockSpec((1, d, tf), lambda i, f: (0, 0, f)),
                  pl.BlockSpec((1, d, tf), lambda i, f: (0, 0, f)),
                  pl.BlockSpec((1, tf, d), lambda i, f: (0, f, 0)),
                  pl.BlockSpec((tm, d), row), pl.BlockSpec((1, d), fix), pl.BlockSpec((1, d), fix)],
        out_specs=[pl.BlockSpec((tm, d), row), pl.BlockSpec((tm, d), row)],
        out_shape=[jax.ShapeDtypeStruct((n, d), F32), jax.ShapeDtypeStruct((n, d), BF16)],
        scratch_shapes=[pltpu.VMEM((tm, d), F32)],
        compiler_params=_cp(("parallel", "arbitrary")),
        name="swiglu_ln",
    )(x, wg, wu, wd, res, g.reshape(1, d), b.reshape(1, d))


def _moe_ffn_kernel(be_ref, idx_ref, idx_next_ref, dst_prev_ref, dst_ref, src_ref, wg_ref, wu_ref, wd_ref,
                    y_ref, xbuf, x16, acc_ref, ybuf, sem_in, sem_out):
    i = pl.program_id(0)
    f = pl.program_id(1)
    n_blk = pl.num_programs(0)
    n_f = pl.num_programs(1)
    tm = x16.shape[0]
    per_step = tm // MOE_NF
    slot = i % 2

    def gather_copy(ids, r, s):
        return pltpu.make_async_copy(src_ref.at[pl.ds(ids[0, r], 1)], xbuf.at[s, pl.ds(r, 1)], sem_in.at[s])

    def scatter_copy(ids, r, s):
        return pltpu.make_async_copy(ybuf.at[s, pl.ds(r, 1)], y_ref.at[pl.ds(ids[0, r], 1)], sem_out.at[s])

    def for_rows(fn):
        def body(r, carry):
            fn(r)
            return carry
        lax.fori_loop(0, tm, body, 0)

    def wait_gather(s):
        for _ in range(tm):
            gather_copy(idx_ref, 0, s).wait()

    def wait_scatter(s):
        for _ in range(tm):
            scatter_copy(dst_ref, 0, s).wait()

    @pl.when((i == 0) & (f == 0))
    def _():
        for_rows(lambda r: gather_copy(idx_ref, r, 0).start())
        ybuf[1] = jnp.zeros(ybuf.shape[1:], F32)

    def start_both(r):
        gather_copy(idx_next_ref, r, 1 - slot).start()
        scatter_copy(dst_prev_ref, r, 1 - slot).start()

    @pl.when(f == 0)
    def _():
        wait_gather(slot)
        x16[...] = xbuf[slot].astype(BF16)
        acc_ref[...] = jnp.zeros(acc_ref.shape, F32)
        for r in range(MOE_NF * per_step, tm):
            start_both(r)

    for k in range(per_step):
        start_both(f * per_step + k)

    _swiglu_step(x16[...], wg_ref, wu_ref, wd_ref, acc_ref)

    @pl.when(f == n_f - 1)
    def _():
        @pl.when(i >= 1)
        def _():
            wait_scatter(slot)
        ybuf[slot] = acc_ref[...]

    @pl.when((i == n_blk - 1) & (f == n_f - 1))
    def _():
        for_rows(lambda r: scatter_copy(dst_ref, r, slot).start())
        wait_scatter(slot)
        wait_scatter(1 - slot)
        wait_gather(1 - slot)


def _moe_ffn(block_expert, tok_pad, dst_rows, src, wg, wu, wd, *, tm):
    n = tok_pad.shape[0]
    d = src.shape[1]
    ff = wg.shape[2]
    tf = ff // MOE_NF
    assert ff % MOE_NF == 0 and tf % MXU_DIM == 0
    n_blk = n // tm
    assert n_blk >= 2
    ids = tok_pad.reshape(n_blk, 1, tm)
    dst = jnp.concatenate([n + jnp.arange(tm, dtype=I32), dst_rows]).reshape(n_blk + 1, 1, tm)
    smem = lambda fn: pl.BlockSpec((None, 1, tm), fn, memory_space=pltpu.SMEM)
    return pl.pallas_call(
        _moe_ffn_kernel,
        grid_spec=pltpu.PrefetchScalarGridSpec(
            num_scalar_prefetch=1, grid=(n_blk, MOE_NF),
            in_specs=[smem(lambda i, f, be: (i, 0, 0)),
                      smem(lambda i, f, be: (jnp.minimum(i + 1, n_blk - 1), 0, 0)),
                      smem(lambda i, f, be: (i, 0, 0)),
                      smem(lambda i, f, be: (i + 1, 0, 0)),
                      pl.BlockSpec(memory_space=pl.ANY),
                      pl.BlockSpec((1, d, tf), lambda i, f, be: (be[i], 0, f)),
                      pl.BlockSpec((1, d, tf), lambda i, f, be: (be[i], 0, f)),
                      pl.BlockSpec((1, tf, d), lambda i, f, be: (be[i], f, 0))],
            out_specs=pl.BlockSpec(memory_space=pl.ANY),
            scratch_shapes=[pltpu.VMEM((2, tm, d), F32), pltpu.VMEM((tm, d), BF16), pltpu.VMEM((tm, d), F32),
                            pltpu.VMEM((2, tm, d), F32),
                            pltpu.SemaphoreType.DMA((2,)), pltpu.SemaphoreType.DMA((2,))]),
        out_shape=jax.ShapeDtypeStruct((n + tm, d), F32),
        compiler_params=_cp(("arbitrary", "arbitrary")),
        name="moe_swiglu",
    )(block_expert, ids, ids, dst, dst, src, wg, wu, wd)


def _router_kernel(x_ref, w_ref, o_ref):
    logits = _dot(x_ref[...], w_ref[...])
    lane = lax.broadcasted_iota(I32, logits.shape, 1)
    lane_f = lane.astype(F32)
    l1 = jnp.where(lane < N_EXPERTS, logits, VERY_NEG)
    m1 = jnp.max(l1, axis=1, keepdims=True)
    i1 = jnp.min(jnp.where(l1 == m1, lane_f, float(LANES)), axis=1, keepdims=True)
    l2 = jnp.where(lane_f == i1, VERY_NEG, l1)
    m2 = jnp.max(l2, axis=1, keepdims=True)
    i2 = jnp.min(jnp.where(l2 == m2, lane_f, float(LANES)), axis=1, keepdims=True)
    e = jnp.exp(m2 - m1)
    g1 = 1.0 / (1.0 + e)
    g2 = e / (1.0 + e)
    o_ref[...] = jnp.where(lane == 0, i1, jnp.where(lane == 1, i2, jnp.where(lane == 2, g1,
                           jnp.where(lane == 3, g2, 0.0))))


def _router(h16, wr, *, tm=512):
    t, d = h16.shape
    return pl.pallas_call(
        _router_kernel,
        grid=(t // tm,),
        in_specs=[pl.BlockSpec((tm, d), lambda i: (i, 0)), pl.BlockSpec((d, LANES), lambda i: (0, 0))],
        out_specs=pl.BlockSpec((tm, LANES), lambda i: (i, 0)),
        out_shape=jax.ShapeDtypeStruct((t, LANES), F32),
        compiler_params=_cp(("parallel",)),
        name="router",
    )(h16, wr)


def _combine_kernel(y0_ref, y1_ref, rt_ref, res_ref, g_ref, b_ref, o_ref, *, alpha):
    f = y0_ref[...] * rt_ref[:, 2:3] + y1_ref[...] * rt_ref[:, 3:4]
    o_ref[...] = _layer_norm_rows(alpha * res_ref[...] + f, g_ref[...], b_ref[...])


def _combine_ln(y, rt, res, g, b, *, alpha, tm=256):
    t, d = res.shape
    per_k = t // tm
    row = lambda i: (i, 0)
    fix = lambda i: (0, 0)
    return pl.pallas_call(
        functools.partial(_combine_kernel, alpha=alpha),
        grid=(t // tm,),
        in_specs=[pl.BlockSpec((tm, d), row), pl.BlockSpec((tm, d), lambda i: (per_k + i, 0)),
                  pl.BlockSpec((tm, LANES), row), pl.BlockSpec((tm, d), row),
                  pl.BlockSpec((1, d), fix), pl.BlockSpec((1, d), fix)],
        out_specs=pl.BlockSpec((tm, d), row),
        out_shape=jax.ShapeDtypeStruct((t, d), F32),
        compiler_params=_cp(("parallel",)),
        name="moe_combine_ln",
    )(y, y, rt, res, g.reshape(1, d), b.reshape(1, d))


def _moe(h16, h32, router, w_gate, w_up, w_down, g, b, *, alpha):
    t, d = h32.shape
    tm = MOE_TM
    wr = jnp.pad(router, ((0, 0), (0, LANES - N_EXPERTS))).astype(BF16)
    rt = _router(h16, wr)
    expert = rt[:, 0:TOP_K].astype(I32).reshape(-1)
    n_assign = t * TOP_K
    onehot = (expert[:, None] == jnp.arange(N_EXPERTS, dtype=I32)[None, :]).astype(I32)
    csum = jnp.cumsum(onehot, axis=0)
    rank = jnp.sum(csum * onehot, axis=1) - 1
    counts = csum[-1]
    padded = (counts + tm - 1) // tm * tm
    pad_end = jnp.cumsum(padded)
    pad_start = pad_end - padded
    dest = (pad_start[expert] + rank).astype(I32)
    n_pad = n_assign + N_EXPERTS * tm
    n_blocks = n_pad // tm
    block_expert = jnp.minimum(
        jnp.searchsorted(pad_end, jnp.arange(n_blocks, dtype=I32) * tm, side="right"),
        N_EXPERTS - 1).astype(I32)
    assign = jnp.arange(n_assign, dtype=I32)
    token = assign // TOP_K
    info = jnp.stack([token, (assign % TOP_K) * t + token], axis=1)
    info = jnp.full((n_pad, 2), -1, I32).at[dest].set(info)
    is_pad = info[:, 0] < 0
    tok_pad = jnp.where(is_pad, 0, info[:, 0])
    dst_rows = jnp.where(is_pad, n_assign + jnp.cumsum(is_pad.astype(I32)) - 1, info[:, 1]).astype(I32)
    y = _moe_ffn(block_expert, tok_pad, dst_rows, h32, w_gate, w_up, w_down, tm=tm)
    return _combine_ln(y, rt, h32, g, b, alpha=alpha)


def _perm_w_in(w, d_model):
    sizes = ((NSA_WIDTH,) + (NSA_KV_WIDTH,) * 6 + (N_BRANCH * NSA_HEADS,)
             + (RW_WIDTH, RW_WIDTH, RW_WIDTH, RW_DECAY_LORA, RW_A_LORA, RW_GATE_LORA)
             + (POOL_WIDTH,) + (d_model,) * N_BRANCH)
    offs = np.concatenate([[0], np.cumsum(sizes)])
    seg = [w[:, int(offs[i]):int(offs[i + 1])] for i in range(len(sizes))]
    (q, kc, vc, ks, vs, kw, vw, ng, r, k, v, wl, al, gl, pool, ga, gb, gc) = seg
    zeros = lambda n: jnp.zeros((w.shape[0], n), w.dtype)
    cols = [ga, gb, gc, q, r, k, v, pool, kc, vc, ks, vs, kw, vw,
            ng, zeros(SM_WL - N_BRANCH * NSA_HEADS), wl, al, gl, zeros(SMALL_W - SM_GL - RW_GATE_LORA)]
    out = jnp.concatenate(cols, axis=1).astype(BF16)
    assert out.shape[1] == NP_COLS and d_model == 2048
    return out


def _rwkv_params(l, mu, w0, w2, a0, a2, g2, k_k, k_a, r_k, lnx_g, lnx_b, v0, v1, v2):
    w = RW_WIDTH
    rowv = lambda a: a.reshape(1, w).astype(F32)
    m = mu[l]
    o = np.cumsum((0, w, w, w, RW_DECAY_LORA, RW_A_LORA, RW_GATE_LORA))
    mu_sm = jnp.zeros((1, SMALL_W), F32)
    mu_sm = mu_sm.at[0, SM_WL:SM_WL + RW_DECAY_LORA].set(m[o[3]:o[4]])
    mu_sm = mu_sm.at[0, SM_AL:SM_AL + RW_A_LORA].set(m[o[4]:o[5]])
    mu_sm = mu_sm.at[0, SM_GL:SM_GL + RW_GATE_LORA].set(m[o[5]:o[6]])

    def pad_rows(a, off):
        return jnp.zeros((SMALL_W, w), F32).at[off:off + a.shape[0]].set(a).astype(BF16)

    p = dict(mu=m[0:3 * w].reshape(3, w), mu_sm=mu_sm, w0=rowv(w0[l]), w2=pad_rows(w2[l], SM_WL),
             a0=rowv(a0[l]), a2=pad_rows(a2[l], SM_AL), g2=pad_rows(g2[l], SM_GL),
             k_k=rowv(k_k[l]), k_a=rowv(k_a[l]), r_k=rowv(r_k[l]), lnx_g=rowv(lnx_g[l]), lnx_b=rowv(lnx_b[l]))
    if l > 0:
        p["v0"] = rowv(v0[l - 1])
        p["v1"] = jnp.pad(v1[l - 1], ((0, 0), (0, LANES - RW_V_LORA))).astype(BF16)
        p["v2"] = jnp.pad(v2[l - 1], ((0, LANES - RW_V_LORA), (0, 0))).astype(BF16)
    return p


def _stride_units(x2d, b, s):
    x = x2d.reshape(b, s // CMP_STRIDE, CMP_STRIDE, NSA_KV_HEADS, HEAD_DIM)
    return x.transpose(0, 3, 1, 2, 4).reshape(b, NSA_KV_HEADS, s // CMP_STRIDE, CMP_STRIDE * HEAD_DIM)


def _token_mixer(l, h16, proj, b, s, tabs, tabs_c, prm, vfirst):
    d_model = h16.shape[1]
    t = b * s
    proj3 = proj.reshape(b, s, NP_COLS)
    c, shi, slo = tabs
    q_t, ksa, vs_t, kwh, vw_t, g_t = _nsa_prep(proj3, c, shi, slo)
    n_units = s // CMP_STRIDE
    half = CMP_STRIDE * HEAD_DIM
    kvc = proj[:, OFF_KVC:OFF_KVC + 2 * NSA_KV_WIDTH]
    cmp_out = []
    for j in range(2):
        xu = _stride_units(kvc[:, NSA_KV_WIDTH * j:NSA_KV_WIDTH * (j + 1)], b, s)
        pe = prm["nsa_cmp_pe"][l, j].reshape(2, half)
        w1 = prm["nsa_cmp_w1"][l, j].astype(BF16)
        b1 = prm["nsa_cmp_b1"][l, j].reshape(1, CMP_HIDDEN)
        w2p = jnp.pad(prm["nsa_cmp_w2"][l, j], ((0, 0), (0, LANES - HEAD_DIM))).astype(BF16)
        cmp_out.append(_compress(xu, pe, w1, b1, w2p, *tabs_c, is_key=(j == 0)))
    kc, vc_t = cmp_out
    y_a = _nsa_attention(q_t, g_t, kc, vc_t, ksa, vs_t, kwh, vw_t).reshape(t, NSA_WIDTH)
    rp = _rwkv_params(l, prm["rwkv_mu"], prm["rwkv_w0"], prm["rwkv_w2"], prm["rwkv_a0"], prm["rwkv_a2"],
                      prm["rwkv_g2"], prm["rwkv_k_k"], prm["rwkv_k_a"], prm["rwkv_r_k"], prm["rwkv_lnx_g"],
                      prm["rwkv_lnx_b"], prm["rwkv_v0"], prm["rwkv_v1"], prm["rwkv_v2"])
    y_b, v_l = _rwkv(proj3, vfirst, rp, has_vres=(l > 0))
    y_c = _pool(proj3, prm["pool_w"][l].astype(BF16), prm["pool_scale"][l])
    merged = _merge(y_a, y_b.reshape(t, RW_WIDTH), y_c.reshape(t, POOL_WIDTH), proj,
                    prm["w_branch"][l].astype(BF16))
    return merged, v_l


def kernel(x, mem, positions, w_in, nsa_cmp_pe, nsa_cmp_w1, nsa_cmp_b1, nsa_cmp_w2, rwkv_mu, rwkv_w0, rwkv_w2, rwkv_a0, rwkv_a2, rwkv_g2, rwkv_k_k, rwkv_k_a, rwkv_r_k, rwkv_lnx_g, rwkv_lnx_b, rwkv_v0, rwkv_v1, rwkv_v2, pool_w, pool_scale, w_branch, w_out, mem_ln_g, mem_ln_b, mem_wkv, xa_wq, xa_wo, ln_g, ln_b, ffn_w_gate, ffn_w_up, ffn_w_down, moe_router, moe_w_gate, moe_w_up, moe_w_down):
    prm = dict(nsa_cmp_pe=nsa_cmp_pe, nsa_cmp_w1=nsa_cmp_w1, nsa_cmp_b1=nsa_cmp_b1, nsa_cmp_w2=nsa_cmp_w2,
               rwkv_mu=rwkv_mu, rwkv_w0=rwkv_w0, rwkv_w2=rwkv_w2, rwkv_a0=rwkv_a0, rwkv_a2=rwkv_a2,
               rwkv_g2=rwkv_g2, rwkv_k_k=rwkv_k_k, rwkv_k_a=rwkv_k_a, rwkv_r_k=rwkv_r_k,
               rwkv_lnx_g=rwkv_lnx_g, rwkv_lnx_b=rwkv_lnx_b, rwkv_v0=rwkv_v0, rwkv_v1=rwkv_v1,
               rwkv_v2=rwkv_v2, pool_w=pool_w, pool_scale=pool_scale, w_branch=w_branch)
    b, s, d = x.shape
    t = b * s
    depth = w_in.shape[0]
    alpha = (2 * depth) ** 0.25
    n_mem = mem.shape[1]

    inv_freq = ROPE_THETA ** (-jnp.arange(ROPE_HALF, dtype=F32) / ROPE_HALF)
    dd = np.arange(LANES) % HEAD_DIM
    invf_lane = jnp.where(jnp.asarray(dd < ROPE_DIM), inv_freq[jnp.asarray(dd % ROPE_HALF)], 0.0).reshape(1, LANES)
    tabs = tuple(a.reshape(b, s, LANES) for a in _rope_tables(positions.reshape(t, 1), invf_lane, ts=512))
    n_units = s // CMP_STRIDE
    pos_c = positions[:, CMP_BLOCK - 1::CMP_STRIDE]
    pos_c = jnp.concatenate([pos_c, pos_c[:, -1:]], axis=1)
    tabs_c = tuple(a.reshape(b, n_units, LANES)
                   for a in _rope_tables(pos_c.reshape(b * n_units, 1), invf_lane, ts=n_units))

    memkv = _memkv(mem.reshape(b * n_mem, d), mem_ln_g, mem_ln_b, mem_wkv.astype(BF16))
    memkv = memkv.reshape(b, n_mem, 2 * XA_WIDTH)

    h32 = x.reshape(t, d)
    h16 = h32.astype(BF16)
    vfirst = None
    for l in range(depth):
        proj = _matmul(h16, _perm_w_in(w_in[l], d), tm=512, tn=1024, out_dtype=BF16)
        merged, v_l = _token_mixer(l, h16, proj, b, s, tabs, tabs_c, prm, vfirst)
        if l == 0:
            vfirst = v_l
        h32, h16 = _matmul_res_ln(merged, w_out[l].astype(BF16), h32, ln_g[l, 0], ln_b[l, 0], alpha=alpha)
        h32, h16 = _xattn(h16, h32, xa_wq[l].astype(BF16), memkv, xa_wo[l].astype(BF16),
                          ln_g[l, 1], ln_b[l, 1], alpha=alpha, seq=s)
        if l % 2 == 0:
            e = l // 2
            h32, h16 = _ffn_ln(h16, ffn_w_gate[e:e + 1].astype(BF16), ffn_w_up[e:e + 1].astype(BF16),
                               ffn_w_down[e:e + 1].astype(BF16), h32, ln_g[l, 2], ln_b[l, 2], alpha=alpha)
        else:
            e = l // 2
            h32 = _moe(h16, h32, moe_router[e], moe_w_gate[e].astype(BF16), moe_w_up[e].astype(BF16),
                       moe_w_down[e].astype(BF16), ln_g[l, 2], ln_b[l, 2], alpha=alpha)
            h16 = h32.astype(BF16)
    return h32.reshape(b, s, d)
```

```python
import functools

import numpy as np
import jax
import jax.numpy as jnp
from jax import lax
from jax.experimental import pallas as pl
from jax.experimental.pallas import tpu as pltpu

F32 = jnp.float32
BF16 = jnp.bfloat16
I32 = jnp.int32

HEAD_DIM = 64
ROPE_DIM = HEAD_DIM // 4
ROPE_HALF = ROPE_DIM // 2
ROPE_THETA = 500000.0
NSA_HEADS = 16
NSA_KV_HEADS = 4
NSA_GROUP = NSA_HEADS // NSA_KV_HEADS
NSA_WIDTH = NSA_HEADS * HEAD_DIM
NSA_KV_WIDTH = NSA_KV_HEADS * HEAD_DIM
CMP_BLOCK = 32
CMP_STRIDE = 16
CMP_HIDDEN = 256
SLC_BLOCK = 64
SLC_TOPN = 8
WINDOW = 512
SEL_FORCE = 1e9
RW_HEADS = 16
RW_HEAD = 64
RW_WIDTH = RW_HEADS * RW_HEAD
RW_DECAY_LORA = 64
RW_A_LORA = 64
RW_V_LORA = 32
RW_GATE_LORA = 160
RW_LNX_EPS = 64e-5
POOL_WINDOWS = (2, 4, 8, 16)
POOL_GROUP = 256
POOL_WIDTH = POOL_GROUP * len(POOL_WINDOWS)
XA_HEADS = 4
XA_HEAD_DIM = 128
XA_WIDTH = XA_HEADS * XA_HEAD_DIM
N_EXPERTS = 8
TOP_K = 2
N_BRANCH = 3
LN_EPS = 1e-5
NEG_INF = -1e30
VERY_NEG = -3e38
MASK_BIG = 2.0 ** 100
LOG2_E = 1.4426950408889634

LANES = 128
MXU_DIM = 256
VMEM_LIMIT = 56 * 1024 * 1024

NP_COLS = 13312
OFF_GATES = 0
OFF_Q = 6144
OFF_R = 7168
OFF_K = 8192
OFF_V = 9216
OFF_POOL = 10240
OFF_KVC = 11264
OFF_KVS = 11776
OFF_KVW = 12288
OFF_SMALL = 12800
SMALL_W = 512
SM_WL = 128
SM_AL = 192
SM_GL = 256

NSA_TQ = 256
NSA_TK = 256
ONES_ROWS = 16
RW_CHUNK = 64
RW_TILE = 128
MOE_TM = 512
MOE_NF = 7


def _cp(sem, vmem=VMEM_LIMIT):
    return pltpu.CompilerParams(dimension_semantics=sem, vmem_limit_bytes=vmem)


def _dot(a, b):
    return jnp.dot(a, b, preferred_element_type=F32)


def _dot_nt(a, b):
    return lax.dot_general(a, b, (((1,), (1,)), ((), ())), preferred_element_type=F32)


def _layer_norm_rows(y, g, b):
    mu = jnp.mean(y, axis=-1, keepdims=True)
    d = y - mu
    var = jnp.mean(d * d, axis=-1, keepdims=True)
    return d * lax.rsqrt(var + LN_EPS) * g + b


def _mm_kernel(x_ref, w_ref, o_ref):
    o_ref[...] = _dot(x_ref[...], w_ref[...]).astype(o_ref.dtype)


def _matmul(x, w, *, tm, tn, out_dtype):
    m, k = x.shape
    n = w.shape[1]
    return pl.pallas_call(
        _mm_kernel,
        grid=(n // tn, m // tm),
        in_specs=[pl.BlockSpec((tm, k), lambda j, i: (i, 0)),
                  pl.BlockSpec((k, tn), lambda j, i: (0, j))],
        out_specs=pl.BlockSpec((tm, tn), lambda j, i: (i, j)),
        out_shape=jax.ShapeDtypeStruct((m, n), out_dtype),
        compiler_params=_cp(("parallel", "parallel")),
        name="matmul",
    )(x, w)


def _mm_ln_kernel(x_ref, w_ref, res_ref, g_ref, b_ref, o32_ref, o16_ref, *, alpha):
    y = alpha * res_ref[...] + _dot(x_ref[...], w_ref[...])
    out = _layer_norm_rows(y, g_ref[...], b_ref[...])
    o32_ref[...] = out
    o16_ref[...] = out.astype(BF16)


def _matmul_res_ln(x, w, res, g, b, *, alpha, tm=256):
    m, k = x.shape
    n = w.shape[1]
    row = lambda i: (i, 0)
    fix = lambda i: (0, 0)
    return pl.pallas_call(
        functools.partial(_mm_ln_kernel, alpha=alpha),
        grid=(m // tm,),
        in_specs=[pl.BlockSpec((tm, k), row), pl.BlockSpec((k, n), fix), pl.BlockSpec((tm, n), row),
                  pl.BlockSpec((1, n), fix), pl.BlockSpec((1, n), fix)],
        out_specs=[pl.BlockSpec((tm, n), row), pl.BlockSpec((tm, n), row)],
        out_shape=[jax.ShapeDtypeStruct((m, n), F32), jax.ShapeDtypeStruct((m, n), BF16)],
        compiler_params=_cp(("parallel",)),
        name="matmul_res_ln",
    )(x, w, res, g.reshape(1, n), b.reshape(1, n))


def _memkv_kernel(m_ref, g_ref, b_ref, w_ref, o_ref):
    xn = _layer_norm_rows(m_ref[...], g_ref[...], b_ref[...])
    o_ref[...] = _dot(xn.astype(BF16), w_ref[...]).astype(o_ref.dtype)


def _memkv(mem2d, g, b, w, *, tm=256):
    m, d = mem2d.shape
    n = w.shape[1]
    row = lambda i: (i, 0)
    fix = lambda i: (0, 0)
    return pl.pallas_call(
        _memkv_kernel,
        grid=(m // tm,),
        in_specs=[pl.BlockSpec((tm, d), row), pl.BlockSpec((1, d), fix), pl.BlockSpec((1, d), fix),
                  pl.BlockSpec((d, n), fix)],
        out_specs=pl.BlockSpec((tm, n), row),
        out_shape=jax.ShapeDtypeStruct((m, n), BF16),
        compiler_params=_cp(("parallel",)),
        name="memkv",
    )(mem2d, g.reshape(1, d), b.reshape(1, d), w)


def _rope_table_kernel(pos_ref, invf_ref, c_ref, shi_ref, slo_ref):
    ang = pos_ref[...].astype(F32) * invf_ref[...]
    d = lax.broadcasted_iota(I32, ang.shape, 1) % HEAD_DIM
    cs = jnp.cos(ang)
    sn = jnp.sin(ang)
    c_ref[...] = jnp.where(d < ROPE_DIM, cs, 1.0)
    shi_ref[...] = jnp.where((d >= ROPE_HALF) & (d < ROPE_DIM), sn, 0.0)
    slo_ref[...] = jnp.where(d < ROPE_HALF, -sn, 0.0)


def _rope_tables(pos_col, invf_lane, *, ts):
    n = pos_col.shape[0]
    row = lambda i: (i, 0)
    out = jax.ShapeDtypeStruct((n, LANES), F32)
    return pl.pallas_call(
        _rope_table_kernel,
        grid=(n // ts,),
        in_specs=[pl.BlockSpec((ts, 1), row), pl.BlockSpec((1, LANES), lambda i: (0, 0))],
        out_specs=[pl.BlockSpec((ts, LANES), row)] * 3,
        out_shape=[out, out, out],
        compiler_params=_cp(("parallel",)),
        name="rope_tables",
    )(pos_col, invf_lane)


def _rope128(x, c, shi, slo):
    return x * c + pltpu.roll(x, ROPE_HALF, 1) * shi + pltpu.roll(x, LANES - ROPE_HALF, 1) * slo


def _nsa_prep_kernel(q_ref, ks_ref, vs_ref, kw_ref, vw_ref, sm_ref, c_ref, shi_ref, slo_ref,
                     qt_ref, ksa_ref, vst_ref, kwh_ref, vwt_ref, gt_ref):
    c = c_ref[...]
    shi = shi_ref[...]
    slo = slo_ref[...]
    ts = c.shape[0]
    scale = HEAD_DIM ** -0.5 * LOG2_E
    for j in range(NSA_WIDTH // LANES):
        sl = slice(LANES * j, LANES * (j + 1))
        qt_ref[sl, :] = (_rope128(q_ref[:, sl].astype(F32), c, shi, slo) * scale).T.astype(BF16)
    t = pl.program_id(1) * ts + lax.broadcasted_iota(I32, (ts, LANES), 0)
    lane = lax.broadcasted_iota(I32, (ts, LANES), 1)
    onehot = jnp.where(lane - HEAD_DIM == t // SLC_BLOCK, 1.0, 0.0)
    lo = lane < HEAD_DIM
    for j in range(NSA_KV_WIDTH // LANES):
        sl = slice(LANES * j, LANES * (j + 1))
        ks = _rope128(ks_ref[:, sl].astype(F32), c, shi, slo)
        kw = _rope128(kw_ref[:, sl].astype(F32), c, shi, slo).astype(BF16)
        ksa_ref[2 * j] = jnp.where(lo, ks, onehot).astype(BF16)
        ksa_ref[2 * j + 1] = jnp.where(lo, pltpu.roll(ks, HEAD_DIM, 1), onehot).astype(BF16)
        kwh_ref[2 * j] = kw[:, 0:HEAD_DIM]
        kwh_ref[2 * j + 1] = kw[:, HEAD_DIM:LANES]
    for k in range(ts // LANES):
        rs = slice(LANES * k, LANES * (k + 1))
        vst_ref[k] = vs_ref[rs, :].astype(F32).T.astype(BF16)
        vwt_ref[k] = vw_ref[rs, :].astype(F32).T.astype(BF16)
    gt_ref[...] = jax.nn.sigmoid(sm_ref[:, 0:LANES].astype(F32)).T


def _nsa_prep(proj3, c, shi, slo, *, ts=512):
    b, s, _ = proj3.shape
    kvw = NSA_KV_WIDTH
    tab = pl.BlockSpec((None, ts, LANES), lambda bi, i: (bi, i, 0))
    kv = lambda off: pl.BlockSpec((None, ts, kvw), lambda bi, i: (bi, i, off // kvw))
    vt_spec = pl.BlockSpec((None, ts // LANES, kvw, LANES), lambda bi, i: (bi, i, 0, 0))
    vt_shape = jax.ShapeDtypeStruct((b, s // LANES, kvw, LANES), BF16)
    return pl.pallas_call(
        _nsa_prep_kernel,
        grid=(b, s // ts),
        in_specs=[
            pl.BlockSpec((None, ts, NSA_WIDTH), lambda bi, i: (bi, i, OFF_Q // NSA_WIDTH)),
            kv(OFF_KVS), kv(OFF_KVS + kvw), kv(OFF_KVW), kv(OFF_KVW + kvw),
            pl.BlockSpec((None, ts, SMALL_W), lambda bi, i: (bi, i, OFF_SMALL // SMALL_W)),
            tab, tab, tab],
        out_specs=[
            pl.BlockSpec((None, NSA_WIDTH, ts), lambda bi, i: (bi, 0, i)),
            pl.BlockSpec((None, NSA_KV_HEADS, ts, LANES), lambda bi, i: (bi, 0, i, 0)),
            vt_spec,
            pl.BlockSpec((None, NSA_KV_HEADS, ts, HEAD_DIM), lambda bi, i: (bi, 0, i, 0)),
            vt_spec,
            pl.BlockSpec((None, LANES, ts), lambda bi, i: (bi, 0, i))],
        out_shape=[jax.ShapeDtypeStruct((b, NSA_WIDTH, s), BF16),
                   jax.ShapeDtypeStruct((b, NSA_KV_HEADS, s, LANES), BF16),
                   vt_shape,
                   jax.ShapeDtypeStruct((b, NSA_KV_HEADS, s, HEAD_DIM), BF16),
                   vt_shape,
                   jax.ShapeDtypeStruct((b, LANES, s), F32)],
        compiler_params=_cp(("parallel", "parallel")),
        name="nsa_prep",
    )(proj3, proj3, proj3, proj3, proj3, proj3, c, shi, slo)


def _gelu_tanh(x):
    return 0.5 * x * (1.0 + jnp.tanh(0.7978845608028654 * (x + 0.044715 * x * x * x)))


def _cmp_kernel(x_ref, pe_ref, w1_ref, b1_ref, w2_ref, c_ref, shi_ref, slo_ref, o_ref, *, is_key):
    x = x_ref[...].astype(F32)
    n = x.shape[0]
    half = x.shape[1]
    xa = (x + pe_ref[0:1, :]).astype(BF16)
    xb = (x + pe_ref[1:2, :]).astype(BF16)
    ha = _dot(xa, w1_ref[0:half, :])
    hb = _dot(xb, w1_ref[half:2 * half, :])
    hid = ha + pltpu.roll(hb, n - 1, 0) + b1_ref[...]
    out = _dot(_gelu_tanh(hid).astype(BF16), w2_ref[...])
    if is_key:
        out = _rope128(out, c_ref[...], shi_ref[...], slo_ref[...])
        o_ref[...] = out[:, 0:HEAD_DIM].astype(BF16)
    else:
        o_ref[...] = out.T[0:HEAD_DIM, :].astype(BF16)


def _compress(xu, pe, w1, b1, w2p, c, shi, slo, *, is_key):
    b, nh, n, w = xu.shape
    fix2 = lambda bi, h: (0, 0)
    tab = pl.BlockSpec((None, n, LANES), lambda bi, h: (bi, 0, 0))
    oshape = (n, HEAD_DIM) if is_key else (HEAD_DIM, n)
    return pl.pallas_call(
        functools.partial(_cmp_kernel, is_key=is_key),
        grid=(b, nh),
        in_specs=[pl.BlockSpec((None, None, n, w), lambda bi, h: (bi, h, 0, 0)),
                  pl.BlockSpec((2, w), fix2), pl.BlockSpec((2 * w, CMP_HIDDEN), fix2),
                  pl.BlockSpec((1, CMP_HIDDEN), fix2), pl.BlockSpec((CMP_HIDDEN, LANES), fix2),
                  tab, tab, tab],
        out_specs=pl.BlockSpec((None, None) + oshape, lambda bi, h: (bi, h, 0, 0)),
        out_shape=jax.ShapeDtypeStruct((b, nh) + oshape, BF16),
        compiler_params=_cp(("parallel", "parallel")),
        name="nsa_compress",
    )(xu, pe, w1, b1, w2p, c, shi, slo)


def _nsa_kernel(qt_ref, gt_ref, kc_ref, vct_ref, mt_ref, ksa_ref, vst_ref, kwh_ref, vwt_ref, o_ref):
    tq = qt_ref.shape[1]
    cols = NSA_GROUP * tq
    n_cmp = kc_ref.shape[1]
    n_slc = mt_ref.shape[0]
    i = pl.program_id(1)
    t0 = i * tq
    loc = lax.broadcasted_iota(I32, (1, tq), 1)
    loc_c = jnp.concatenate([loc] * NSA_GROUP, axis=1)
    tok = t0 + loc
    tok_c = t0 + loc_c
    cmp_end = lax.broadcasted_iota(I32, (n_cmp, 1), 0) * CMP_STRIDE + (CMP_BLOCK - 1)
    cmp_mask = cmp_end <= tok_c
    blk = lax.broadcasted_iota(I32, (n_slc, 1), 0)
    blk_f = blk.astype(F32)
    cur = tok // SLC_BLOCK
    forced = (blk == 0) | (blk == cur) | (blk == cur - 1)
    allowed = blk <= cur
    tk = NSA_TK
    n_full = t0 // tk
    last_mask = (n_full * tk + lax.broadcasted_iota(I32, (tk, 1), 0)) <= tok_c
    n_win = WINDOW + tq
    ws = pl.multiple_of(jnp.maximum(t0 - WINDOW, 0), tq)
    wt0 = ws // LANES
    dwin = tok_c - (ws + lax.broadcasted_iota(I32, (n_win, 1), 0))
    win_mask = (dwin >= 0) & (dwin < WINDOW)
    cmp_to_slc_t = mt_ref[...]
    ones_rows = jnp.ones((ONES_ROWS, n_win), BF16)

    kvh = range(NSA_KV_HEADS)
    hss = [slice(HEAD_DIM * h, HEAD_DIM * (h + 1)) for h in kvh]
    qhs = [jnp.concatenate(
        [qt_ref[HEAD_DIM * (NSA_GROUP * h + g):HEAD_DIM * (NSA_GROUP * h + g + 1), :]
         for g in range(NSA_GROUP)], axis=1) for h in kvh]
    ss = [jnp.where(cmp_mask, _dot(kc_ref[h], qhs[h]), NEG_INF) for h in kvh]
    es = [jnp.where(cmp_mask, jnp.exp2(s - jnp.max(s, axis=0, keepdims=True)), 0.0) for s in ss]
    ps = [e / jnp.maximum(jnp.sum(e, axis=0, keepdims=True), 1e-30) for e in es]
    o_cs = [_dot(vct_ref[h], ps[h].astype(BF16)) for h in kvh]
    p4s = [sum(p[:, g * tq:(g + 1) * tq] for g in range(1, NSA_GROUP)) + p[:, 0:tq] for p in ps]
    p_his = [p4.astype(BF16) for p4 in p4s]
    p_los = [(p4 - p_hi.astype(F32)).astype(BF16) for p4, p_hi in zip(p4s, p_his)]
    imps = [_dot(cmp_to_slc_t, p_hi) + _dot(cmp_to_slc_t, p_lo) for p_hi, p_lo in zip(p_his, p_los)]
    imps = [jnp.where(forced, SEL_FORCE, jnp.where(allowed, imp, -SEL_FORCE)) for imp in imps]
    sels = [jnp.zeros((n_slc, tq), F32) for _ in kvh]
    for _ in range(SLC_TOPN):
        bests = [jnp.max(imp, axis=0, keepdims=True) for imp in imps]
        idxs = [jnp.min(jnp.where(imp == best, blk_f, float(n_slc)), axis=0, keepdims=True)
                for imp, best in zip(imps, bests)]
        hits = [blk_f == idx for idx in idxs]
        sels = [jnp.where(hit, 1.0, sel) for hit, sel in zip(hits, sels)]
        imps = [jnp.where(hit, VERY_NEG, imp) for hit, imp in zip(hits, imps)]
    biases = [jnp.where(sel > 0.5, 0.0, -MASK_BIG).astype(BF16) for sel in sels]
    q_augs = [jnp.concatenate([qh, jnp.concatenate([bias] * NSA_GROUP, axis=1)], axis=0)
              for qh, bias in zip(qhs, biases)]

    def sel_step(j, carry, masked):
        k0 = pl.multiple_of(j * tk, tk)
        scs = [_dot(ksa_ref[h, pl.ds(k0, tk), :], q_augs[h]) for h in kvh]
        if masked:
            scs = [jnp.where(last_mask, sc, NEG_INF) for sc in scs]
        m_news = [jnp.maximum(carry[h][0], jnp.max(scs[h], axis=0, keepdims=True)) for h in kvh]
        alphas = [jnp.exp2(carry[h][0] - m_news[h]) for h in kvh]
        pps = [jnp.exp2(scs[h] - m_news[h]).astype(BF16) for h in kvh]
        vts = [jnp.concatenate([vst_ref[j * (tk // LANES) + d, hss[h], :] for d in range(tk // LANES)], axis=1)
               for h in kvh]
        vts = [jnp.concatenate([vt, ones_rows[:, 0:tk]], axis=0) for vt in vts]
        accs = [alphas[h] * carry[h][1] + _dot(vts[h], pps[h]) for h in kvh]
        return tuple((m_news[h], accs[h]) for h in kvh)

    carry = tuple((jnp.full((1, cols), VERY_NEG, F32), jnp.zeros((HEAD_DIM + ONES_ROWS, cols), F32))
                  for _ in kvh)
    carry = lax.fori_loop(0, n_full, lambda j, cr: sel_step(j, cr, False), carry)
    carry = sel_step(n_full, carry, True)
    o_ss = [acc[0:HEAD_DIM] / acc[HEAD_DIM:HEAD_DIM + 1] for _, acc in carry]
    sws = [jnp.where(win_mask, _dot(kwh_ref[h, pl.ds(ws, n_win), :], qhs[h]), NEG_INF) for h in kvh]
    ews = [jnp.exp2(sw - jnp.max(sw, axis=0, keepdims=True)).astype(BF16) for sw in sws]
    vws = [jnp.concatenate([vwt_ref[wt0 + d, hss[h], :] for d in range(n_win // LANES)], axis=1) for h in kvh]
    vws = [jnp.concatenate([vw, ones_rows], axis=0) for vw in vws]
    a_ws = [_dot(vw, ew) for vw, ew in zip(vws, ews)]
    o_ws = [a[0:HEAD_DIM] / a[HEAD_DIM:HEAD_DIM + 1] for a in a_ws]
    out_rows = []
    for h in kvh:
        for g in range(NSA_GROUP):
            head = NSA_GROUP * h + g
            cs = slice(g * tq, (g + 1) * tq)
            out_rows.append(gt_ref[head:head + 1, :] * o_cs[h][:, cs]
                            + gt_ref[NSA_HEADS + head:NSA_HEADS + head + 1, :] * o_ss[h][:, cs]
                            + gt_ref[2 * NSA_HEADS + head:2 * NSA_HEADS + head + 1, :] * o_ws[h][:, cs])
    o_ref[...] = jnp.concatenate(out_rows, axis=0).T.astype(BF16)


def _cmp_slc_weights(n_cmp_pad, n_cmp, n_slc_pad):
    c0 = np.arange(n_cmp_pad)[:, None] * CMP_STRIDE
    s0 = np.arange(n_slc_pad)[None, :] * SLC_BLOCK
    shared = np.minimum(c0 + CMP_BLOCK, s0 + SLC_BLOCK) - np.maximum(c0, s0)
    w = np.maximum(shared, 0) / CMP_STRIDE
    w[n_cmp:] = 0.0
    return w.astype(np.float32)


def _nsa_attention(q_t, g_t, kc, vc_t, ksa, vs_t, kwh, vw_t):
    b, _, s = q_t.shape
    tq = NSA_TQ
    n_cmp_pad = kc.shape[2]
    n_slc = HEAD_DIM
    assert SLC_TOPN <= s // SLC_BLOCK <= n_slc and s >= WINDOW + tq and tq % LANES == 0
    assert s % NSA_TK == 0 and NSA_TK % tq == 0 and WINDOW % tq == 0
    m_t = jnp.asarray(_cmp_slc_weights(n_cmp_pad, s // CMP_STRIDE - 1, n_slc).T, BF16)
    per_b4 = lambda bi, i: (bi, 0, 0, 0)
    return pl.pallas_call(
        _nsa_kernel,
        grid=(b, s // tq),
        in_specs=[
            pl.BlockSpec((None, NSA_WIDTH, tq), lambda bi, i: (bi, 0, i)),
            pl.BlockSpec((None, LANES, tq), lambda bi, i: (bi, 0, i)),
            pl.BlockSpec((None, NSA_KV_HEADS, n_cmp_pad, HEAD_DIM), per_b4),
            pl.BlockSpec((None, NSA_KV_HEADS, HEAD_DIM, n_cmp_pad), per_b4),
            pl.BlockSpec((n_slc, n_cmp_pad), lambda bi, i: (0, 0)),
            pl.BlockSpec((None, NSA_KV_HEADS, s, LANES), per_b4),
            pl.BlockSpec((None, s // LANES, NSA_KV_WIDTH, LANES), per_b4),
            pl.BlockSpec((None, NSA_KV_HEADS, s, HEAD_DIM), per_b4),
            pl.BlockSpec((None, s // LANES, NSA_KV_WIDTH, LANES), per_b4)],
        out_specs=pl.BlockSpec((None, tq, NSA_WIDTH), lambda bi, i: (bi, i, 0)),
        out_shape=jax.ShapeDtypeStruct((b, s, NSA_WIDTH), BF16),
        compiler_params=_cp(("parallel", "arbitrary")),
        name="nsa_attention",
    )(q_t, g_t, kc, vc_t, m_t, ksa, vs_t, kwh, vw_t)


def _softplus(x):
    return jnp.maximum(x, 0.0) + jnp.log(1.0 + jnp.exp(-jnp.abs(x)))


def _split3(x):
    h1 = x.astype(BF16)
    r1 = x - h1.astype(F32)
    h2 = r1.astype(BF16)
    h3 = (r1 - h2.astype(F32)).astype(BF16)
    return h1, h2, h3


def _seg_sum(x, bd):
    cols = []
    for j in range(x.shape[1] // LANES):
        xs = x[:, LANES * j:LANES * (j + 1)]
        h1, h2, h3 = _split3(xs)
        cols.append(_dot(h1, bd) + _dot(h2, bd) + _dot(h3, bd))
    return jnp.concatenate(cols, axis=1)


def _rwkv_kernel(*refs, has_vres):
    it = iter(refs)
    r_ref, k_ref, v_ref, sm_ref = next(it), next(it), next(it), next(it)
    vf_ref = next(it) if has_vres else None
    mu_ref, musm_ref, w0_ref, w2_ref, a0_ref, a2_ref, g2_ref = (next(it) for _ in range(7))
    kk_ref, ka_ref, rk_ref, lng_ref, lnb_ref = (next(it) for _ in range(5))
    if has_vres:
        v0_ref, v1_ref, v2_ref = next(it), next(it), next(it)
    tri_ref, bd_ref = next(it), next(it)
    y_ref, vo_ref = next(it), next(it)
    z_ref, prev_ref, prevsm_ref = next(it), next(it), next(it)

    c = r_ref.shape[0]
    width = r_ref.shape[1]

    @pl.when(pl.program_id(1) == 0)
    def _():
        z_ref[...] = jnp.zeros(z_ref.shape, F32)
        prev_ref[...] = jnp.zeros(prev_ref.shape, F32)
        prevsm_ref[...] = jnp.zeros(prevsm_ref.shape, F32)

    row = lax.broadcasted_iota(I32, (c, 1), 0)

    def shift_mix(x, pref, idx, mu):
        prev = pref[idx, 0:1, :]
        xs = jnp.where(row == 0, prev, pltpu.roll(x, 1, 0))
        pref[idx, 0:1, :] = x[c - 1:c, :]
        return x + (xs - x) * mu

    xr = shift_mix(r_ref[...].astype(F32), prev_ref, 0, mu_ref[0:1, :])
    xk = shift_mix(k_ref[...].astype(F32), prev_ref, 1, mu_ref[1:2, :])
    xv = shift_mix(v_ref[...].astype(F32), prev_ref, 2, mu_ref[2:3, :])
    sm = shift_mix(sm_ref[...].astype(F32), prevsm_ref, 0, musm_ref[...])

    wpre = w0_ref[...] + _dot(jnp.tanh(sm).astype(BF16), w2_ref[...])
    lw = -jnp.exp(-_softplus(-wpre) - 0.5)
    a = jax.nn.sigmoid(a0_ref[...] + _dot(sm.astype(BF16), a2_ref[...]))
    gate = _dot(jax.nn.sigmoid(sm).astype(BF16), g2_ref[...])
    if has_vres:
        lo = _dot(_dot(xv.astype(BF16), v1_ref[...]).astype(BF16), v2_ref[...])
        xv = xv + (vf_ref[...].astype(F32) - xv) * jax.nn.sigmoid(v0_ref[...] + lo)
    vo_ref[...] = xv.astype(BF16)

    bd = bd_ref[...]
    kk = xk * kk_ref[...]
    kk = kk / jnp.maximum(jnp.sqrt(_seg_sum(kk * kk, bd)), 1e-12)
    kmod = xk * (1.0 + (a - 1.0) * ka_ref[...])
    ah = -kk
    bh = kk * a

    tri = tri_ref[...]
    l1, l2, l3 = _split3(lw)
    cum = _dot(tri, l1) + _dot(tri, l2) + _dot(tri, l3)
    cumx = cum - lw
    bonus_in = xr * kmod * rk_ref[...]
    cc = RW_CHUNK
    chunks = range(c // cc)
    rss = [slice(cc * n, cc * (n + 1)) for n in chunks]
    a_t, r_t, b_t, k_t, a_0, r_0, b_e, k_e, w_end = ([] for _ in range(9))
    for rs in rss:
        cu, cx = cum[rs], cumx[rs]
        cmid = cu[cc // 2 - 1:cc // 2, :]
        cend = cu[cc - 1:cc, :]
        e_inv = jnp.exp(cmid - cu)
        e_end = jnp.exp(cend - cu)
        a_t.append((ah[rs] * jnp.exp(cx - cmid)).astype(BF16))
        r_t.append((xr[rs] * jnp.exp(cu - cmid)).astype(BF16))
        b_t.append((bh[rs] * e_inv).astype(BF16))
        k_t.append((kmod[rs] * e_inv).astype(BF16))
        a_0.append((ah[rs] * jnp.exp(cx)).astype(BF16))
        r_0.append((xr[rs] * jnp.exp(cu)).astype(BF16))
        b_e.append((bh[rs] * e_end).astype(BF16))
        k_e.append((kmod[rs] * e_end).astype(BF16))
        w_end.append(jnp.exp(cend))

    ri = lax.broadcasted_iota(I32, (cc, cc), 0)
    ci = lax.broadcasted_iota(I32, (cc, cc), 1)
    strict = ci < ri
    incl = ci <= ri
    eye = jnp.where(ci == ri, 1.0, 0.0)
    n_levels = int(np.log2(cc))
    assert 2 ** n_levels == cc

    heads = range(width // RW_HEAD)
    sls = [slice(RW_HEAD * h, RW_HEAD * (h + 1)) for h in heads]
    pairs = [(n, h) for n in chunks for h in heads]
    vhs = {(n, h): xv[rss[n], sls[h]] for n, h in pairs}
    vhbs = {p: vhs[p].astype(BF16) for p in pairs}
    m1s = {(n, h): _dot_nt(jnp.concatenate([a_t[n][:, sls[h]], r_t[n][:, sls[h]]], axis=0),
                           jnp.concatenate([b_t[n][:, sls[h]], k_t[n][:, sls[h]]], axis=0))
           for n, h in pairs}
    nmat = {p: jnp.where(strict, m1s[p][0:cc, 0:cc], 0.0) for p in pairs}
    a_aks = {p: jnp.where(strict, m1s[p][0:cc, cc:2 * cc], 0.0).astype(BF16) for p in pairs}
    pqs = {p: jnp.concatenate([jnp.where(incl, m1s[p][cc:2 * cc, 0:cc], 0.0),
                               jnp.where(incl, m1s[p][cc:2 * cc, cc:2 * cc], 0.0)], axis=1).astype(BF16)
           for p in pairs}
    tinv = {p: eye + nmat[p] for p in pairs}
    npows = {p: nmat[p].astype(BF16) for p in pairs}
    for _ in range(1, n_levels):
        npows = {p: _dot(npows[p], npows[p]).astype(BF16) for p in pairs}
        tinv = {p: tinv[p] + _dot(npows[p], tinv[p].astype(BF16)) for p in pairs}
    tinvb = {p: tinv[p].astype(BF16) for p in pairs}
    bkes = {(n, h): jnp.concatenate([b_e[n][:, sls[h]], k_e[n][:, sls[h]]], axis=0) for n, h in pairs}

    zts = [z_ref[h] for h in heads]
    ys = {}
    for n in chunks:
        ztbs = [zt.astype(BF16) for zt in zts]
        xs = [_dot_nt(a_0[n][:, sls[h]], ztbs[h]) + _dot(a_aks[n, h], vhbs[n, h]) for h in heads]
        y0s = [_dot_nt(r_0[n][:, sls[h]], ztbs[h]) for h in heads]
        us = [_dot(tinvb[n, h], xs[h].astype(BF16)) for h in heads]
        uvs = [jnp.concatenate([us[h], vhs[n, h]], axis=0) for h in heads]
        for h in heads:
            ys[n, h] = y0s[h] + _dot(pqs[n, h], uvs[h].astype(BF16))
        zts = [zts[h] * w_end[n][:, sls[h]] + _dot(uvs[h].T.astype(BF16), bkes[n, h]) for h in heads]
    for h in heads:
        z_ref[h] = zts[h]
    for n, h in pairs:
        rs, sl = rss[n], sls[h]
        y = ys[n, h]
        ym = jnp.mean(y, axis=1, keepdims=True)
        yd = y - ym
        yv = jnp.mean(yd * yd, axis=1, keepdims=True)
        yn = yd * lax.rsqrt(yv + RW_LNX_EPS) * lng_ref[:, sl] + lnb_ref[:, sl]
        bonus = jnp.sum(bonus_in[rs, sl], axis=1, keepdims=True) * vhs[n, h]
        y_ref[rs, sl] = ((yn + bonus) * gate[rs, sl]).astype(BF16)


def _rwkv(proj3, vfirst, p, *, has_vres):
    b, s, _ = proj3.shape
    c = RW_TILE
    w = RW_WIDTH
    cb = lambda off: off // w
    fix = lambda bi, i: (0, 0)
    blk = lambda off: pl.BlockSpec((None, c, w), lambda bi, i: (bi, i, cb(off)))
    row_w = pl.BlockSpec((1, w), fix)
    in_specs = [blk(OFF_R), blk(OFF_K), blk(OFF_V),
                pl.BlockSpec((None, c, SMALL_W), lambda bi, i: (bi, i, OFF_SMALL // SMALL_W))]
    args = [proj3, proj3, proj3, proj3]
    if has_vres:
        in_specs.append(pl.BlockSpec((None, c, w), lambda bi, i: (bi, i, 0)))
        args.append(vfirst)
    in_specs += [pl.BlockSpec((3, w), fix), pl.BlockSpec((1, SMALL_W), fix), row_w,
                 pl.BlockSpec((SMALL_W, w), fix), row_w, pl.BlockSpec((SMALL_W, w), fix),
                 pl.BlockSpec((SMALL_W, w), fix), row_w, row_w, row_w, row_w, row_w]
    args += [p["mu"], p["mu_sm"], p["w0"], p["w2"], p["a0"], p["a2"], p["g2"],
             p["k_k"], p["k_a"], p["r_k"], p["lnx_g"], p["lnx_b"]]
    if has_vres:
        in_specs += [row_w, pl.BlockSpec((w, LANES), fix), pl.BlockSpec((LANES, w), fix)]
        args += [p["v0"], p["v1"], p["v2"]]
    assert c % RW_CHUNK == 0 and s % c == 0
    pos = np.arange(c)
    tri = jnp.asarray(((pos[None, :] <= pos[:, None])
                       & (pos[None, :] // RW_CHUNK == pos[:, None] // RW_CHUNK)).astype(np.float32), BF16)
    lane_head = np.arange(LANES) // RW_HEAD
    bd = jnp.asarray((lane_head[:, None] == lane_head[None, :]).astype(np.float32), BF16)
    in_specs += [pl.BlockSpec((c, c), fix), pl.BlockSpec((LANES, LANES), fix)]
    args += [tri, bd]
    out_blk = pl.BlockSpec((None, c, w), lambda bi, i: (bi, i, 0))
    return pl.pallas_call(
        functools.partial(_rwkv_kernel, has_vres=has_vres),
        grid=(b, s // c),
        in_specs=in_specs,
        out_specs=[out_blk, out_blk],
        out_shape=[jax.ShapeDtypeStruct((b, s, w), BF16), jax.ShapeDtypeStruct((b, s, w), BF16)],
        scratch_shapes=[pltpu.VMEM((w // RW_HEAD, RW_HEAD, RW_HEAD), F32),
                        pltpu.VMEM((3, 8, w), F32), pltpu.VMEM((1, 8, SMALL_W), F32)],
        compiler_params=_cp(("parallel", "arbitrary")),
        name="rwkv7",
    )(*args)


POOL_HALO = 16


def _pool_kernel(cur_ref, halo_ref, w_ref, sc_ref, o_ref):
    i = pl.program_id(1)
    tp = cur_ref.shape[0]
    cur = cur_ref[...].astype(F32)
    halo = jnp.where(i > 0, halo_ref[...].astype(F32), 0.0)
    ext = jnp.concatenate([halo, cur], axis=0)
    sums = {1: ext}
    win = 1
    while win < max(POOL_WINDOWS):
        sums[2 * win] = sums[win] + pltpu.roll(sums[win], win, 0)
        win *= 2
    count = (i * tp + 1 + lax.broadcasted_iota(I32, (tp, 1), 0)).astype(F32)
    for gi, win in enumerate(POOL_WINDOWS):
        sl = slice(POOL_GROUP * gi, POOL_GROUP * (gi + 1))
        mean = sums[win][POOL_HALO:, sl] / jnp.minimum(count, float(win))
        y = _dot((mean - cur[:, sl]).astype(BF16), w_ref[gi])
        o_ref[:, sl] = (y * sc_ref[:, sl]).astype(BF16)


def _pool(proj3, pool_w, pool_scale, *, tp=256):
    b, s, _ = proj3.shape
    w = POOL_WIDTH
    cb = OFF_POOL // w
    per = tp // POOL_HALO
    return pl.pallas_call(
        _pool_kernel,
        grid=(b, s // tp),
        in_specs=[pl.BlockSpec((None, tp, w), lambda bi, i: (bi, i, cb)),
                  pl.BlockSpec((None, POOL_HALO, w), lambda bi, i: (bi, jnp.maximum(i * per - 1, 0), cb)),
                  pl.BlockSpec((len(POOL_WINDOWS), POOL_GROUP, POOL_GROUP), lambda bi, i: (0, 0, 0)),
                  pl.BlockSpec((1, w), lambda bi, i: (0, 0))],
        out_specs=pl.BlockSpec((None, tp, w), lambda bi, i: (bi, i, 0)),
        out_shape=jax.ShapeDtypeStruct((b, s, w), BF16),
        compiler_params=_cp(("parallel", "parallel")),
        name="pool",
    )(proj3, proj3, pool_w, pool_scale.reshape(1, w))


def _merge_kernel(ya_ref, yb_ref, yc_ref, ga_ref, gb_ref, gc_ref, wb_ref, o_ref):
    acc = jax.nn.sigmoid(ga_ref[...].astype(F32)) * _dot(ya_ref[...], wb_ref[0])
    acc = acc + jax.nn.sigmoid(gb_ref[...].astype(F32)) * _dot(yb_ref[...], wb_ref[1])
    acc = acc + jax.nn.sigmoid(gc_ref[...].astype(F32)) * _dot(yc_ref[...], wb_ref[2])
    o_ref[...] = acc.astype(BF16)


def _merge(ya, yb, yc, proj, wb, *, tm=256):
    t, w = ya.shape
    d = wb.shape[2]
    row = lambda i: (i, 0)
    gate = lambda k: pl.BlockSpec((tm, d), lambda i: (i, k))
    return pl.pallas_call(
        _merge_kernel,
        grid=(t // tm,),
        in_specs=[pl.BlockSpec((tm, w), row)] * 3 + [gate(0), gate(1), gate(2),
                                                      pl.BlockSpec((N_BRANCH, w, d), lambda i: (0, 0, 0))],
        out_specs=pl.BlockSpec((tm, d), row),
        out_shape=jax.ShapeDtypeStruct((t, d), BF16),
        compiler_params=_cp(("parallel",)),
        name="merge",
    )(ya, yb, yc, proj, proj, proj, wb)


def _xattn_kernel(h16_ref, h32_ref, wq_ref, mkv_ref, wo_ref, g_ref, b_ref, o32_ref, o16_ref, *, alpha):
    q = _dot(h16_ref[...], wq_ref[...])
    scale = XA_HEAD_DIM ** -0.5
    outs = []
    for hd in range(XA_HEADS):
        sl = slice(XA_HEAD_DIM * hd, XA_HEAD_DIM * (hd + 1))
        slv = slice(XA_WIDTH + XA_HEAD_DIM * hd, XA_WIDTH + XA_HEAD_DIM * (hd + 1))
        s = _dot_nt(q[:, sl].astype(BF16), mkv_ref[:, sl]) * scale
        e = jnp.exp(s - jnp.max(s, axis=1, keepdims=True))
        p = e / jnp.sum(e, axis=1, keepdims=True)
        outs.append(_dot(p.astype(BF16), mkv_ref[:, slv]))
    o = jnp.concatenate(outs, axis=1).astype(BF16)
    y = alpha * h32_ref[...] + _dot(o, wo_ref[...])
    out = _layer_norm_rows(y, g_ref[...], b_ref[...])
    o32_ref[...] = out
    o16_ref[...] = out.astype(BF16)


def _xattn(h16, h32, wq, memkv, wo, g, b, *, alpha, seq, tm=256):
    t, d = h16.shape
    per_b = seq // tm
    nm = memkv.shape[1]
    row = lambda i: (i, 0)
    fix = lambda i: (0, 0)
    return pl.pallas_call(
        functools.partial(_xattn_kernel, alpha=alpha),
        grid=(t // tm,),
        in_specs=[pl.BlockSpec((tm, d), row), pl.BlockSpec((tm, d), row),
                  pl.BlockSpec((d, XA_WIDTH), fix),
                  pl.BlockSpec((None, nm, 2 * XA_WIDTH), lambda i: (i // per_b, 0, 0)),
                  pl.BlockSpec((XA_WIDTH, d), fix), pl.BlockSpec((1, d), fix), pl.BlockSpec((1, d), fix)],
        out_specs=[pl.BlockSpec((tm, d), row), pl.BlockSpec((tm, d), row)],
        out_shape=[jax.ShapeDtypeStruct((t, d), F32), jax.ShapeDtypeStruct((t, d), BF16)],
        compiler_params=_cp(("parallel",)),
        name="xattn",
    )(h16, h32, wq, memkv, wo, g.reshape(1, d), b.reshape(1, d))


def _swiglu_step(x, wg_ref, wu_ref, wd_ref, acc_ref):
    gt = _dot(x, wg_ref[0])
    up = _dot(x, wu_ref[0])
    act = (gt * jax.nn.sigmoid(gt) * up).astype(BF16)
    acc_ref[...] += _dot(act, wd_ref[0])


def _ffn_ln_kernel(x_ref, wg_ref, wu_ref, wd_ref, res_ref, g_ref, b_ref, o32_ref, o16_ref, acc_ref, *, alpha):
    f = pl.program_id(1)

    @pl.when(f == 0)
    def _():
        acc_ref[...] = jnp.zeros(acc_ref.shape, F32)

    _swiglu_step(x_ref[...], wg_ref, wu_ref, wd_ref, acc_ref)

    @pl.when(f == pl.num_programs(1) - 1)
    def _():
        out = _layer_norm_rows(alpha * res_ref[...] + acc_ref[...], g_ref[...], b_ref[...])
        o32_ref[...] = out
        o16_ref[...] = out.astype(BF16)


def _ffn_ln(x, wg, wu, wd, res, g, b, *, alpha, tm=512, tf=512):
    n, d = x.shape
    ff = wg.shape[2]
    row = lambda i, f: (i, 0)
    fix = lambda i, f: (0, 0)
    return pl.pallas_call(
        functools.partial(_ffn_ln_kernel, alpha=alpha),
        grid=(n // tm, ff // tf),
        in_specs=[pl.BlockSpec((tm, d), row),
                  pl.BlockSpec((1, d, tf), lambda i, f: (0, 0, f)),
                  pl.BlockSpec((1, d, tf), lambda i, f: (0, 0, f)),
                  pl.BlockSpec((1, tf, d), lambda i, f: (0, f, 0)),
                  pl.BlockSpec((tm, d), row), pl.BlockSpec((1, d), fix), pl.BlockSpec((1, d), fix)],
        out_specs=[pl.BlockSpec((tm, d), row), pl.BlockSpec((tm, d), row)],
        out_shape=[jax.ShapeDtypeStruct((n, d), F32), jax.ShapeDtypeStruct((n, d), BF16)],
        scratch_shapes=[pltpu.VMEM((tm, d), F32)],
        compiler_params=_cp(("parallel", "arbitrary")),
        name="swiglu_ln",
    )(x, wg, wu, wd, res, g.reshape(1, d), b.reshape(1, d))


def _moe_ffn_kernel(be_ref, idx_ref, idx_next_ref, dst_prev_ref, dst_ref, src_ref, wg_ref, wu_ref, wd_ref,
                    y_ref, xbuf, x16, acc_ref, ybuf, sem_in, sem_out):
    i = pl.program_id(0)
    f = pl.program_id(1)
    n_blk = pl.num_programs(0)
    n_f = pl.num_programs(1)
    tm = x16.shape[0]
    per_step = tm // MOE_NF
    slot = i % 2

    def gather_copy(ids, r, s):
        return pltpu.make_async_copy(src_ref.at[pl.ds(ids[0, r], 1)], xbuf.at[s, pl.ds(r, 1)], sem_in.at[s])

    def scatter_copy(ids, r, s):
        return pltpu.make_async_copy(ybuf.at[s, pl.ds(r, 1)], y_ref.at[pl.ds(ids[0, r], 1)], sem_out.at[s])

    def for_rows(fn):
        def body(r, carry):
            fn(r)
            return carry
        lax.fori_loop(0, tm, body, 0)

    def wait_gather(s):
        for _ in range(tm):
            gather_copy(idx_ref, 0, s).wait()

    def wait_scatter(s):
        for _ in range(tm):
            scatter_copy(dst_ref, 0, s).wait()

    @pl.when((i == 0) & (f == 0))
    def _():
        for_rows(lambda r: gather_copy(idx_ref, r, 0).start())
        ybuf[1] = jnp.zeros(ybuf.shape[1:], F32)

    def start_both(r):
        gather_copy(idx_next_ref, r, 1 - slot).start()
        scatter_copy(dst_prev_ref, r, 1 - slot).start()

    @pl.when(f == 0)
    def _():
        wait_gather(slot)
        x16[...] = xbuf[slot].astype(BF16)
        acc_ref[...] = jnp.zeros(acc_ref.shape, F32)
        for r in range(MOE_NF * per_step, tm):
            start_both(r)

    for k in range(per_step):
        start_both(f * per_step + k)

    _swiglu_step(x16[...], wg_ref, wu_ref, wd_ref, acc_ref)

    @pl.when(f == n_f - 1)
    def _():
        @pl.when(i >= 1)
        def _():
            wait_scatter(slot)
        ybuf[slot] = acc_ref[...]

    @pl.when((i == n_blk - 1) & (f == n_f - 1))
    def _():
        for_rows(lambda r: scatter_copy(dst_ref, r, slot).start())
        wait_scatter(slot)
        wait_scatter(1 - slot)
        wait_gather(1 - slot)


def _moe_ffn(block_expert, tok_pad, dst_rows, src, wg, wu, wd, *, tm):
    n = tok_pad.shape[0]
    d = src.shape[1]
    ff = wg.shape[2]
    tf = ff // MOE_NF
    assert ff % MOE_NF == 0 and tf % MXU_DIM == 0
    n_blk = n // tm
    assert n_blk >= 2
    ids = tok_pad.reshape(n_blk, 1, tm)
    dst = jnp.concatenate([n + jnp.arange(tm, dtype=I32), dst_rows]).reshape(n_blk + 1, 1, tm)
    smem = lambda fn: pl.BlockSpec((None, 1, tm), fn, memory_space=pltpu.SMEM)
    return pl.pallas_call(
        _moe_ffn_kernel,
        grid_spec=pltpu.PrefetchScalarGridSpec(
            num_scalar_prefetch=1, grid=(n_blk, MOE_NF),
            in_specs=[smem(lambda i, f, be: (i, 0, 0)),
                      smem(lambda i, f, be: (jnp.minimum(i + 1, n_blk - 1), 0, 0)),
                      smem(lambda i, f, be: (i, 0, 0)),
                      smem(lambda i, f, be: (i + 1, 0, 0)),
                      pl.BlockSpec(memory_space=pl.ANY),
                      pl.BlockSpec((1, d, tf), lambda i, f, be: (be[i], 0, f)),
                      pl.BlockSpec((1, d, tf), lambda i, f, be: (be[i], 0, f)),
                      pl.BlockSpec((1, tf, d), lambda i, f, be: (be[i], f, 0))],
            out_specs=pl.BlockSpec(memory_space=pl.ANY),
            scratch_shapes=[pltpu.VMEM((2, tm, d), F32), pltpu.VMEM((tm, d), BF16), pltpu.VMEM((tm, d), F32),
                            pltpu.VMEM((2, tm, d), F32),
                            pltpu.SemaphoreType.DMA((2,)), pltpu.SemaphoreType.DMA((2,))]),
        out_shape=jax.ShapeDtypeStruct((n + tm, d), F32),
        compiler_params=_cp(("arbitrary", "arbitrary")),
        name="moe_swiglu",
    )(block_expert, ids, ids, dst, dst, src, wg, wu, wd)


def _router_kernel(x_ref, w_ref, o_ref):
    logits = _dot(x_ref[...], w_ref[...])
    lane = lax.broadcasted_iota(I32, logits.shape, 1)
    lane_f = lane.astype(F32)
    l1 = jnp.where(lane < N_EXPERTS, logits, VERY_NEG)
    m1 = jnp.max(l1, axis=1, keepdims=True)
    i1 = jnp.min(jnp.where(l1 == m1, lane_f, float(LANES)), axis=1, keepdims=True)
    l2 = jnp.where(lane_f == i1, VERY_NEG, l1)
    m2 = jnp.max(l2, axis=1, keepdims=True)
    i2 = jnp.min(jnp.where(l2 == m2, lane_f, float(LANES)), axis=1, keepdims=True)
    e = jnp.exp(m2 - m1)
    g1 = 1.0 / (1.0 + e)
    g2 = e / (1.0 + e)
    o_ref[...] = jnp.where(lane == 0, i1, jnp.where(lane == 1, i2, jnp.where(lane == 2, g1,
                           jnp.where(lane == 3, g2, 0.0))))


def _router(h16, wr, *, tm=512):
    t, d = h16.shape
    return pl.pallas_call(
        _router_kernel,
        grid=(t // tm,),
        in_specs=[pl.BlockSpec((tm, d), lambda i: (i, 0)), pl.BlockSpec((d, LANES), lambda i: (0, 0))],
        out_specs=pl.BlockSpec((tm, LANES), lambda i: (i, 0)),
        out_shape=jax.ShapeDtypeStruct((t, LANES), F32),
        compiler_params=_cp(("parallel",)),
        name="router",
    )(h16, wr)


def _combine_kernel(y0_ref, y1_ref, rt_ref, res_ref, g_ref, b_ref, o_ref, *, alpha):
    f = y0_ref[...] * rt_ref[:, 2:3] + y1_ref[...] * rt_ref[:, 3:4]
    o_ref[...] = _layer_norm_rows(alpha * res_ref[...] + f, g_ref[...], b_ref[...])


def _combine_ln(y, rt, res, g, b, *, alpha, tm=256):
    t, d = res.shape
    per_k = t // tm
    row = lambda i: (i, 0)
    fix = lambda i: (0, 0)
    return pl.pallas_call(
        functools.partial(_combine_kernel, alpha=alpha),
        grid=(t // tm,),
        in_specs=[pl.BlockSpec((tm, d), row), pl.BlockSpec((tm, d), lambda i: (per_k + i, 0)),
                  pl.BlockSpec((tm, LANES), row), pl.BlockSpec((tm, d), row),
                  pl.BlockSpec((1, d), fix), pl.BlockSpec((1, d), fix)],
        out_specs=pl.BlockSpec((tm, d), row),
        out_shape=jax.ShapeDtypeStruct((t, d), F32),
        compiler_params=_cp(("parallel",)),
        name="moe_combine_ln",
    )(y, y, rt, res, g.reshape(1, d), b.reshape(1, d))


def _moe(h16, h32, router, w_gate, w_up, w_down, g, b, *, alpha):
    t, d = h32.shape
    tm = MOE_TM
    wr = jnp.pad(router, ((0, 0), (0, LANES - N_EXPERTS))).astype(BF16)
    rt = _router(h16, wr)
    expert = rt[:, 0:TOP_K].astype(I32).reshape(-1)
    n_assign = t * TOP_K
    onehot = (expert[:, None] == jnp.arange(N_EXPERTS, dtype=I32)[None, :]).astype(I32)
    csum = jnp.cumsum(onehot, axis=0)
    rank = jnp.sum(csum * onehot, axis=1) - 1
    counts = csum[-1]
    padded = (counts + tm - 1) // tm * tm
    pad_end = jnp.cumsum(padded)
    pad_start = pad_end - padded
    dest = (pad_start[expert] + rank).astype(I32)
    n_pad = n_assign + N_EXPERTS * tm
    n_blocks = n_pad // tm
    block_expert = jnp.minimum(
        jnp.searchsorted(pad_end, jnp.arange(n_blocks, dtype=I32) * tm, side="right"),
        N_EXPERTS - 1).astype(I32)
    assign = jnp.arange(n_assign, dtype=I32)
    token = assign // TOP_K
    info = jnp.stack([token, (assign % TOP_K) * t + token], axis=1)
    info = jnp.full((n_pad, 2), -1, I32).at[dest].set(info)
    is_pad = info[:, 0] < 0
    tok_pad = jnp.where(is_pad, 0, info[:, 0])
    dst_rows = jnp.where(is_pad, n_assign + jnp.cumsum(is_pad.astype(I32)) - 1, info[:, 1]).astype(I32)
    y = _moe_ffn(block_expert, tok_pad, dst_rows, h32, w_gate, w_up, w_down, tm=tm)
    return _combine_ln(y, rt, h32, g, b, alpha=alpha)


def _perm_w_in(w, d_model):
    sizes = ((NSA_WIDTH,) + (NSA_KV_WIDTH,) * 6 + (N_BRANCH * NSA_HEADS,)
             + (RW_WIDTH, RW_WIDTH, RW_WIDTH, RW_DECAY_LORA, RW_A_LORA, RW_GATE_LORA)
             + (POOL_WIDTH,) + (d_model,) * N_BRANCH)
    offs = np.concatenate([[0], np.cumsum(sizes)])
    seg = [w[:, int(offs[i]):int(offs[i + 1])] for i in range(len(sizes))]
    (q, kc, vc, ks, vs, kw, vw, ng, r, k, v, wl, al, gl, pool, ga, gb, gc) = seg
    zeros = lambda n: jnp.zeros((w.shape[0], n), w.dtype)
    cols = [ga, gb, gc, q, r, k, v, pool, kc, vc, ks, vs, kw, vw,
            ng, zeros(SM_WL - N_BRANCH * NSA_HEADS), wl, al, gl, zeros(SMALL_W - SM_GL - RW_GATE_LORA)]
    out = jnp.concatenate(cols, axis=1).astype(BF16)
    assert out.shape[1] == NP_COLS and d_model == 2048
    return out


def _rwkv_params(l, mu, w0, w2, a0, a2, g2, k_k, k_a, r_k, lnx_g, lnx_b, v0, v1, v2):
    w = RW_WIDTH
    rowv = lambda a: a.reshape(1, w).astype(F32)
    m = mu[l]
    o = np.cumsum((0, w, w, w, RW_DECAY_LORA, RW_A_LORA, RW_GATE_LORA))
    mu_sm = jnp.zeros((1, SMALL_W), F32)
    mu_sm = mu_sm.at[0, SM_WL:SM_WL + RW_DECAY_LORA].set(m[o[3]:o[4]])
    mu_sm = mu_sm.at[0, SM_AL:SM_AL + RW_A_LORA].set(m[o[4]:o[5]])
    mu_sm = mu_sm.at[0, SM_GL:SM_GL + RW_GATE_LORA].set(m[o[5]:o[6]])

    def pad_rows(a, off):
        return jnp.zeros((SMALL_W, w), F32).at[off:off + a.shape[0]].set(a).astype(BF16)

    p = dict(mu=m[0:3 * w].reshape(3, w), mu_sm=mu_sm, w0=rowv(w0[l]), w2=pad_rows(w2[l], SM_WL),
             a0=rowv(a0[l]), a2=pad_rows(a2[l], SM_AL), g2=pad_rows(g2[l], SM_GL),
             k_k=rowv(k_k[l]), k_a=rowv(k_a[l]), r_k=rowv(r_k[l]), lnx_g=rowv(lnx_g[l]), lnx_b=rowv(lnx_b[l]))
    if l > 0:
        p["v0"] = rowv(v0[l - 1])
        p["v1"] = jnp.pad(v1[l - 1], ((0, 0), (0, LANES - RW_V_LORA))).astype(BF16)
        p["v2"] = jnp.pad(v2[l - 1], ((0, LANES - RW_V_LORA), (0, 0))).astype(BF16)
    return p


def _stride_units(x2d, b, s):
    x = x2d.reshape(b, s // CMP_STRIDE, CMP_STRIDE, NSA_KV_HEADS, HEAD_DIM)
    return x.transpose(0, 3, 1, 2, 4).reshape(b, NSA_KV_HEADS, s // CMP_STRIDE, CMP_STRIDE * HEAD_DIM)


def _token_mixer(l, h16, proj, b, s, tabs, tabs_c, prm, vfirst):
    d_model = h16.shape[1]
    t = b * s
    proj3 = proj.reshape(b, s, NP_COLS)
    c, shi, slo = tabs
    q_t, ksa, vs_t, kwh, vw_t, g_t = _nsa_prep(proj3, c, shi, slo)
    n_units = s // CMP_STRIDE
    half = CMP_STRIDE * HEAD_DIM
    kvc = proj[:, OFF_KVC:OFF_KVC + 2 * NSA_KV_WIDTH]
    cmp_out = []
    for j in range(2):
        xu = _stride_units(kvc[:, NSA_KV_WIDTH * j:NSA_KV_WIDTH * (j + 1)], b, s)
        pe = prm["nsa_cmp_pe"][l, j].reshape(2, half)
        w1 = prm["nsa_cmp_w1"][l, j].astype(BF16)
        b1 = prm["nsa_cmp_b1"][l, j].reshape(1, CMP_HIDDEN)
        w2p = jnp.pad(prm["nsa_cmp_w2"][l, j], ((0, 0), (0, LANES - HEAD_DIM))).astype(BF16)
        cmp_out.append(_compress(xu, pe, w1, b1, w2p, *tabs_c, is_key=(j == 0)))
    kc, vc_t = cmp_out
    y_a = _nsa_attention(q_t, g_t, kc, vc_t, ksa, vs_t, kwh, vw_t).reshape(t, NSA_WIDTH)
    rp = _rwkv_params(l, prm["rwkv_mu"], prm["rwkv_w0"], prm["rwkv_w2"], prm["rwkv_a0"], prm["rwkv_a2"],
                      prm["rwkv_g2"], prm["rwkv_k_k"], prm["rwkv_k_a"], prm["rwkv_r_k"], prm["rwkv_lnx_g"],
                      prm["rwkv_lnx_b"], prm["rwkv_v0"], prm["rwkv_v1"], prm["rwkv_v2"])
    y_b, v_l = _rwkv(proj3, vfirst, rp, has_vres=(l > 0))
    y_c = _pool(proj3, prm["pool_w"][l].astype(BF16), prm["pool_scale"][l])
    merged = _merge(y_a, y_b.reshape(t, RW_WIDTH), y_c.reshape(t, POOL_WIDTH), proj,
                    prm["w_branch"][l].astype(BF16))
    return merged, v_l


def kernel(x, mem, positions, w_in, nsa_cmp_pe, nsa_cmp_w1, nsa_cmp_b1, nsa_cmp_w2, rwkv_mu, rwkv_w0, rwkv_w2, rwkv_a0, rwkv_a2, rwkv_g2, rwkv_k_k, rwkv_k_a, rwkv_r_k, rwkv_lnx_g, rwkv_lnx_b, rwkv_v0, rwkv_v1, rwkv_v2, pool_w, pool_scale, w_branch, w_out, mem_ln_g, mem_ln_b, mem_wkv, xa_wq, xa_wo, ln_g, ln_b, ffn_w_gate, ffn_w_up, ffn_w_down, moe_router, moe_w_gate, moe_w_up, moe_w_down):
    prm = dict(nsa_cmp_pe=nsa_cmp_pe, nsa_cmp_w1=nsa_cmp_w1, nsa_cmp_b1=nsa_cmp_b1, nsa_cmp_w2=nsa_cmp_w2,
               rwkv_mu=rwkv_mu, rwkv_w0=rwkv_w0, rwkv_w2=rwkv_w2, rwkv_a0=rwkv_a0, rwkv_a2=rwkv_a2,
               rwkv_g2=rwkv_g2, rwkv_k_k=rwkv_k_k, rwkv_k_a=rwkv_k_a, rwkv_r_k=rwkv_r_k,
               rwkv_lnx_g=rwkv_lnx_g, rwkv_lnx_b=rwkv_lnx_b, rwkv_v0=rwkv_v0, rwkv_v1=rwkv_v1,
               rwkv_v2=rwkv_v2, pool_w=pool_w, pool_scale=pool_scale, w_branch=w_branch)
    b, s, d = x.shape
    t = b * s
    depth = w_in.shape[0]
    alpha = (2 * depth) ** 0.25
    n_mem = mem.shape[1]

    inv_freq = ROPE_THETA ** (-jnp.arange(ROPE_HALF, dtype=F32) / ROPE_HALF)
    dd = np.arange(LANES) % HEAD_DIM
    invf_lane = jnp.where(jnp.asarray(dd < ROPE_DIM), inv_freq[jnp.asarray(dd % ROPE_HALF)], 0.0).reshape(1, LANES)
    tabs = tuple(a.reshape(b, s, LANES) for a in _rope_tables(positions.reshape(t, 1), invf_lane, ts=512))
    n_units = s // CMP_STRIDE
    pos_c = positions[:, CMP_BLOCK - 1::CMP_STRIDE]
    pos_c = jnp.concatenate([pos_c, pos_c[:, -1:]], axis=1)
    tabs_c = tuple(a.reshape(b, n_units, LANES)
                   for a in _rope_tables(pos_c.reshape(b * n_units, 1), invf_lane, ts=n_units))

    memkv = _memkv(mem.reshape(b * n_mem, d), mem_ln_g, mem_ln_b, mem_wkv.astype(BF16))
    memkv = memkv.reshape(b, n_mem, 2 * XA_WIDTH)

    h32 = x.reshape(t, d)
    h16 = h32.astype(BF16)
    vfirst = None
    for l in range(depth):
        proj = _matmul(h16, _perm_w_in(w_in[l], d), tm=512, tn=1024, out_dtype=BF16)
        merged, v_l = _token_mixer(l, h16, proj, b, s, tabs, tabs_c, prm, vfirst)
        if l == 0:
            vfirst = v_l
        h32, h16 = _matmul_res_ln(merged, w_out[l].astype(BF16), h32, ln_g[l, 0], ln_b[l, 0], alpha=alpha)
        h32, h16 = _xattn(h16, h32, xa_wq[l].astype(BF16), memkv, xa_wo[l].astype(BF16),
                          ln_g[l, 1], ln_b[l, 1], alpha=alpha, seq=s)
        if l % 2 == 0:
            e = l // 2
            h32, h16 = _ffn_ln(h16, ffn_w_gate[e:e + 1].astype(BF16), ffn_w_up[e:e + 1].astype(BF16),
                               ffn_w_down[e:e + 1].astype(BF16), h32, ln_g[l, 2], ln_b[l, 2], alpha=alpha)
        else:
            e = l // 2
            h32 = _moe(h16, h32, moe_router[e], moe_w_gate[e].astype(BF16), moe_w_up[e].astype(BF16),
                       moe_w_down[e].astype(BF16), ln_g[l, 2], ln_b[l, 2], alpha=alpha)
            h16 = h32.astype(BF16)
    return h32.reshape(b, s, d)
```

```python
import functools

import numpy as np
import jax
import jax.numpy as jnp
from jax import lax
from jax.experimental import pallas as pl
from jax.experimental.pallas import tpu as pltpu

F32 = jnp.float32
BF16 = jnp.bfloat16
I32 = jnp.int32

HEAD_DIM = 64
ROPE_DIM = HEAD_DIM // 4
ROPE_HALF = ROPE_DIM // 2
ROPE_THETA = 500000.0
NSA_HEADS = 16
NSA_KV_HEADS = 4
NSA_GROUP = NSA_HEADS // NSA_KV_HEADS
NSA_WIDTH = NSA_HEADS * HEAD_DIM
NSA_KV_WIDTH = NSA_KV_HEADS * HEAD_DIM
CMP_BLOCK = 32
CMP_STRIDE = 16
CMP_HIDDEN = 256
SLC_BLOCK = 64
SLC_TOPN = 8
WINDOW = 512
SEL_FORCE = 1e9
RW_HEADS = 16
RW_HEAD = 64
RW_WIDTH = RW_HEADS * RW_HEAD
RW_DECAY_LORA = 64
RW_A_LORA = 64
RW_V_LORA = 32
RW_GATE_LORA = 160
RW_LNX_EPS = 64e-5
POOL_WINDOWS = (2, 4, 8, 16)
POOL_GROUP = 256
POOL_WIDTH = POOL_GROUP * len(POOL_WINDOWS)
XA_HEADS = 4
XA_HEAD_DIM = 128
XA_WIDTH = XA_HEADS * XA_HEAD_DIM
N_EXPERTS = 8
TOP_K = 2
N_BRANCH = 3
LN_EPS = 1e-5
NEG_INF = -1e30
VERY_NEG = -3e38
MASK_BIG = 2.0 ** 100
LOG2_E = 1.4426950408889634

LANES = 128
MXU_DIM = 256
VMEM_LIMIT = 56 * 1024 * 1024

NP_COLS = 13312
OFF_GATES = 0
OFF_Q = 6144
OFF_R = 7168
OFF_K = 8192
OFF_V = 9216
OFF_POOL = 10240
OFF_KVC = 11264
OFF_KVS = 11776
OFF_KVW = 12288
OFF_SMALL = 12800
SMALL_W = 512
SM_WL = 128
SM_AL = 192
SM_GL = 256

NSA_TQ = 256
NSA_TK = 256
ONES_ROWS = 16
RW_CHUNK = 64
RW_TILE = 128
MOE_TM = 512
MOE_NF = 7


def _cp(sem, vmem=VMEM_LIMIT):
    return pltpu.CompilerParams(dimension_semantics=sem, vmem_limit_bytes=vmem)


def _dot(a, b):
    return jnp.dot(a, b, preferred_element_type=F32)


def _dot_nt(a, b):
    return lax.dot_general(a, b, (((1,), (1,)), ((), ())), preferred_element_type=F32)


def _layer_norm_rows(y, g, b):
    mu = jnp.mean(y, axis=-1, keepdims=True)
    d = y - mu
    var = jnp.mean(d * d, axis=-1, keepdims=True)
    return d * lax.rsqrt(var + LN_EPS) * g + b


def _mm_kernel(x_ref, w_ref, o_ref):
    o_ref[...] = _dot(x_ref[...], w_ref[...]).astype(o_ref.dtype)


def _matmul(x, w, *, tm, tn, out_dtype):
    m, k = x.shape
    n = w.shape[1]
    return pl.pallas_call(
        _mm_kernel,
        grid=(n // tn, m // tm),
        in_specs=[pl.BlockSpec((tm, k), lambda j, i: (i, 0)),
                  pl.BlockSpec((k, tn), lambda j, i: (0, j))],
        out_specs=pl.BlockSpec((tm, tn), lambda j, i: (i, j)),
        out_shape=jax.ShapeDtypeStruct((m, n), out_dtype),
        compiler_params=_cp(("parallel", "parallel")),
        name="matmul",
    )(x, w)


def _mm_ln_kernel(x_ref, w_ref, res_ref, g_ref, b_ref, o32_ref, o16_ref, *, alpha):
    y = alpha * res_ref[...] + _dot(x_ref[...], w_ref[...])
    out = _layer_norm_rows(y, g_ref[...], b_ref[...])
    o32_ref[...] = out
    o16_ref[...] = out.astype(BF16)


def _matmul_res_ln(x, w, res, g, b, *, alpha, tm=256):
    m, k = x.shape
    n = w.shape[1]
    row = lambda i: (i, 0)
    fix = lambda i: (0, 0)
    return pl.pallas_call(
        functools.partial(_mm_ln_kernel, alpha=alpha),
        grid=(m // tm,),
        in_specs=[pl.BlockSpec((tm, k), row), pl.BlockSpec((k, n), fix), pl.BlockSpec((tm, n), row),
                  pl.BlockSpec((1, n), fix), pl.BlockSpec((1, n), fix)],
        out_specs=[pl.BlockSpec((tm, n), row), pl.BlockSpec((tm, n), row)],
        out_shape=[jax.ShapeDtypeStruct((m, n), F32), jax.ShapeDtypeStruct((m, n), BF16)],
        compiler_params=_cp(("parallel",)),
        name="matmul_res_ln",
    )(x, w, res, g.reshape(1, n), b.reshape(1, n))


def _memkv_kernel(m_ref, g_ref, b_ref, w_ref, o_ref):
    xn = _layer_norm_rows(m_ref[...], g_ref[...], b_ref[...])
    o_ref[...] = _dot(xn.astype(BF16), w_ref[...]).astype(o_ref.dtype)


def _memkv(mem2d, g, b, w, *, tm=256):
    m, d = mem2d.shape
    n = w.shape[1]
    row = lambda i: (i, 0)
    fix = lambda i: (0, 0)
    return pl.pallas_call(
        _memkv_kernel,
        grid=(m // tm,),
        in_specs=[pl.BlockSpec((tm, d), row), pl.BlockSpec((1, d), fix), pl.BlockSpec((1, d), fix),
                  pl.BlockSpec((d, n), fix)],
        out_specs=pl.BlockSpec((tm, n), row),
        out_shape=jax.ShapeDtypeStruct((m, n), BF16),
        compiler_params=_cp(("parallel",)),
        name="memkv",
    )(mem2d, g.reshape(1, d), b.reshape(1, d), w)


def _rope_table_kernel(pos_ref, invf_ref, c_ref, shi_ref, slo_ref):
    ang = pos_ref[...].astype(F32) * invf_ref[...]
    d = lax.broadcasted_iota(I32, ang.shape, 1) % HEAD_DIM
    cs = jnp.cos(ang)
    sn = jnp.sin(ang)
    c_ref[...] = jnp.where(d < ROPE_DIM, cs, 1.0)
    shi_ref[...] = jnp.where((d >= ROPE_HALF) & (d < ROPE_DIM), sn, 0.0)
    slo_ref[...] = jnp.where(d < ROPE_HALF, -sn, 0.0)


def _rope_tables(pos_col, invf_lane, *, ts):
    n = pos_col.shape[0]
    row = lambda i: (i, 0)
    out = jax.ShapeDtypeStruct((n, LANES), F32)
    return pl.pallas_call(
        _rope_table_kernel,
        grid=(n // ts,),
        in_specs=[pl.BlockSpec((ts, 1), row), pl.BlockSpec((1, LANES), lambda i: (0, 0))],
        out_specs=[pl.BlockSpec((ts, LANES), row)] * 3,
        out_shape=[out, out, out],
        compiler_params=_cp(("parallel",)),
        name="rope_tables",
    )(pos_col, invf_lane)


def _rope128(x, c, shi, slo):
    return x * c + pltpu.roll(x, ROPE_HALF, 1) * shi + pltpu.roll(x, LANES - ROPE_HALF, 1) * slo


def _nsa_prep_kernel(q_ref, ks_ref, vs_ref, kw_ref, vw_ref, sm_ref, c_ref, shi_ref, slo_ref,
                     qt_ref, ksa_ref, vst_ref, kwh_ref, vwt_ref, gt_ref):
    c = c_ref[...]
    shi = shi_ref[...]
    slo = slo_ref[...]
    ts = c.shape[0]
    scale = HEAD_DIM ** -0.5 * LOG2_E
    for j in range(NSA_WIDTH // LANES):
        sl = slice(LANES * j, LANES * (j + 1))
        qt_ref[sl, :] = (_rope128(q_ref[:, sl].astype(F32), c, shi, slo) * scale).T.astype(BF16)
    t = pl.program_id(1) * ts + lax.broadcasted_iota(I32, (ts, LANES), 0)
    lane = lax.broadcasted_iota(I32, (ts, LANES), 1)
    onehot = jnp.where(lane - HEAD_DIM == t // SLC_BLOCK, 1.0, 0.0)
    lo = lane < HEAD_DIM
    for j in range(NSA_KV_WIDTH // LANES):
        sl = slice(LANES * j, LANES * (j + 1))
        ks = _rope128(ks_ref[:, sl].astype(F32), c, shi, slo)
        kw = _rope128(kw_ref[:, sl].astype(F32), c, shi, slo).astype(BF16)
        ksa_ref[2 * j] = jnp.where(lo, ks, onehot).astype(BF16)
        ksa_ref[2 * j + 1] = jnp.where(lo, pltpu.roll(ks, HEAD_DIM, 1), onehot).astype(BF16)
        kwh_ref[2 * j] = kw[:, 0:HEAD_DIM]
        kwh_ref[2 * j + 1] = kw[:, HEAD_DIM:LANES]
    for k in range(ts // LANES):
        rs = slice(LANES * k, LANES * (k + 1))
        vst_ref[k] = vs_ref[rs, :].astype(F32).T.astype(BF16)
        vwt_ref[k] = vw_ref[rs, :].astype(F32).T.astype(BF16)
    gt_ref[...] = jax.nn.sigmoid(sm_ref[:, 0:LANES].astype(F32)).T


def _nsa_prep(proj3, c, shi, slo, *, ts=512):
    b, s, _ = proj3.shape
    kvw = NSA_KV_WIDTH
    tab = pl.BlockSpec((None, ts, LANES), lambda bi, i: (bi, i, 0))
    kv = lambda off: pl.BlockSpec((None, ts, kvw), lambda bi, i: (bi, i, off // kvw))
    vt_spec = pl.BlockSpec((None, ts // LANES, kvw, LANES), lambda bi, i: (bi, i, 0, 0))
    vt_shape = jax.ShapeDtypeStruct((b, s // LANES, kvw, LANES), BF16)
    return pl.pallas_call(
        _nsa_prep_kernel,
        grid=(b, s // ts),
        in_specs=[
            pl.BlockSpec((None, ts, NSA_WIDTH), lambda bi, i: (bi, i, OFF_Q // NSA_WIDTH)),
            kv(OFF_KVS), kv(OFF_KVS + kvw), kv(OFF_KVW), kv(OFF_KVW + kvw),
            pl.BlockSpec((None, ts, SMALL_W), lambda bi, i: (bi, i, OFF_SMALL // SMALL_W)),
            tab, tab, tab],
        out_specs=[
            pl.BlockSpec((None, NSA_WIDTH, ts), lambda bi, i: (bi, 0, i)),
            pl.BlockSpec((None, NSA_KV_HEADS, ts, LANES), lambda bi, i: (bi, 0, i, 0)),
            vt_spec,
            pl.BlockSpec((None, NSA_KV_HEADS, ts, HEAD_DIM), lambda bi, i: (bi, 0, i, 0)),
            vt_spec,
            pl.BlockSpec((None, LANES, ts), lambda bi, i: (bi, 0, i))],
        out_shape=[jax.ShapeDtypeStruct((b, NSA_WIDTH, s), BF16),
                   jax.ShapeDtypeStruct((b, NSA_KV_HEADS, s, LANES), BF16),
                   vt_shape,
                   jax.ShapeDtypeStruct((b, NSA_KV_HEADS, s, HEAD_DIM), BF16),
                   vt_shape,
                   jax.ShapeDtypeStruct((b, LANES, s), F32)],
        compiler_params=_cp(("parallel", "parallel")),
        name="nsa_prep",
    )(proj3, proj3, proj3, proj3, proj3, proj3, c, shi, slo)


def _gelu_tanh(x):
    return 0.5 * x * (1.0 + jnp.tanh(0.7978845608028654 * (x + 0.044715 * x * x * x)))


def _cmp_kernel(x_ref, pe_ref, w1_ref, b1_ref, w2_ref, c_ref, shi_ref, slo_ref, o_ref, *, is_key):
    x = x_ref[...].astype(F32)
    n = x.shape[0]
    half = x.shape[1]
    xa = (x + pe_ref[0:1, :]).astype(BF16)
    xb = (x + pe_ref[1:2, :]).astype(BF16)
    ha = _dot(xa, w1_ref[0:half, :])
    hb = _dot(xb, w1_ref[half:2 * half, :])
    hid = ha + pltpu.roll(hb, n - 1, 0) + b1_ref[...]
    out = _dot(_gelu_tanh(hid).astype(BF16), w2_ref[...])
    if is_key:
        out = _rope128(out, c_ref[...], shi_ref[...], slo_ref[...])
        o_ref[...] = out[:, 0:HEAD_DIM].astype(BF16)
    else:
        o_ref[...] = out.T[0:HEAD_DIM, :].astype(BF16)


def _compress(xu, pe, w1, b1, w2p, c, shi, slo, *, is_key):
    b, nh, n, w = xu.shape
    fix2 = lambda bi, h: (0, 0)
    tab = pl.BlockSpec((None, n, LANES), lambda bi, h: (bi, 0, 0))
    oshape = (n, HEAD_DIM) if is_key else (HEAD_DIM, n)
    return pl.pallas_call(
        functools.partial(_cmp_kernel, is_key=is_key),
        grid=(b, nh),
        in_specs=[pl.BlockSpec((None, None, n, w), lambda bi, h: (bi, h, 0, 0)),
                  pl.BlockSpec((2, w), fix2), pl.BlockSpec((2 * w, CMP_HIDDEN), fix2),
                  pl.BlockSpec((1, CMP_HIDDEN), fix2), pl.BlockSpec((CMP_HIDDEN, LANES), fix2),
                  tab, tab, tab],
        out_specs=pl.BlockSpec((None, None) + oshape, lambda bi, h: (bi, h, 0, 0)),
        out_shape=jax.ShapeDtypeStruct((b, nh) + oshape, BF16),
        compiler_params=_cp(("parallel", "parallel")),
        name="nsa_compress",
    )(xu, pe, w1, b1, w2p, c, shi, slo)


def _nsa_kernel(qt_ref, gt_ref, kc_ref, vct_ref, mt_ref, ksa_ref, vst_ref, kwh_ref, vwt_ref, o_ref):
    tq = qt_ref.shape[1]
    cols = NSA_GROUP * tq
    n_cmp = kc_ref.shape[1]
    n_slc = mt_ref.shape[0]
    i = pl.program_id(1)
    t0 = i * tq
    loc = lax.broadcasted_iota(I32, (1, tq), 1)
    loc_c = jnp.concatenate([loc] * NSA_GROUP, axis=1)
    tok = t0 + loc
    tok_c = t0 + loc_c
    cmp_end = lax.broadcasted_iota(I32, (n_cmp, 1), 0) * CMP_STRIDE + (CMP_BLOCK - 1)
    cmp_mask = cmp_end <= tok_c
    blk = lax.broadcasted_iota(I32, (n_slc, 1), 0)
    blk_f = blk.astype(F32)
    cur = tok // SLC_BLOCK
    forced = (blk == 0) | (blk == cur) | (blk == cur - 1)
    allowed = blk <= cur
    tk = NSA_TK
    n_full = t0 // tk
    last_mask = (n_full * tk + lax.broadcasted_iota(I32, (tk, 1), 0)) <= tok_c
    n_win = WINDOW + tq
    ws = pl.multiple_of(jnp.maximum(t0 - WINDOW, 0), tq)
    wt0 = ws // LANES
    dwin = tok_c - (ws + lax.broadcasted_iota(I32, (n_win, 1), 0))
    win_mask = (dwin >= 0) & (dwin < WINDOW)
    cmp_to_slc_t = mt_ref[...]
    ones_rows = jnp.ones((ONES_ROWS, n_win), BF16)

    kvh = range(NSA_KV_HEADS)
    hss = [slice(HEAD_DIM * h, HEAD_DIM * (h + 1)) for h in kvh]
    qhs = [jnp.concatenate(
        [qt_ref[HEAD_DIM * (NSA_GROUP * h + g):HEAD_DIM * (NSA_GROUP * h + g + 1), :]
         for g in range(NSA_GROUP)], axis=1) for h in kvh]
    ss = [jnp.where(cmp_mask, _dot(kc_ref[h], qhs[h]), NEG_INF) for h in kvh]
    es = [jnp.where(cmp_mask, jnp.exp2(s - jnp.max(s, axis=0, keepdims=True)), 0.0) for s in ss]
    ps = [e / jnp.maximum(jnp.sum(e, axis=0, keepdims=True), 1e-30) for e in es]
    o_cs = [_dot(vct_ref[h], ps[h].astype(BF16)) for h in kvh]
    p4s = [sum(p[:, g * tq:(g + 1) * tq] for g in range(1, NSA_GROUP)) + p[:, 0:tq] for p in ps]
    p_his = [p4.astype(BF16) for p4 in p4s]
    p_los = [(p4 - p_hi.astype(F32)).astype(BF16) for p4, p_hi in zip(p4s, p_his)]
    imps = [_dot(cmp_to_slc_t, p_hi) + _dot(cmp_to_slc_t, p_lo) for p_hi, p_lo in zip(p_his, p_los)]
    imps = [jnp.where(forced, SEL_FORCE, jnp.where(allowed, imp, -SEL_FORCE)) for imp in imps]
    sels = [jnp.zeros((n_slc, tq), F32) for _ in kvh]
    for _ in range(SLC_TOPN):
        bests = [jnp.max(imp, axis=0, keepdims=True) for imp in imps]
        idxs = [jnp.min(jnp.where(imp == best, blk_f, float(n_slc)), axis=0, keepdims=True)
                for imp, best in zip(imps, bests)]
        hits = [blk_f == idx for idx in idxs]
        sels = [jnp.where(hit, 1.0, sel) for hit, sel in zip(hits, sels)]
        imps = [jnp.where(hit, VERY_NEG, imp) for hit, imp in zip(hits, imps)]
    biases = [jnp.where(sel > 0.5, 0.0, -MASK_BIG).astype(BF16) for sel in sels]
    q_augs = [jnp.concatenate([qh, jnp.concatenate([bias] * NSA_GROUP, axis=1)], axis=0)
              for qh, bias in zip(qhs, biases)]

    def sel_step(j, carry, masked):
        k0 = pl.multiple_of(j * tk, tk)
        scs = [_dot(ksa_ref[h, pl.ds(k0, tk), :], q_augs[h]) for h in kvh]
        if masked:
            scs = [jnp.where(last_mask, sc, NEG_INF) for sc in scs]
        m_news = [jnp.maximum(carry[h][0], jnp.max(scs[h], axis=0, keepdims=True)) for h in kvh]
        alphas = [jnp.exp2(carry[h][0] - m_news[h]) for h in kvh]
        pps = [jnp.exp2(scs[h] - m_news[h]).astype(BF16) for h in kvh]
        vts = [jnp.concatenate([vst_ref[j * (tk // LANES) + d, hss[h], :] for d in range(tk // LANES)], axis=1)
               for h in kvh]
        vts = [jnp.concatenate([vt, ones_rows[:, 0:tk]], axis=0) for vt in vts]
        accs = [alphas[h] * carry[h][1] + _dot(vts[h], pps[h]) for h in kvh]
        return tuple((m_news[h], accs[h]) for h in kvh)

    carry = tuple((jnp.full((1, cols), VERY_NEG, F32), jnp.zeros((HEAD_DIM + ONES_ROWS, cols), F32))
                  for _ in kvh)
    carry = lax.fori_loop(0, n_full, lambda j, cr: sel_step(j, cr, False), carry)
    carry = sel_step(n_full, carry, True)
    o_ss = [acc[0:HEAD_DIM] / acc[HEAD_DIM:HEAD_DIM + 1] for _, acc in carry]
    sws = [jnp.where(win_mask, _dot(kwh_ref[h, pl.ds(ws, n_win), :], qhs[h]), NEG_INF) for h in kvh]
    ews = [jnp.exp2(sw - jnp.max(sw, axis=0, keepdims=True)).astype(BF16) for sw in sws]
    vws = [jnp.concatenate([vwt_ref[wt0 + d, hss[h], :] for d in range(n_win // LANES)], axis=1) for h in kvh]
    vws = [jnp.concatenate([vw, ones_rows], axis=0) for vw in vws]
    a_ws = [_dot(vw, ew) for vw, ew in zip(vws, ews)]
    o_ws = [a[0:HEAD_DIM] / a[HEAD_DIM:HEAD_DIM + 1] for a in a_ws]
    out_rows = []
    for h in kvh:
        for g in range(NSA_GROUP):
            head = NSA_GROUP * h + g
            cs = slice(g * tq, (g + 1) * tq)
            out_rows.append(gt_ref[head:head + 1, :] * o_cs[h][:, cs]
                            + gt_ref[NSA_HEADS + head:NSA_HEADS + head + 1, :] * o_ss[h][:, cs]
                            + gt_ref[2 * NSA_HEADS + head:2 * NSA_HEADS + head + 1, :] * o_ws[h][:, cs])
    o_ref[...] = jnp.concatenate(out_rows, axis=0).T.astype(BF16)


def _cmp_slc_weights(n_cmp_pad, n_cmp, n_slc_pad):
    c0 = np.arange(n_cmp_pad)[:, None] * CMP_STRIDE
    s0 = np.arange(n_slc_pad)[None, :] * SLC_BLOCK
    shared = np.minimum(c0 + CMP_BLOCK, s0 + SLC_BLOCK) - np.maximum(c0, s0)
    w = np.maximum(shared, 0) / CMP_STRIDE
    w[n_cmp:] = 0.0
    return w.astype(np.float32)


def _nsa_attention(q_t, g_t, kc, vc_t, ksa, vs_t, kwh, vw_t):
    b, _, s = q_t.shape
    tq = NSA_TQ
    n_cmp_pad = kc.shape[2]
    n_slc = HEAD_DIM
    assert SLC_TOPN <= s // SLC_BLOCK <= n_slc and s >= WINDOW + tq and tq % LANES == 0
    assert s % NSA_TK == 0 and NSA_TK % tq == 0 and WINDOW % tq == 0
    m_t = jnp.asarray(_cmp_slc_weights(n_cmp_pad, s // CMP_STRIDE - 1, n_slc).T, BF16)
    per_b4 = lambda bi, i: (bi, 0, 0, 0)
    return pl.pallas_call(
        _nsa_kernel,
        grid=(b, s // tq),
        in_specs=[
            pl.BlockSpec((None, NSA_WIDTH, tq), lambda bi, i: (bi, 0, i)),
            pl.BlockSpec((None, LANES, tq), lambda bi, i: (bi, 0, i)),
            pl.BlockSpec((None, NSA_KV_HEADS, n_cmp_pad, HEAD_DIM), per_b4),
            pl.BlockSpec((None, NSA_KV_HEADS, HEAD_DIM, n_cmp_pad), per_b4),
            pl.BlockSpec((n_slc, n_cmp_pad), lambda bi, i: (0, 0)),
            pl.BlockSpec((None, NSA_KV_HEADS, s, LANES), per_b4),
            pl.BlockSpec((None, s // LANES, NSA_KV_WIDTH, LANES), per_b4),
            pl.BlockSpec((None, NSA_KV_HEADS, s, HEAD_DIM), per_b4),
            pl.BlockSpec((None, s // LANES, NSA_KV_WIDTH, LANES), per_b4)],
        out_specs=pl.BlockSpec((None, tq, NSA_WIDTH), lambda bi, i: (bi, i, 0)),
        out_shape=jax.ShapeDtypeStruct((b, s, NSA_WIDTH), BF16),
        compiler_params=_cp(("parallel", "arbitrary")),
        name="nsa_attention",
    )(q_t, g_t, kc, vc_t, m_t, ksa, vs_t, kwh, vw_t)


def _softplus(x):
    return jnp.maximum(x, 0.0) + jnp.log(1.0 + jnp.exp(-jnp.abs(x)))


def _split3(x):
    h1 = x.astype(BF16)
    r1 = x - h1.astype(F32)
    h2 = r1.astype(BF16)
    h3 = (r1 - h2.astype(F32)).astype(BF16)
    return h1, h2, h3


def _seg_sum(x, bd):
    cols = []
    for j in range(x.shape[1] // LANES):
        xs = x[:, LANES * j:LANES * (j + 1)]
        h1, h2, h3 = _split3(xs)
        cols.append(_dot(h1, bd) + _dot(h2, bd) + _dot(h3, bd))
    return jnp.concatenate(cols, axis=1)


def _rwkv_kernel(*refs, has_vres):
    it = iter(refs)
    r_ref, k_ref, v_ref, sm_ref = next(it), next(it), next(it), next(it)
    vf_ref = next(it) if has_vres else None
    mu_ref, musm_ref, w0_ref, w2_ref, a0_ref, a2_ref, g2_ref = (next(it) for _ in range(7))
    kk_ref, ka_ref, rk_ref, lng_ref, lnb_ref = (next(it) for _ in range(5))
    if has_vres:
        v0_ref, v1_ref, v2_ref = next(it), next(it), next(it)
    tri_ref, bd_ref = next(it), next(it)
    y_ref, vo_ref = next(it), next(it)
    z_ref, prev_ref, prevsm_ref = next(it), next(it), next(it)

    c = r_ref.shape[0]
    width = r_ref.shape[1]

    @pl.when(pl.program_id(1) == 0)
    def _():
        z_ref[...] = jnp.zeros(z_ref.shape, F32)
        prev_ref[...] = jnp.zeros(prev_ref.shape, F32)
        prevsm_ref[...] = jnp.zeros(prevsm_ref.shape, F32)

    row = lax.broadcasted_iota(I32, (c, 1), 0)

    def shift_mix(x, pref, idx, mu):
        prev = pref[idx, 0:1, :]
        xs = jnp.where(row == 0, prev, pltpu.roll(x, 1, 0))
        pref[idx, 0:1, :] = x[c - 1:c, :]
        return x + (xs - x) * mu

    xr = shift_mix(r_ref[...].astype(F32), prev_ref, 0, mu_ref[0:1, :])
    xk = shift_mix(k_ref[...].astype(F32), prev_ref, 1, mu_ref[1:2, :])
    xv = shift_mix(v_ref[...].astype(F32), prev_ref, 2, mu_ref[2:3, :])
    sm = shift_mix(sm_ref[...].astype(F32), prevsm_ref, 0, musm_ref[...])

    wpre = w0_ref[...] + _dot(jnp.tanh(sm).astype(BF16), w2_ref[...])
    lw = -jnp.exp(-_softplus(-wpre) - 0.5)
    a = jax.nn.sigmoid(a0_ref[...] + _dot(sm.astype(BF16), a2_ref[...]))
    gate = _dot(jax.nn.sigmoid(sm).astype(BF16), g2_ref[...])
    if has_vres:
        lo = _dot(_dot(xv.astype(BF16), v1_ref[...]).astype(BF16), v2_ref[...])
        xv = xv + (vf_ref[...].astype(F32) - xv) * jax.nn.sigmoid(v0_ref[...] + lo)
    vo_ref[...] = xv.astype(BF16)

    bd = bd_ref[...]
    kk = xk * kk_ref[...]
    kk = kk / jnp.maximum(jnp.sqrt(_seg_sum(kk * kk, bd)), 1e-12)
    kmod = xk * (1.0 + (a - 1.0) * ka_ref[...])
    ah = -kk
    bh = kk * a

    tri = tri_ref[...]
    l1, l2, l3 = _split3(lw)
    cum = _dot(tri, l1) + _dot(tri, l2) + _dot(tri, l3)
    cumx = cum - lw
    bonus_in = xr * kmod * rk_ref[...]
    cc = RW_CHUNK
    chunks = range(c // cc)
    rss = [slice(cc * n, cc * (n + 1)) for n in chunks]
    a_t, r_t, b_t, k_t, a_0, r_0, b_e, k_e, w_end = ([] for _ in range(9))
    for rs in rss:
        cu, cx = cum[rs], cumx[rs]
        cmid = cu[cc // 2 - 1:cc // 2, :]
        cend = cu[cc - 1:cc, :]
        e_inv = jnp.exp(cmid - cu)
        e_end = jnp.exp(cend - cu)
        a_t.append((ah[rs] * jnp.exp(cx - cmid)).astype(BF16))
        r_t.append((xr[rs] * jnp.exp(cu - cmid)).astype(BF16))
        b_t.append((bh[rs] * e_inv).astype(BF16))
        k_t.append((kmod[rs] * e_inv).astype(BF16))
        a_0.append((ah[rs] * jnp.exp(cx)).astype(BF16))
        r_0.append((xr[rs] * jnp.exp(cu)).astype(BF16))
        b_e.append((bh[rs] * e_end).astype(BF16))
        k_e.append((kmod[rs] * e_end).astype(BF16))
        w_end.append(jnp.exp(cend))

    ri = lax.broadcasted_iota(I32, (cc, cc), 0)
    ci = lax.broadcasted_iota(I32, (cc, cc), 1)
    strict = ci < ri
    incl = ci <= ri
    eye = jnp.where(ci == ri, 1.0, 0.0)
    n_levels = int(np.log2(cc))
    assert 2 ** n_levels == cc

    heads = range(width // RW_HEAD)
    sls = [slice(RW_HEAD * h, RW_HEAD * (h + 1)) for h in heads]
    pairs = [(n, h) for n in chunks for h in heads]
    vhs = {(n, h): xv[rss[n], sls[h]] for n, h in pairs}
    vhbs = {p: vhs[p].astype(BF16) for p in pairs}
    m1s = {(n, h): _dot_nt(jnp.concatenate([a_t[n][:, sls[h]], r_t[n][:, sls[h]]], axis=0),
                           jnp.concatenate([b_t[n][:, sls[h]], k_t[n][:, sls[h]]], axis=0))
           for n, h in pairs}
    nmat = {p: jnp.where(strict, m1s[p][0:cc, 0:cc], 0.0) for p in pairs}
    a_aks = {p: jnp.where(strict, m1s[p][0:cc, cc:2 * cc], 0.0).astype(BF16) for p in pairs}
    pqs = {p: jnp.concatenate([jnp.where(incl, m1s[p][cc:2 * cc, 0:cc], 0.0),
                               jnp.where(incl, m1s[p][cc:2 * cc, cc:2 * cc], 0.0)], axis=1).astype(BF16)
           for p in pairs}
    tinv = {p: eye + nmat[p] for p in pairs}
    npows = {p: nmat[p].astype(BF16) for p in pairs}
    for _ in range(1, n_levels):
        npows = {p: _dot(npows[p], npows[p]).astype(BF16) for p in pairs}
        tinv = {p: tinv[p] + _dot(npows[p], tinv[p].astype(BF16)) for p in pairs}
    tinvb = {p: tinv[p].astype(BF16) for p in pairs}
    bkes = {(n, h): jnp.concatenate([b_e[n][:, sls[h]], k_e[n][:, sls[h]]], axis=0) for n, h in pairs}

    zts = [z_ref[h] for h in heads]
    ys = {}
    for n in chunks:
        ztbs = [zt.astype(BF16) for zt in zts]
        xs = [_dot_nt(a_0[n][:, sls[h]], ztbs[h]) + _dot(a_aks[n, h], vhbs[n, h]) for h in heads]
        y0s = [_dot_nt(r_0[n][:, sls[h]], ztbs[h]) for h in heads]
        us = [_dot(tinvb[n, h], xs[h].astype(BF16)) for h in heads]
        uvs = [jnp.concatenate([us[h], vhs[n, h]], axis=0) for h in heads]
        for h in heads:
            ys[n, h] = y0s[h] + _dot(pqs[n, h], uvs[h].astype(BF16))
        zts = [zts[h] * w_end[n][:, sls[h]] + _dot(uvs[h].T.astype(BF16), bkes[n, h]) for h in heads]
    for h in heads:
        z_ref[h] = zts[h]
    for n, h in pairs:
        rs, sl = rss[n], sls[h]
        y = ys[n, h]
        ym = jnp.mean(y, axis=1, keepdims=True)
        yd = y - ym
        yv = jnp.mean(yd * yd, axis=1, keepdims=True)
        yn = yd * lax.rsqrt(yv + RW_LNX_EPS) * lng_ref[:, sl] + lnb_ref[:, sl]
        bonus = jnp.sum(bonus_in[rs, sl], axis=1, keepdims=True) * vhs[n, h]
        y_ref[rs, sl] = ((yn + bonus) * gate[rs, sl]).astype(BF16)


def _rwkv(proj3, vfirst, p, *, has_vres):
    b, s, _ = proj3.shape
    c = RW_TILE
    w = RW_WIDTH
    cb = lambda off: off // w
    fix = lambda bi, i: (0, 0)
    blk = lambda off: pl.BlockSpec((None, c, w), lambda bi, i: (bi, i, cb(off)))
    row_w = pl.BlockSpec((1, w), fix)
    in_specs = [blk(OFF_R), blk(OFF_K), blk(OFF_V),
                pl.BlockSpec((None, c, SMALL_W), lambda bi, i: (bi, i, OFF_SMALL // SMALL_W))]
    args = [proj3, proj3, proj3, proj3]
    if has_vres:
        in_specs.append(pl.BlockSpec((None, c, w), lambda bi, i: (bi, i, 0)))
        args.append(vfirst)
    in_specs += [pl.BlockSpec((3, w), fix), pl.BlockSpec((1, SMALL_W), fix), row_w,
                 pl.BlockSpec((SMALL_W, w), fix), row_w, pl.BlockSpec((SMALL_W, w), fix),
                 pl.BlockSpec((SMALL_W, w), fix), row_w, row_w, row_w, row_w, row_w]
    args += [p["mu"], p["mu_sm"], p["w0"], p["w2"], p["a0"], p["a2"], p["g2"],
             p["k_k"], p["k_a"], p["r_k"], p["lnx_g"], p["lnx_b"]]
    if has_vres:
        in_specs += [row_w, pl.BlockSpec((w, LANES), fix), pl.BlockSpec((LANES, w), fix)]
        args += [p["v0"], p["v1"], p["v2"]]
    assert c % RW_CHUNK == 0 and s % c == 0
    pos = np.arange(c)
    tri = jnp.asarray(((pos[None, :] <= pos[:, None])
                       & (pos[None, :] // RW_CHUNK == pos[:, None] // RW_CHUNK)).astype(np.float32), BF16)
    lane_head = np.arange(LANES) // RW_HEAD
    bd = jnp.asarray((lane_head[:, None] == lane_head[None, :]).astype(np.float32), BF16)
    in_specs += [pl.BlockSpec((c, c), fix), pl.BlockSpec((LANES, LANES), fix)]
    args += [tri, bd]
    out_blk = pl.BlockSpec((None, c, w), lambda bi, i: (bi, i, 0))
    return pl.pallas_call(
        functools.partial(_rwkv_kernel, has_vres=has_vres),
        grid=(b, s // c),
        in_specs=in_specs,
        out_specs=[out_blk, out_blk],
        out_shape=[jax.ShapeDtypeStruct((b, s, w), BF16), jax.ShapeDtypeStruct((b, s, w), BF16)],
        scratch_shapes=[pltpu.VMEM((w // RW_HEAD, RW_HEAD, RW_HEAD), F32),
                        pltpu.VMEM((3, 8, w), F32), pltpu.VMEM((1, 8, SMALL_W), F32)],
        compiler_params=_cp(("parallel", "arbitrary")),
        name="rwkv7",
    )(*args)


POOL_HALO = 16


def _pool_kernel(cur_ref, halo_ref, w_ref, sc_ref, o_ref):
    i = pl.program_id(1)
    tp = cur_ref.shape[0]
    cur = cur_ref[...].astype(F32)
    halo = jnp.where(i > 0, halo_ref[...].astype(F32), 0.0)
    ext = jnp.concatenate([halo, cur], axis=0)
    sums = {1: ext}
    win = 1
    while win < max(POOL_WINDOWS):
        sums[2 * win] = sums[win] + pltpu.roll(sums[win], win, 0)
        win *= 2
    count = (i * tp + 1 + lax.broadcasted_iota(I32, (tp, 1), 0)).astype(F32)
    for gi, win in enumerate(POOL_WINDOWS):
        sl = slice(POOL_GROUP * gi, POOL_GROUP * (gi + 1))
        mean = sums[win][POOL_HALO:, sl] / jnp.minimum(count, float(win))
        y = _dot((mean - cur[:, sl]).astype(BF16), w_ref[gi])
        o_ref[:, sl] = (y * sc_ref[:, sl]).astype(BF16)


def _pool(proj3, pool_w, pool_scale, *, tp=256):
    b, s, _ = proj3.shape
    w = POOL_WIDTH
    cb = OFF_POOL // w
    per = tp // POOL_HALO
    return pl.pallas_call(
        _pool_kernel,
        grid=(b, s // tp),
        in_specs=[pl.BlockSpec((None, tp, w), lambda bi, i: (bi, i, cb)),
                  pl.BlockSpec((None, POOL_HALO, w), lambda bi, i: (bi, jnp.maximum(i * per - 1, 0), cb)),
                  pl.BlockSpec((len(POOL_WINDOWS), POOL_GROUP, POOL_GROUP), lambda bi, i: (0, 0, 0)),
                  pl.BlockSpec((1, w), lambda bi, i: (0, 0))],
        out_specs=pl.BlockSpec((None, tp, w), lambda bi, i: (bi, i, 0)),
        out_shape=jax.ShapeDtypeStruct((b, s, w), BF16),
        compiler_params=_cp(("parallel", "parallel")),
        name="pool",
    )(proj3, proj3, pool_w, pool_scale.reshape(1, w))


def _merge_kernel(ya_ref, yb_ref, yc_ref, ga_ref, gb_ref, gc_ref, wb_ref, o_ref):
    acc = jax.nn.sigmoid(ga_ref[...].astype(F32)) * _dot(ya_ref[...], wb_ref[0])
    acc = acc + jax.nn.sigmoid(gb_ref[...].astype(F32)) * _dot(yb_ref[...], wb_ref[1])
    acc = acc + jax.nn.sigmoid(gc_ref[...].astype(F32)) * _dot(yc_ref[...], wb_ref[2])
    o_ref[...] = acc.astype(BF16)


def _merge(ya, yb, yc, proj, wb, *, tm=256):
    t, w = ya.shape
    d = wb.shape[2]
    row = lambda i: (i, 0)
    gate = lambda k: pl.BlockSpec((tm, d), lambda i: (i, k))
    return pl.pallas_call(
        _merge_kernel,
        grid=(t // tm,),
        in_specs=[pl.BlockSpec((tm, w), row)] * 3 + [gate(0), gate(1), gate(2),
                                                      pl.BlockSpec((N_BRANCH, w, d), lambda i: (0, 0, 0))],
        out_specs=pl.BlockSpec((tm, d), row),
        out_shape=jax.ShapeDtypeStruct((t, d), BF16),
        compiler_params=_cp(("parallel",)),
        name="merge",
    )(ya, yb, yc, proj, proj, proj, wb)


def _xattn_kernel(h16_ref, h32_ref, wq_ref, mkv_ref, wo_ref, g_ref, b_ref, o32_ref, o16_ref, *, alpha):
    q = _dot(h16_ref[...], wq_ref[...])
    scale = XA_HEAD_DIM ** -0.5
    outs = []
    for hd in range(XA_HEADS):
        sl = slice(XA_HEAD_DIM * hd, XA_HEAD_DIM * (hd + 1))
        slv = slice(XA_WIDTH + XA_HEAD_DIM * hd, XA_WIDTH + XA_HEAD_DIM * (hd + 1))
        s = _dot_nt(q[:, sl].astype(BF16), mkv_ref[:, sl]) * scale
        e = jnp.exp(s - jnp.max(s, axis=1, keepdims=True))
        p = e / jnp.sum(e, axis=1, keepdims=True)
        outs.append(_dot(p.astype(BF16), mkv_ref[:, slv]))
    o = jnp.concatenate(outs, axis=1).astype(BF16)
    y = alpha * h32_ref[...] + _dot(o, wo_ref[...])
    out = _layer_norm_rows(y, g_ref[...], b_ref[...])
    o32_ref[...] = out
    o16_ref[...] = out.astype(BF16)


def _xattn(h16, h32, wq, memkv, wo, g, b, *, alpha, seq, tm=256):
    t, d = h16.shape
    per_b = seq // tm
    nm = memkv.shape[1]
    row = lambda i: (i, 0)
    fix = lambda i: (0, 0)
    return pl.pallas_call(
        functools.partial(_xattn_kernel, alpha=alpha),
        grid=(t // tm,),
        in_specs=[pl.BlockSpec((tm, d), row), pl.BlockSpec((tm, d), row),
                  pl.BlockSpec((d, XA_WIDTH), fix),
                  pl.BlockSpec((None, nm, 2 * XA_WIDTH), lambda i: (i // per_b, 0, 0)),
                  pl.BlockSpec((XA_WIDTH, d), fix), pl.BlockSpec((1, d), fix), pl.BlockSpec((1, d), fix)],
        out_specs=[pl.BlockSpec((tm, d), row), pl.BlockSpec((tm, d), row)],
        out_shape=[jax.ShapeDtypeStruct((t, d), F32), jax.ShapeDtypeStruct((t, d), BF16)],
        compiler_params=_cp(("parallel",)),
        name="xattn",
    )(h16, h32, wq, memkv, wo, g.reshape(1, d), b.reshape(1, d))


def _swiglu_step(x, wg_ref, wu_ref, wd_ref, acc_ref):
    gt = _dot(x, wg_ref[0])
    up = _dot(x, wu_ref[0])
    act = (gt * jax.nn.sigmoid(gt) * up).astype(BF16)
    acc_ref[...] += _dot(act, wd_ref[0])


def _ffn_ln_kernel(x_ref, wg_ref, wu_ref, wd_ref, res_ref, g_ref, b_ref, o32_ref, o16_ref, acc_ref, *, alpha):
    f = pl.program_id(1)

    @pl.when(f == 0)
    def _():
        acc_ref[...] = jnp.zeros(acc_ref.shape, F32)

    _swiglu_step(x_ref[...], wg_ref, wu_ref, wd_ref, acc_ref)

    @pl.when(f == pl.num_programs(1) - 1)
    def _():
        out = _layer_norm_rows(alpha * res_ref[...] + acc_ref[...], g_ref[...], b_ref[...])
        o32_ref[...] = out
        o16_ref[...] = out.astype(BF16)


def _ffn_ln(x, wg, wu, wd, res, g, b, *, alpha, tm=512, tf=512):
    n, d = x.shape
    ff = wg.shape[2]
    row = lambda i, f: (i, 0)
    fix = lambda i, f: (0, 0)
    return pl.pallas_call(
        functools.partial(_ffn_ln_kernel, alpha=alpha),
        grid=(n // tm, ff // tf),
        in_specs=[pl.BlockSpec((tm, d), row),
                  pl.BlockSpec((1, d, tf), lambda i, f: (0, 0, f)),
                  pl.BlockSpec((1, d, tf), lambda i, f: (0, 0, f)),
                  pl.BlockSpec((1, tf, d), lambda i, f: (0, f, 0)),
                  pl.BlockSpec((tm, d), row), pl.BlockSpec((1, d), fix), pl.BlockSpec((1, d), fix)],
        out_specs=[pl.BlockSpec((tm, d), row), pl.BlockSpec((tm, d), row)],
        out_shape=[jax.ShapeDtypeStruct((n, d), F32), jax.ShapeDtypeStruct((n, d), BF16)],
        scratch_shapes=[pltpu.VMEM((tm, d), F32)],
        compiler_params=_cp(("parallel", "arbitrary")),
        name="swiglu_ln",
    )(x, wg, wu, wd, res, g.reshape(1, d), b.reshape(1, d))


def _moe_ffn_kernel(be_ref, idx_ref, idx_next_ref, dst_prev_ref, dst_ref, src_ref, wg_ref, wu_ref, wd_ref,
                    y_ref, xbuf, x16, acc_ref, ybuf, sem_in, sem_out):
    i = pl.program_id(0)
    f = pl.program_id(1)
    n_blk = pl.num_programs(0)
    n_f = pl.num_programs(1)
    tm = x16.shape[0]
    per_step = tm // MOE_NF
    slot = i % 2

    def gather_copy(ids, r, s):
        return pltpu.make_async_copy(src_ref.at[pl.ds(ids[0, r], 1)], xbuf.at[s, pl.ds(r, 1)], sem_in.at[s])

    def scatter_copy(ids, r, s):
        return pltpu.make_async_copy(ybuf.at[s, pl.ds(r, 1)], y_ref.at[pl.ds(ids[0, r], 1)], sem_out.at[s])

    def for_rows(fn):
        def body(r, carry):
            fn(r)
            return carry
        lax.fori_loop(0, tm, body, 0)

    def wait_gather(s):
        for _ in range(tm):
            gather_copy(idx_ref, 0, s).wait()

    def wait_scatter(s):
        for _ in range(tm):
            scatter_copy(dst_ref, 0, s).wait()

    @pl.when((i == 0) & (f == 0))
    def _():
        for_rows(lambda r: gather_copy(idx_ref, r, 0).start())
        ybuf[1] = jnp.zeros(ybuf.shape[1:], F32)

    def start_both(r):
        gather_copy(idx_next_ref, r, 1 - slot).start()
        scatter_copy(dst_prev_ref, r, 1 - slot).start()

    @pl.when(f == 0)
    def _():
        wait_gather(slot)
        x16[...] = xbuf[slot].astype(BF16)
        acc_ref[...] = jnp.zeros(acc_ref.shape, F32)
        for r in range(MOE_NF * per_step, tm):
            start_both(r)

    active = i < be_ref[n_blk]

    @pl.when(active)
    def _():
        for k in range(per_step):
            start_both(f * per_step + k)
        _swiglu_step(x16[...], wg_ref, wu_ref, wd_ref, acc_ref)

    @pl.when(jnp.logical_not(active))
    def _():
        for k in range(per_step):
            start_both(f * per_step + k)

    @pl.when(f == n_f - 1)
    def _():
        @pl.when(i >= 1)
        def _():
            wait_scatter(slot)
        ybuf[slot] = acc_ref[...]

    @pl.when((i == n_blk - 1) & (f == n_f - 1))
    def _():
        for_rows(lambda r: scatter_copy(dst_ref, r, slot).start())
        wait_scatter(slot)
        wait_scatter(1 - slot)
        wait_gather(1 - slot)


def _moe_ffn(block_expert, tok_pad, dst_rows, src, wg, wu, wd, *, tm):
    n = tok_pad.shape[0]
    d = src.shape[1]
    ff = wg.shape[2]
    tf = ff // MOE_NF
    assert ff % MOE_NF == 0 and tf % MXU_DIM == 0
    n_blk = n // tm
    assert n_blk >= 2
    ids = tok_pad.reshape(n_blk, 1, tm)
    dst = jnp.concatenate([n + jnp.arange(tm, dtype=I32), dst_rows]).reshape(n_blk + 1, 1, tm)
    smem = lambda fn: pl.BlockSpec((None, 1, tm), fn, memory_space=pltpu.SMEM)
    ft = lambda i, f, be: jnp.where(i < be[n_blk], f, MOE_NF - 1)
    return pl.pallas_call(
        _moe_ffn_kernel,
        grid_spec=pltpu.PrefetchScalarGridSpec(
            num_scalar_prefetch=1, grid=(n_blk, MOE_NF),
            in_specs=[smem(lambda i, f, be: (i, 0, 0)),
                      smem(lambda i, f, be: (jnp.minimum(i + 1, n_blk - 1), 0, 0)),
                      smem(lambda i, f, be: (i, 0, 0)),
                      smem(lambda i, f, be: (i + 1, 0, 0)),
                      pl.BlockSpec(memory_space=pl.ANY),
                      pl.BlockSpec((1, d, tf), lambda i, f, be: (be[i], 0, ft(i, f, be))),
                      pl.BlockSpec((1, d, tf), lambda i, f, be: (be[i], 0, ft(i, f, be))),
                      pl.BlockSpec((1, tf, d), lambda i, f, be: (be[i], ft(i, f, be), 0))],
            out_specs=pl.BlockSpec(memory_space=pl.ANY),
            scratch_shapes=[pltpu.VMEM((2, tm, d), F32), pltpu.VMEM((tm, d), BF16), pltpu.VMEM((tm, d), F32),
                            pltpu.VMEM((2, tm, d), F32),
                            pltpu.SemaphoreType.DMA((2,)), pltpu.SemaphoreType.DMA((2,))]),
        out_shape=jax.ShapeDtypeStruct((n + tm, d), F32),
        compiler_params=_cp(("arbitrary", "arbitrary")),
        name="moe_swiglu",
    )(block_expert, ids, ids, dst, dst, src, wg, wu, wd)


def _router_kernel(x_ref, w_ref, o_ref):
    logits = _dot(x_ref[...], w_ref[...])
    lane = lax.broadcasted_iota(I32, logits.shape, 1)
    lane_f = lane.astype(F32)
    l1 = jnp.where(lane < N_EXPERTS, logits, VERY_NEG)
    m1 = jnp.max(l1, axis=1, keepdims=True)
    i1 = jnp.min(jnp.where(l1 == m1, lane_f, float(LANES)), axis=1, keepdims=True)
    l2 = jnp.where(lane_f == i1, VERY_NEG, l1)
    m2 = jnp.max(l2, axis=1, keepdims=True)
    i2 = jnp.min(jnp.where(l2 == m2, lane_f, float(LANES)), axis=1, keepdims=True)
    e = jnp.exp(m2 - m1)
    g1 = 1.0 / (1.0 + e)
    g2 = e / (1.0 + e)
    o_ref[...] = jnp.where(lane == 0, i1, jnp.where(lane == 1, i2, jnp.where(lane == 2, g1,
                           jnp.where(lane == 3, g2, 0.0))))


def _router(h16, wr, *, tm=512):
    t, d = h16.shape
    return pl.pallas_call(
        _router_kernel,
        grid=(t // tm,),
        in_specs=[pl.BlockSpec((tm, d), lambda i: (i, 0)), pl.BlockSpec((d, LANES), lambda i: (0, 0))],
        out_specs=pl.BlockSpec((tm, LANES), lambda i: (i, 0)),
        out_shape=jax.ShapeDtypeStruct((t, LANES), F32),
        compiler_params=_cp(("parallel",)),
        name="router",
    )(h16, wr)


def _combine_kernel(y0_ref, y1_ref, rt_ref, res_ref, g_ref, b_ref, o_ref, *, alpha):
    f = y0_ref[...] * rt_ref[:, 2:3] + y1_ref[...] * rt_ref[:, 3:4]
    o_ref[...] = _layer_norm_rows(alpha * res_ref[...] + f, g_ref[...], b_ref[...])


def _combine_ln(y, rt, res, g, b, *, alpha, tm=256):
    t, d = res.shape
    per_k = t // tm
    row = lambda i: (i, 0)
    fix = lambda i: (0, 0)
    return pl.pallas_call(
        functools.partial(_combine_kernel, alpha=alpha),
        grid=(t // tm,),
        in_specs=[pl.BlockSpec((tm, d), row), pl.BlockSpec((tm, d), lambda i: (per_k + i, 0)),
                  pl.BlockSpec((tm, LANES), row), pl.BlockSpec((tm, d), row),
                  pl.BlockSpec((1, d), fix), pl.BlockSpec((1, d), fix)],
        out_specs=pl.BlockSpec((tm, d), row),
        out_shape=jax.ShapeDtypeStruct((t, d), F32),
        compiler_params=_cp(("parallel",)),
        name="moe_combine_ln",
    )(y, y, rt, res, g.reshape(1, d), b.reshape(1, d))


def _moe(h16, h32, router, w_gate, w_up, w_down, g, b, *, alpha):
    t, d = h32.shape
    tm = MOE_TM
    wr = jnp.pad(router, ((0, 0), (0, LANES - N_EXPERTS))).astype(BF16)
    rt = _router(h16, wr)
    expert = rt[:, 0:TOP_K].astype(I32).reshape(-1)
    n_assign = t * TOP_K
    onehot = (expert[:, None] == jnp.arange(N_EXPERTS, dtype=I32)[None, :]).astype(I32)
    csum = jnp.cumsum(onehot, axis=0)
    rank = jnp.sum(csum * onehot, axis=1) - 1
    counts = csum[-1]
    padded = (counts + tm - 1) // tm * tm
    pad_end = jnp.cumsum(padded)
    pad_start = pad_end - padded
    dest = (pad_start[expert] + rank).astype(I32)
    n_pad = n_assign + N_EXPERTS * tm
    n_blocks = n_pad // tm
    block_expert = jnp.minimum(
        jnp.searchsorted(pad_end, jnp.arange(n_blocks, dtype=I32) * tm, side="right"),
        N_EXPERTS - 1).astype(I32)
    block_expert = jnp.concatenate([block_expert, (pad_end[-1:] // tm).astype(I32)])
    assign = jnp.arange(n_assign, dtype=I32)
    token = assign // TOP_K
    info = jnp.stack([token, (assign % TOP_K) * t + token], axis=1)
    info = jnp.full((n_pad, 2), -1, I32).at[dest].set(info)
    is_pad = info[:, 0] < 0
    tok_pad = jnp.where(is_pad, 0, info[:, 0])
    dst_rows = jnp.where(is_pad, n_assign + jnp.cumsum(is_pad.astype(I32)) - 1, info[:, 1]).astype(I32)
    y = _moe_ffn(block_expert, tok_pad, dst_rows, h32, w_gate, w_up, w_down, tm=tm)
    return _combine_ln(y, rt, h32, g, b, alpha=alpha)


def _perm_w_in(w, d_model):
    sizes = ((NSA_WIDTH,) + (NSA_KV_WIDTH,) * 6 + (N_BRANCH * NSA_HEADS,)
             + (RW_WIDTH, RW_WIDTH, RW_WIDTH, RW_DECAY_LORA, RW_A_LORA, RW_GATE_LORA)
             + (POOL_WIDTH,) + (d_model,) * N_BRANCH)
    offs = np.concatenate([[0], np.cumsum(sizes)])
    seg = [w[:, int(offs[i]):int(offs[i + 1])] for i in range(len(sizes))]
    (q, kc, vc, ks, vs, kw, vw, ng, r, k, v, wl, al, gl, pool, ga, gb, gc) = seg
    zeros = lambda n: jnp.zeros((w.shape[0], n), w.dtype)
    cols = [ga, gb, gc, q, r, k, v, pool, kc, vc, ks, vs, kw, vw,
            ng, zeros(SM_WL - N_BRANCH * NSA_HEADS), wl, al, gl, zeros(SMALL_W - SM_GL - RW_GATE_LORA)]
    out = jnp.concatenate(cols, axis=1).astype(BF16)
    assert out.shape[1] == NP_COLS and d_model == 2048
    return out


def _rwkv_params(l, mu, w0, w2, a0, a2, g2, k_k, k_a, r_k, lnx_g, lnx_b, v0, v1, v2):
    w = RW_WIDTH
    rowv = lambda a: a.reshape(1, w).astype(F32)
    m = mu[l]
    o = np.cumsum((0, w, w, w, RW_DECAY_LORA, RW_A_LORA, RW_GATE_LORA))
    mu_sm = jnp.zeros((1, SMALL_W), F32)
    mu_sm = mu_sm.at[0, SM_WL:SM_WL + RW_DECAY_LORA].set(m[o[3]:o[4]])
    mu_sm = mu_sm.at[0, SM_AL:SM_AL + RW_A_LORA].set(m[o[4]:o[5]])
    mu_sm = mu_sm.at[0, SM_GL:SM_GL + RW_GATE_LORA].set(m[o[5]:o[6]])

    def pad_rows(a, off):
        return jnp.zeros((SMALL_W, w), F32).at[off:off + a.shape[0]].set(a).astype(BF16)

    p = dict(mu=m[0:3 * w].reshape(3, w), mu_sm=mu_sm, w0=rowv(w0[l]), w2=pad_rows(w2[l], SM_WL),
             a0=rowv(a0[l]), a2=pad_rows(a2[l], SM_AL), g2=pad_rows(g2[l], SM_GL),
             k_k=rowv(k_k[l]), k_a=rowv(k_a[l]), r_k=rowv(r_k[l]), lnx_g=rowv(lnx_g[l]), lnx_b=rowv(lnx_b[l]))
    if l > 0:
        p["v0"] = rowv(v0[l - 1])
        p["v1"] = jnp.pad(v1[l - 1], ((0, 0), (0, LANES - RW_V_LORA))).astype(BF16)
        p["v2"] = jnp.pad(v2[l - 1], ((0, LANES - RW_V_LORA), (0, 0))).astype(BF16)
    return p


def _stride_units(x2d, b, s):
    x = x2d.reshape(b, s // CMP_STRIDE, CMP_STRIDE, NSA_KV_HEADS, HEAD_DIM)
    return x.transpose(0, 3, 1, 2, 4).reshape(b, NSA_KV_HEADS, s // CMP_STRIDE, CMP_STRIDE * HEAD_DIM)


def _token_mixer(l, h16, proj, b, s, tabs, tabs_c, prm, vfirst):
    d_model = h16.shape[1]
    t = b * s
    proj3 = proj.reshape(b, s, NP_COLS)
    c, shi, slo = tabs
    q_t, ksa, vs_t, kwh, vw_t, g_t = _nsa_prep(proj3, c, shi, slo)
    n_units = s // CMP_STRIDE
    half = CMP_STRIDE * HEAD_DIM
    kvc = proj[:, OFF_KVC:OFF_KVC + 2 * NSA_KV_WIDTH]
    cmp_out = []
    for j in range(2):
        xu = _stride_units(kvc[:, NSA_KV_WIDTH * j:NSA_KV_WIDTH * (j + 1)], b, s)
        pe = prm["nsa_cmp_pe"][l, j].reshape(2, half)
        w1 = prm["nsa_cmp_w1"][l, j].astype(BF16)
        b1 = prm["nsa_cmp_b1"][l, j].reshape(1, CMP_HIDDEN)
        w2p = jnp.pad(prm["nsa_cmp_w2"][l, j], ((0, 0), (0, LANES - HEAD_DIM))).astype(BF16)
        cmp_out.append(_compress(xu, pe, w1, b1, w2p, *tabs_c, is_key=(j == 0)))
    kc, vc_t = cmp_out
    y_a = _nsa_attention(q_t, g_t, kc, vc_t, ksa, vs_t, kwh, vw_t).reshape(t, NSA_WIDTH)
    rp = _rwkv_params(l, prm["rwkv_mu"], prm["rwkv_w0"], prm["rwkv_w2"], prm["rwkv_a0"], prm["rwkv_a2"],
                      prm["rwkv_g2"], prm["rwkv_k_k"], prm["rwkv_k_a"], prm["rwkv_r_k"], prm["rwkv_lnx_g"],
                      prm["rwkv_lnx_b"], prm["rwkv_v0"], prm["rwkv_v1"], prm["rwkv_v2"])
    y_b, v_l = _rwkv(proj3, vfirst, rp, has_vres=(l > 0))
    y_c = _pool(proj3, prm["pool_w"][l].astype(BF16), prm["pool_scale"][l])
    merged = _merge(y_a, y_b.reshape(t, RW_WIDTH), y_c.reshape(t, POOL_WIDTH), proj,
                    prm["w_branch"][l].astype(BF16))
    return merged, v_l


def kernel(x, mem, positions, w_in, nsa_cmp_pe, nsa_cmp_w1, nsa_cmp_b1, nsa_cmp_w2, rwkv_mu, rwkv_w0, rwkv_w2, rwkv_a0, rwkv_a2, rwkv_g2, rwkv_k_k, rwkv_k_a, rwkv_r_k, rwkv_lnx_g, rwkv_lnx_b, rwkv_v0, rwkv_v1, rwkv_v2, pool_w, pool_scale, w_branch, w_out, mem_ln_g, mem_ln_b, mem_wkv, xa_wq, xa_wo, ln_g, ln_b, ffn_w_gate, ffn_w_up, ffn_w_down, moe_router, moe_w_gate, moe_w_up, moe_w_down):
    prm = dict(nsa_cmp_pe=nsa_cmp_pe, nsa_cmp_w1=nsa_cmp_w1, nsa_cmp_b1=nsa_cmp_b1, nsa_cmp_w2=nsa_cmp_w2,
               rwkv_mu=rwkv_mu, rwkv_w0=rwkv_w0, rwkv_w2=rwkv_w2, rwkv_a0=rwkv_a0, rwkv_a2=rwkv_a2,
               rwkv_g2=rwkv_g2, rwkv_k_k=rwkv_k_k, rwkv_k_a=rwkv_k_a, rwkv_r_k=rwkv_r_k,
               rwkv_lnx_g=rwkv_lnx_g, rwkv_lnx_b=rwkv_lnx_b, rwkv_v0=rwkv_v0, rwkv_v1=rwkv_v1,
               rwkv_v2=rwkv_v2, pool_w=pool_w, pool_scale=pool_scale, w_branch=w_branch)
    b, s, d = x.shape
    t = b * s
    depth = w_in.shape[0]
    alpha = (2 * depth) ** 0.25
    n_mem = mem.shape[1]

    inv_freq = ROPE_THETA ** (-jnp.arange(ROPE_HALF, dtype=F32) / ROPE_HALF)
    dd = np.arange(LANES) % HEAD_DIM
    invf_lane = jnp.where(jnp.asarray(dd < ROPE_DIM), inv_freq[jnp.asarray(dd % ROPE_HALF)], 0.0).reshape(1, LANES)
    tabs = tuple(a.reshape(b, s, LANES) for a in _rope_tables(positions.reshape(t, 1), invf_lane, ts=512))
    n_units = s // CMP_STRIDE
    pos_c = positions[:, CMP_BLOCK - 1::CMP_STRIDE]
    pos_c = jnp.concatenate([pos_c, pos_c[:, -1:]], axis=1)
    tabs_c = tuple(a.reshape(b, n_units, LANES)
                   for a in _rope_tables(pos_c.reshape(b * n_units, 1), invf_lane, ts=n_units))

    memkv = _memkv(mem.reshape(b * n_mem, d), mem_ln_g, mem_ln_b, mem_wkv.astype(BF16))
    memkv = memkv.reshape(b, n_mem, 2 * XA_WIDTH)

    h32 = x.reshape(t, d)
    h16 = h32.astype(BF16)
    vfirst = None
    for l in range(depth):
        proj = _matmul(h16, _perm_w_in(w_in[l], d), tm=512, tn=1024, out_dtype=BF16)
        merged, v_l = _token_mixer(l, h16, proj, b, s, tabs, tabs_c, prm, vfirst)
        if l == 0:
            vfirst = v_l
        h32, h16 = _matmul_res_ln(merged, w_out[l].astype(BF16), h32, ln_g[l, 0], ln_b[l, 0], alpha=alpha)
        h32, h16 = _xattn(h16, h32, xa_wq[l].astype(BF16), memkv, xa_wo[l].astype(BF16),
                          ln_g[l, 1], ln_b[l, 1], alpha=alpha, seq=s)
        if l % 2 == 0:
            e = l // 2
            h32, h16 = _ffn_ln(h16, ffn_w_gate[e:e + 1].astype(BF16), ffn_w_up[e:e + 1].astype(BF16),
                               ffn_w_down[e:e + 1].astype(BF16), h32, ln_g[l, 2], ln_b[l, 2], alpha=alpha)
        else:
            e = l // 2
            h32 = _moe(h16, h32, moe_router[e], moe_w_gate[e].astype(BF16), moe_w_up[e].astype(BF16),
                       moe_w_down[e].astype(BF16), ln_g[l, 2], ln_b[l, 2], alpha=alpha)
            h16 = h32.astype(BF16)
    return h32.reshape(b, s, d)
```

```python
import functools

import numpy as np
import jax
import jax.numpy as jnp
from jax import lax
from jax.experimental import pallas as pl
from jax.experimental.pallas import tpu as pltpu

F32 = jnp.float32
BF16 = jnp.bfloat16
I32 = jnp.int32

HEAD_DIM = 64
ROPE_DIM = HEAD_DIM // 4
ROPE_HALF = ROPE_DIM // 2
ROPE_THETA = 500000.0
NSA_HEADS = 16
NSA_KV_HEADS = 4
NSA_GROUP = NSA_HEADS // NSA_KV_HEADS
NSA_WIDTH = NSA_HEADS * HEAD_DIM
NSA_KV_WIDTH = NSA_KV_HEADS * HEAD_DIM
CMP_BLOCK = 32
CMP_STRIDE = 16
CMP_HIDDEN = 256
SLC_BLOCK = 64
SLC_TOPN = 8
WINDOW = 512
SEL_FORCE = 1e9
RW_HEADS = 16
RW_HEAD = 64
RW_WIDTH = RW_HEADS * RW_HEAD
RW_DECAY_LORA = 64
RW_A_LORA = 64
RW_V_LORA = 32
RW_GATE_LORA = 160
RW_LNX_EPS = 64e-5
POOL_WINDOWS = (2, 4, 8, 16)
POOL_GROUP = 256
POOL_WIDTH = POOL_GROUP * len(POOL_WINDOWS)
XA_HEADS = 4
XA_HEAD_DIM = 128
XA_WIDTH = XA_HEADS * XA_HEAD_DIM
N_EXPERTS = 8
TOP_K = 2
N_BRANCH = 3
LN_EPS = 1e-5
NEG_INF = -1e30
VERY_NEG = -3e38
MASK_BIG = 2.0 ** 100
LOG2_E = 1.4426950408889634

LANES = 128
MXU_DIM = 256
VMEM_LIMIT = 56 * 1024 * 1024

NP_COLS = 13312
OFF_GATES = 0
OFF_Q = 6144
OFF_R = 7168
OFF_K = 8192
OFF_V = 9216
OFF_POOL = 10240
OFF_KVC = 11264
OFF_KVS = 11776
OFF_KVW = 12288
OFF_SMALL = 12800
SMALL_W = 512
SM_WL = 128
SM_AL = 192
SM_GL = 256

NSA_TQ = 256
NSA_TK = 256
ONES_ROWS = 16
RW_CHUNK = 64
RW_TILE = 128
MOE_TM = 512
MOE_NF = 7


def _cp(sem, vmem=VMEM_LIMIT):
    return pltpu.CompilerParams(dimension_semantics=sem, vmem_limit_bytes=vmem)


def _dot(a, b):
    return jnp.dot(a, b, preferred_element_type=F32)


def _dot_nt(a, b):
    return lax.dot_general(a, b, (((1,), (1,)), ((), ())), preferred_element_type=F32)


def _layer_norm_rows(y, g, b):
    mu = jnp.mean(y, axis=-1, keepdims=True)
    d = y - mu
    var = jnp.mean(d * d, axis=-1, keepdims=True)
    return d * lax.rsqrt(var + LN_EPS) * g + b


def _mm_kernel(x_ref, w_ref, o_ref):
    o_ref[...] = _dot(x_ref[...], w_ref[...]).astype(o_ref.dtype)


def _matmul(x, w, *, tm, tn, out_dtype):
    m, k = x.shape
    n = w.shape[1]
    return pl.pallas_call(
        _mm_kernel,
        grid=(n // tn, m // tm),
        in_specs=[pl.BlockSpec((tm, k), lambda j, i: (i, 0)),
                  pl.BlockSpec((k, tn), lambda j, i: (0, j))],
        out_specs=pl.BlockSpec((tm, tn), lambda j, i: (i, j)),
        out_shape=jax.ShapeDtypeStruct((m, n), out_dtype),
        compiler_params=_cp(("parallel", "parallel")),
        name="matmul",
    )(x, w)


def _mm_ln_kernel(x_ref, w_ref, res_ref, g_ref, b_ref, o32_ref, o16_ref, *, alpha):
    y = alpha * res_ref[...] + _dot(x_ref[...], w_ref[...])
    out = _layer_norm_rows(y, g_ref[...], b_ref[...])
    o32_ref[...] = out
    o16_ref[...] = out.astype(BF16)


def _matmul_res_ln(x, w, res, g, b, *, alpha, tm=256):
    m, k = x.shape
    n = w.shape[1]
    row = lambda i: (i, 0)
    fix = lambda i: (0, 0)
    return pl.pallas_call(
        functools.partial(_mm_ln_kernel, alpha=alpha),
        grid=(m // tm,),
        in_specs=[pl.BlockSpec((tm, k), row), pl.BlockSpec((k, n), fix), pl.BlockSpec((tm, n), row),
                  pl.BlockSpec((1, n), fix), pl.BlockSpec((1, n), fix)],
        out_specs=[pl.BlockSpec((tm, n), row), pl.BlockSpec((tm, n), row)],
        out_shape=[jax.ShapeDtypeStruct((m, n), F32), jax.ShapeDtypeStruct((m, n), BF16)],
        compiler_params=_cp(("parallel",)),
        name="matmul_res_ln",
    )(x, w, res, g.reshape(1, n), b.reshape(1, n))


def _memkv_kernel(m_ref, g_ref, b_ref, w_ref, o_ref):
    xn = _layer_norm_rows(m_ref[...], g_ref[...], b_ref[...])
    o_ref[...] = _dot(xn.astype(BF16), w_ref[...]).astype(o_ref.dtype)


def _memkv(mem2d, g, b, w, *, tm=256):
    m, d = mem2d.shape
    n = w.shape[1]
    row = lambda i: (i, 0)
    fix = lambda i: (0, 0)
    return pl.pallas_call(
        _memkv_kernel,
        grid=(m // tm,),
        in_specs=[pl.BlockSpec((tm, d), row), pl.BlockSpec((1, d), fix), pl.BlockSpec((1, d), fix),
                  pl.BlockSpec((d, n), fix)],
        out_specs=pl.BlockSpec((tm, n), row),
        out_shape=jax.ShapeDtypeStruct((m, n), BF16),
        compiler_params=_cp(("parallel",)),
        name="memkv",
    )(mem2d, g.reshape(1, d), b.reshape(1, d), w)


def _rope_table_kernel(pos_ref, invf_ref, c_ref, shi_ref, slo_ref):
    ang = pos_ref[...].astype(F32) * invf_ref[...]
    d = lax.broadcasted_iota(I32, ang.shape, 1) % HEAD_DIM
    cs = jnp.cos(ang)
    sn = jnp.sin(ang)
    c_ref[...] = jnp.where(d < ROPE_DIM, cs, 1.0)
    shi_ref[...] = jnp.where((d >= ROPE_HALF) & (d < ROPE_DIM), sn, 0.0)
    slo_ref[...] = jnp.where(d < ROPE_HALF, -sn, 0.0)


def _rope_tables(pos_col, invf_lane, *, ts):
    n = pos_col.shape[0]
    row = lambda i: (i, 0)
    out = jax.ShapeDtypeStruct((n, LANES), F32)
    return pl.pallas_call(
        _rope_table_kernel,
        grid=(n // ts,),
        in_specs=[pl.BlockSpec((ts, 1), row), pl.BlockSpec((1, LANES), lambda i: (0, 0))],
        out_specs=[pl.BlockSpec((ts, LANES), row)] * 3,
        out_shape=[out, out, out],
        compiler_params=_cp(("parallel",)),
        name="rope_tables",
    )(pos_col, invf_lane)


def _rope128(x, c, shi, slo):
    return x * c + pltpu.roll(x, ROPE_HALF, 1) * shi + pltpu.roll(x, LANES - ROPE_HALF, 1) * slo


def _nsa_prep_kernel(q_ref, ks_ref, vs_ref, kw_ref, vw_ref, sm_ref, c_ref, shi_ref, slo_ref,
                     qt_ref, ksa_ref, vst_ref, kwh_ref, vwt_ref, gt_ref):
    c = c_ref[...]
    shi = shi_ref[...]
    slo = slo_ref[...]
    ts = c.shape[0]
    scale = HEAD_DIM ** -0.5 * LOG2_E
    for j in range(NSA_WIDTH // LANES):
        sl = slice(LANES * j, LANES * (j + 1))
        qt_ref[sl, :] = (_rope128(q_ref[:, sl].astype(F32), c, shi, slo) * scale).T.astype(BF16)
    t = pl.program_id(1) * ts + lax.broadcasted_iota(I32, (ts, LANES), 0)
    lane = lax.broadcasted_iota(I32, (ts, LANES), 1)
    onehot = jnp.where(lane - HEAD_DIM == t // SLC_BLOCK, 1.0, 0.0)
    lo = lane < HEAD_DIM
    for j in range(NSA_KV_WIDTH // LANES):
        sl = slice(LANES * j, LANES * (j + 1))
        ks = _rope128(ks_ref[:, sl].astype(F32), c, shi, slo)
        kw = _rope128(kw_ref[:, sl].astype(F32), c, shi, slo).astype(BF16)
        ksa_ref[2 * j] = jnp.where(lo, ks, onehot).astype(BF16)
        ksa_ref[2 * j + 1] = jnp.where(lo, pltpu.roll(ks, HEAD_DIM, 1), onehot).astype(BF16)
        kwh_ref[2 * j] = kw[:, 0:HEAD_DIM]
        kwh_ref[2 * j + 1] = kw[:, HEAD_DIM:LANES]
    for k in range(ts // LANES):
        rs = slice(LANES * k, LANES * (k + 1))
        vst_ref[k] = vs_ref[rs, :].astype(F32).T.astype(BF16)
        vwt_ref[k] = vw_ref[rs, :].astype(F32).T.astype(BF16)
    gt_ref[...] = jax.nn.sigmoid(sm_ref[:, 0:LANES].astype(F32)).T


def _nsa_prep(proj3, c, shi, slo, *, ts=512):
    b, s, _ = proj3.shape
    kvw = NSA_KV_WIDTH
    tab = pl.BlockSpec((None, ts, LANES), lambda bi, i: (bi, i, 0))
    kv = lambda off: pl.BlockSpec((None, ts, kvw), lambda bi, i: (bi, i, off // kvw))
    vt_spec = pl.BlockSpec((None, ts // LANES, kvw, LANES), lambda bi, i: (bi, i, 0, 0))
    vt_shape = jax.ShapeDtypeStruct((b, s // LANES, kvw, LANES), BF16)
    return pl.pallas_call(
        _nsa_prep_kernel,
        grid=(b, s // ts),
        in_specs=[
            pl.BlockSpec((None, ts, NSA_WIDTH), lambda bi, i: (bi, i, OFF_Q // NSA_WIDTH)),
            kv(OFF_KVS), kv(OFF_KVS + kvw), kv(OFF_KVW), kv(OFF_KVW + kvw),
            pl.BlockSpec((None, ts, SMALL_W), lambda bi, i: (bi, i, OFF_SMALL // SMALL_W)),
            tab, tab, tab],
        out_specs=[
            pl.BlockSpec((None, NSA_WIDTH, ts), lambda bi, i: (bi, 0, i)),
            pl.BlockSpec((None, NSA_KV_HEADS, ts, LANES), lambda bi, i: (bi, 0, i, 0)),
            vt_spec,
            pl.BlockSpec((None, NSA_KV_HEADS, ts, HEAD_DIM), lambda bi, i: (bi, 0, i, 0)),
            vt_spec,
            pl.BlockSpec((None, LANES, ts), lambda bi, i: (bi, 0, i))],
        out_shape=[jax.ShapeDtypeStruct((b, NSA_WIDTH, s), BF16),
                   jax.ShapeDtypeStruct((b, NSA_KV_HEADS, s, LANES), BF16),
                   vt_shape,
                   jax.ShapeDtypeStruct((b, NSA_KV_HEADS, s, HEAD_DIM), BF16),
                   vt_shape,
                   jax.ShapeDtypeStruct((b, LANES, s), F32)],
        compiler_params=_cp(("parallel", "parallel")),
        name="nsa_prep",
    )(proj3, proj3, proj3, proj3, proj3, proj3, c, shi, slo)


def _gelu_tanh(x):
    return 0.5 * x * (1.0 + jnp.tanh(0.7978845608028654 * (x + 0.044715 * x * x * x)))


def _cmp_kernel(x_ref, pe_ref, w1_ref, b1_ref, w2_ref, c_ref, shi_ref, slo_ref, o_ref, *, is_key):
    x = x_ref[...].astype(F32)
    n = x.shape[0]
    half = x.shape[1]
    xa = (x + pe_ref[0:1, :]).astype(BF16)
    xb = (x + pe_ref[1:2, :]).astype(BF16)
    ha = _dot(xa, w1_ref[0:half, :])
    hb = _dot(xb, w1_ref[half:2 * half, :])
    hid = ha + pltpu.roll(hb, n - 1, 0) + b1_ref[...]
    out = _dot(_gelu_tanh(hid).astype(BF16), w2_ref[...])
    if is_key:
        out = _rope128(out, c_ref[...], shi_ref[...], slo_ref[...])
        o_ref[...] = out[:, 0:HEAD_DIM].astype(BF16)
    else:
        o_ref[...] = out.T[0:HEAD_DIM, :].astype(BF16)


def _compress(xu, pe, w1, b1, w2p, c, shi, slo, *, is_key):
    b, nh, n, w = xu.shape
    fix2 = lambda bi, h: (0, 0)
    tab = pl.BlockSpec((None, n, LANES), lambda bi, h: (bi, 0, 0))
    oshape = (n, HEAD_DIM) if is_key else (HEAD_DIM, n)
    return pl.pallas_call(
        functools.partial(_cmp_kernel, is_key=is_key),
        grid=(b, nh),
        in_specs=[pl.BlockSpec((None, None, n, w), lambda bi, h: (bi, h, 0, 0)),
                  pl.BlockSpec((2, w), fix2), pl.BlockSpec((2 * w, CMP_HIDDEN), fix2),
                  pl.BlockSpec((1, CMP_HIDDEN), fix2), pl.BlockSpec((CMP_HIDDEN, LANES), fix2),
                  tab, tab, tab],
        out_specs=pl.BlockSpec((None, None) + oshape, lambda bi, h: (bi, h, 0, 0)),
        out_shape=jax.ShapeDtypeStruct((b, nh) + oshape, BF16),
        compiler_params=_cp(("parallel", "parallel")),
        name="nsa_compress",
    )(xu, pe, w1, b1, w2p, c, shi, slo)


def _nsa_kernel(qt_ref, gt_ref, kc_ref, vct_ref, mt_ref, ksa_ref, vst_ref, kwh_ref, vwt_ref, o_ref):
    tq = qt_ref.shape[1]
    cols = NSA_GROUP * tq
    n_cmp = kc_ref.shape[1]
    n_slc = mt_ref.shape[0]
    i = pl.program_id(1)
    t0 = i * tq
    loc = lax.broadcasted_iota(I32, (1, tq), 1)
    loc_c = jnp.concatenate([loc] * NSA_GROUP, axis=1)
    tok = t0 + loc
    tok_c = t0 + loc_c
    cmp_end = lax.broadcasted_iota(I32, (n_cmp, 1), 0) * CMP_STRIDE + (CMP_BLOCK - 1)
    cmp_mask = cmp_end <= tok_c
    blk = lax.broadcasted_iota(I32, (n_slc, 1), 0)
    blk_f = blk.astype(F32)
    cur = tok // SLC_BLOCK
    forced = (blk == 0) | (blk == cur) | (blk == cur - 1)
    allowed = blk <= cur
    tk = NSA_TK
    n_full = t0 // tk
    last_mask = (n_full * tk + lax.broadcasted_iota(I32, (tk, 1), 0)) <= tok_c
    n_win = WINDOW + tq
    ws = pl.multiple_of(jnp.maximum(t0 - WINDOW, 0), tq)
    wt0 = ws // LANES
    dwin = tok_c - (ws + lax.broadcasted_iota(I32, (n_win, 1), 0))
    win_mask = (dwin >= 0) & (dwin < WINDOW)
    cmp_to_slc_t = mt_ref[...]
    ones_rows = jnp.ones((ONES_ROWS, n_win), BF16)

    kvh = range(NSA_KV_HEADS)
    hss = [slice(HEAD_DIM * h, HEAD_DIM * (h + 1)) for h in kvh]
    qhs = [jnp.concatenate(
        [qt_ref[HEAD_DIM * (NSA_GROUP * h + g):HEAD_DIM * (NSA_GROUP * h + g + 1), :]
         for g in range(NSA_GROUP)], axis=1) for h in kvh]
    ss = [jnp.where(cmp_mask, _dot(kc_ref[h], qhs[h]), NEG_INF) for h in kvh]
    es = [jnp.where(cmp_mask, jnp.exp2(s - jnp.max(s, axis=0, keepdims=True)), 0.0) for s in ss]
    ps = [e / jnp.maximum(jnp.sum(e, axis=0, keepdims=True), 1e-30) for e in es]
    o_cs = [_dot(vct_ref[h], ps[h].astype(BF16)) for h in kvh]
    p4s = [sum(p[:, g * tq:(g + 1) * tq] for g in range(1, NSA_GROUP)) + p[:, 0:tq] for p in ps]
    p_his = [p4.astype(BF16) for p4 in p4s]
    p_los = [(p4 - p_hi.astype(F32)).astype(BF16) for p4, p_hi in zip(p4s, p_his)]
    imps = [_dot(cmp_to_slc_t, p_hi) + _dot(cmp_to_slc_t, p_lo) for p_hi, p_lo in zip(p_his, p_los)]
    imps = [jnp.where(forced, SEL_FORCE, jnp.where(allowed, imp, -SEL_FORCE)) for imp in imps]
    sels = [jnp.zeros((n_slc, tq), F32) for _ in kvh]
    for _ in range(SLC_TOPN):
        bests = [jnp.max(imp, axis=0, keepdims=True) for imp in imps]
        idxs = [jnp.min(jnp.where(imp == best, blk_f, float(n_slc)), axis=0, keepdims=True)
                for imp, best in zip(imps, bests)]
        hits = [blk_f == idx for idx in idxs]
        sels = [jnp.where(hit, 1.0, sel) for hit, sel in zip(hits, sels)]
        imps = [jnp.where(hit, VERY_NEG, imp) for hit, imp in zip(hits, imps)]
    biases = [jnp.where(sel > 0.5, 0.0, -MASK_BIG).astype(BF16) for sel in sels]
    q_augs = [jnp.concatenate([qh, jnp.concatenate([bias] * NSA_GROUP, axis=1)], axis=0)
              for qh, bias in zip(qhs, biases)]

    def sel_step(j, carry, masked):
        k0 = pl.multiple_of(j * tk, tk)
        scs = [_dot(ksa_ref[h, pl.ds(k0, tk), :], q_augs[h]) for h in kvh]
        if masked:
            scs = [jnp.where(last_mask, sc, NEG_INF) for sc in scs]
        m_news = [jnp.maximum(carry[h][0], jnp.max(scs[h], axis=0, keepdims=True)) for h in kvh]
        alphas = [jnp.exp2(carry[h][0] - m_news[h]) for h in kvh]
        pps = [jnp.exp2(scs[h] - m_news[h]).astype(BF16) for h in kvh]
        vts = [jnp.concatenate([vst_ref[j * (tk // LANES) + d, hss[h], :] for d in range(tk // LANES)], axis=1)
               for h in kvh]
        vts = [jnp.concatenate([vt, ones_rows[:, 0:tk]], axis=0) for vt in vts]
        accs = [alphas[h] * carry[h][1] + _dot(vts[h], pps[h]) for h in kvh]
        return tuple((m_news[h], accs[h]) for h in kvh)

    carry = tuple((jnp.full((1, cols), VERY_NEG, F32), jnp.zeros((HEAD_DIM + ONES_ROWS, cols), F32))
                  for _ in kvh)
    carry = lax.fori_loop(0, n_full, lambda j, cr: sel_step(j, cr, False), carry)
    carry = sel_step(n_full, carry, True)
    o_ss = [acc[0:HEAD_DIM] / acc[HEAD_DIM:HEAD_DIM + 1] for _, acc in carry]
    sws = [jnp.where(win_mask, _dot(kwh_ref[h, pl.ds(ws, n_win), :], qhs[h]), NEG_INF) for h in kvh]
    ews = [jnp.exp2(sw - jnp.max(sw, axis=0, keepdims=True)).astype(BF16) for sw in sws]
    vws = [jnp.concatenate([vwt_ref[wt0 + d, hss[h], :] for d in range(n_win // LANES)], axis=1) for h in kvh]
    vws = [jnp.concatenate([vw, ones_rows], axis=0) for vw in vws]
    a_ws = [_dot(vw, ew) for vw, ew in zip(vws, ews)]
    o_ws = [a[0:HEAD_DIM] / a[HEAD_DIM:HEAD_DIM + 1] for a in a_ws]
    out_rows = []
    for h in kvh:
        for g in range(NSA_GROUP):
            head = NSA_GROUP * h + g
            cs = slice(g * tq, (g + 1) * tq)
            out_rows.append(gt_ref[head:head + 1, :] * o_cs[h][:, cs]
                            + gt_ref[NSA_HEADS + head:NSA_HEADS + head + 1, :] * o_ss[h][:, cs]
                            + gt_ref[2 * NSA_HEADS + head:2 * NSA_HEADS + head + 1, :] * o_ws[h][:, cs])
    o_ref[...] = jnp.concatenate(out_rows, axis=0).T.astype(BF16)


def _cmp_slc_weights(n_cmp_pad, n_cmp, n_slc_pad):
    c0 = np.arange(n_cmp_pad)[:, None] * CMP_STRIDE
    s0 = np.arange(n_slc_pad)[None, :] * SLC_BLOCK
    shared = np.minimum(c0 + CMP_BLOCK, s0 + SLC_BLOCK) - np.maximum(c0, s0)
    w = np.maximum(shared, 0) / CMP_STRIDE
    w[n_cmp:] = 0.0
    return w.astype(np.float32)


def _nsa_attention(q_t, g_t, kc, vc_t, ksa, vs_t, kwh, vw_t):
    b, _, s = q_t.shape
    tq = NSA_TQ
    n_cmp_pad = kc.shape[2]
    n_slc = HEAD_DIM
    assert SLC_TOPN <= s // SLC_BLOCK <= n_slc and s >= WINDOW + tq and tq % LANES == 0
    assert s % NSA_TK == 0 and NSA_TK % tq == 0 and WINDOW % tq == 0
    m_t = jnp.asarray(_cmp_slc_weights(n_cmp_pad, s // CMP_STRIDE - 1, n_slc).T, BF16)
    per_b4 = lambda bi, i: (bi, 0, 0, 0)
    return pl.pallas_call(
        _nsa_kernel,
        grid=(b, s // tq),
        in_specs=[
            pl.BlockSpec((None, NSA_WIDTH, tq), lambda bi, i: (bi, 0, i)),
            pl.BlockSpec((None, LANES, tq), lambda bi, i: (bi, 0, i)),
            pl.BlockSpec((None, NSA_KV_HEADS, n_cmp_pad, HEAD_DIM), per_b4),
            pl.BlockSpec((None, NSA_KV_HEADS, HEAD_DIM, n_cmp_pad), per_b4),
            pl.BlockSpec((n_slc, n_cmp_pad), lambda bi, i: (0, 0)),
            pl.BlockSpec((None, NSA_KV_HEADS, s, LANES), per_b4),
            pl.BlockSpec((None, s // LANES, NSA_KV_WIDTH, LANES), per_b4),
            pl.BlockSpec((None, NSA_KV_HEADS, s, HEAD_DIM), per_b4),
            pl.BlockSpec((None, s // LANES, NSA_KV_WIDTH, LANES), per_b4)],
        out_specs=pl.BlockSpec((None, tq, NSA_WIDTH), lambda bi, i: (bi, i, 0)),
        out_shape=jax.ShapeDtypeStruct((b, s, NSA_WIDTH), BF16),
        compiler_params=_cp(("parallel", "arbitrary")),
        name="nsa_attention",
    )(q_t, g_t, kc, vc_t, m_t, ksa, vs_t, kwh, vw_t)


def _softplus(x):
    return jnp.maximum(x, 0.0) + jnp.log(1.0 + jnp.exp(-jnp.abs(x)))


def _split3(x):
    h1 = x.astype(BF16)
    r1 = x - h1.astype(F32)
    h2 = r1.astype(BF16)
    h3 = (r1 - h2.astype(F32)).astype(BF16)
    return h1, h2, h3


def _seg_sum(x, bd):
    cols = []
    for j in range(x.shape[1] // LANES):
        xs = x[:, LANES * j:LANES * (j + 1)]
        h1, h2, h3 = _split3(xs)
        cols.append(_dot(h1, bd) + _dot(h2, bd) + _dot(h3, bd))
    return jnp.concatenate(cols, axis=1)


def _rwkv_kernel(*refs, has_vres):
    it = iter(refs)
    r_ref, k_ref, v_ref, sm_ref = next(it), next(it), next(it), next(it)
    vf_ref = next(it) if has_vres else None
    mu_ref, musm_ref, w0_ref, w2_ref, a0_ref, a2_ref, g2_ref = (next(it) for _ in range(7))
    kk_ref, ka_ref, rk_ref, lng_ref, lnb_ref = (next(it) for _ in range(5))
    if has_vres:
        v0_ref, v1_ref, v2_ref = next(it), next(it), next(it)
    tri_ref, bd_ref = next(it), next(it)
    y_ref, vo_ref = next(it), next(it)
    z_ref, prev_ref, prevsm_ref = next(it), next(it), next(it)

    c = r_ref.shape[0]
    width = r_ref.shape[1]

    @pl.when(pl.program_id(1) == 0)
    def _():
        z_ref[...] = jnp.zeros(z_ref.shape, F32)
        prev_ref[...] = jnp.zeros(prev_ref.shape, F32)
        prevsm_ref[...] = jnp.zeros(prevsm_ref.shape, F32)

    row = lax.broadcasted_iota(I32, (c, 1), 0)

    def shift_mix(x, pref, idx, mu):
        prev = pref[idx, 0:1, :]
        xs = jnp.where(row == 0, prev, pltpu.roll(x, 1, 0))
        pref[idx, 0:1, :] = x[c - 1:c, :]
        return x + (xs - x) * mu

    xr = shift_mix(r_ref[...].astype(F32), prev_ref, 0, mu_ref[0:1, :])
    xk = shift_mix(k_ref[...].astype(F32), prev_ref, 1, mu_ref[1:2, :])
    xv = shift_mix(v_ref[...].astype(F32), prev_ref, 2, mu_ref[2:3, :])
    sm = shift_mix(sm_ref[...].astype(F32), prevsm_ref, 0, musm_ref[...])

    wpre = w0_ref[...] + _dot(jnp.tanh(sm).astype(BF16), w2_ref[...])
    lw = -jnp.exp(-_softplus(-wpre) - 0.5)
    a = jax.nn.sigmoid(a0_ref[...] + _dot(sm.astype(BF16), a2_ref[...]))
    gate = _dot(jax.nn.sigmoid(sm).astype(BF16), g2_ref[...])
    if has_vres:
        lo = _dot(_dot(xv.astype(BF16), v1_ref[...]).astype(BF16), v2_ref[...])
        xv = xv + (vf_ref[...].astype(F32) - xv) * jax.nn.sigmoid(v0_ref[...] + lo)
    vo_ref[...] = xv.astype(BF16)

    bd = bd_ref[...]
    kk = xk * kk_ref[...]
    kk = kk / jnp.maximum(jnp.sqrt(_seg_sum(kk * kk, bd)), 1e-12)
    kmod = xk * (1.0 + (a - 1.0) * ka_ref[...])
    ah = -kk
    bh = kk * a

    tri = tri_ref[...]
    l1, l2, l3 = _split3(lw)
    cum = _dot(tri, l1) + _dot(tri, l2) + _dot(tri, l3)
    cumx = cum - lw
    bonus_in = xr * kmod * rk_ref[...]
    cc = RW_CHUNK
    chunks = range(c // cc)
    rss = [slice(cc * n, cc * (n + 1)) for n in chunks]
    a_t, r_t, b_t, k_t, a_0, r_0, b_e, k_e, w_end = ([] for _ in range(9))
    for rs in rss:
        cu, cx = cum[rs], cumx[rs]
        cmid = cu[cc // 2 - 1:cc // 2, :]
        cend = cu[cc - 1:cc, :]
        e_inv = jnp.exp(cmid - cu)
        e_end = jnp.exp(cend - cu)
        a_t.append((ah[rs] * jnp.exp(cx - cmid)).astype(BF16))
        r_t.append((xr[rs] * jnp.exp(cu - cmid)).astype(BF16))
        b_t.append((bh[rs] * e_inv).astype(BF16))
        k_t.append((kmod[rs] * e_inv).astype(BF16))
        a_0.append((ah[rs] * jnp.exp(cx)).astype(BF16))
        r_0.append((xr[rs] * jnp.exp(cu)).astype(BF16))
        b_e.append((bh[rs] * e_end).astype(BF16))
        k_e.append((kmod[rs] * e_end).astype(BF16))
        w_end.append(jnp.exp(cend))

    ri = lax.broadcasted_iota(I32, (cc, cc), 0)
    ci = lax.broadcasted_iota(I32, (cc, cc), 1)
    strict = ci < ri
    incl = ci <= ri
    eye = jnp.where(ci == ri, 1.0, 0.0)
    n_levels = int(np.log2(cc))
    assert 2 ** n_levels == cc

    heads = range(width // RW_HEAD)
    sls = [slice(RW_HEAD * h, RW_HEAD * (h + 1)) for h in heads]
    pairs = [(n, h) for n in chunks for h in heads]
    vhs = {(n, h): xv[rss[n], sls[h]] for n, h in pairs}
    vhbs = {p: vhs[p].astype(BF16) for p in pairs}
    m1s = {(n, h): _dot_nt(jnp.concatenate([a_t[n][:, sls[h]], r_t[n][:, sls[h]]], axis=0),
                           jnp.concatenate([b_t[n][:, sls[h]], k_t[n][:, sls[h]]], axis=0))
           for n, h in pairs}
    nmat = {p: jnp.where(strict, m1s[p][0:cc, 0:cc], 0.0) for p in pairs}
    a_aks = {p: jnp.where(strict, m1s[p][0:cc, cc:2 * cc], 0.0).astype(BF16) for p in pairs}
    pqs = {p: jnp.concatenate([jnp.where(incl, m1s[p][cc:2 * cc, 0:cc], 0.0),
                               jnp.where(incl, m1s[p][cc:2 * cc, cc:2 * cc], 0.0)], axis=1).astype(BF16)
           for p in pairs}
    tinv = {p: eye + nmat[p] for p in pairs}
    npows = {p: nmat[p].astype(BF16) for p in pairs}
    for _ in range(1, n_levels):
        npows = {p: _dot(npows[p], npows[p]).astype(BF16) for p in pairs}
        tinv = {p: tinv[p] + _dot(npows[p], tinv[p].astype(BF16)) for p in pairs}
    tinvb = {p: tinv[p].astype(BF16) for p in pairs}
    bkes = {(n, h): jnp.concatenate([b_e[n][:, sls[h]], k_e[n][:, sls[h]]], axis=0) for n, h in pairs}

    zts = [z_ref[h] for h in heads]
    ys = {}
    for n in chunks:
        ztbs = [zt.astype(BF16) for zt in zts]
        xs = [_dot_nt(a_0[n][:, sls[h]], ztbs[h]) + _dot(a_aks[n, h], vhbs[n, h]) for h in heads]
        y0s = [_dot_nt(r_0[n][:, sls[h]], ztbs[h]) for h in heads]
        us = [_dot(tinvb[n, h], xs[h].astype(BF16)) for h in heads]
        uvs = [jnp.concatenate([us[h], vhs[n, h]], axis=0) for h in heads]
        for h in heads:
            ys[n, h] = y0s[h] + _dot(pqs[n, h], uvs[h].astype(BF16))
        zts = [zts[h] * w_end[n][:, sls[h]] + _dot(uvs[h].T.astype(BF16), bkes[n, h]) for h in heads]
    for h in heads:
        z_ref[h] = zts[h]
    for n, h in pairs:
        rs, sl = rss[n], sls[h]
        y = ys[n, h]
        ym = jnp.mean(y, axis=1, keepdims=True)
        yd = y - ym
        yv = jnp.mean(yd * yd, axis=1, keepdims=True)
        yn = yd * lax.rsqrt(yv + RW_LNX_EPS) * lng_ref[:, sl] + lnb_ref[:, sl]
        bonus = jnp.sum(bonus_in[rs, sl], axis=1, keepdims=True) * vhs[n, h]
        y_ref[rs, sl] = ((yn + bonus) * gate[rs, sl]).astype(BF16)


def _rwkv(proj3, vfirst, p, *, has_vres):
    b, s, _ = proj3.shape
    c = RW_TILE
    w = RW_WIDTH
    cb = lambda off: off // w
    fix = lambda bi, i: (0, 0)
    blk = lambda off: pl.BlockSpec((None, c, w), lambda bi, i: (bi, i, cb(off)))
    row_w = pl.BlockSpec((1, w), fix)
    in_specs = [blk(OFF_R), blk(OFF_K), blk(OFF_V),
                pl.BlockSpec((None, c, SMALL_W), lambda bi, i: (bi, i, OFF_SMALL // SMALL_W))]
    args = [proj3, proj3, proj3, proj3]
    if has_vres:
        in_specs.append(pl.BlockSpec((None, c, w), lambda bi, i: (bi, i, 0)))
        args.append(vfirst)
    in_specs += [pl.BlockSpec((3, w), fix), pl.BlockSpec((1, SMALL_W), fix), row_w,
                 pl.BlockSpec((SMALL_W, w), fix), row_w, pl.BlockSpec((SMALL_W, w), fix),
                 pl.BlockSpec((SMALL_W, w), fix), row_w, row_w, row_w, row_w, row_w]
    args += [p["mu"], p["mu_sm"], p["w0"], p["w2"], p["a0"], p["a2"], p["g2"],
             p["k_k"], p["k_a"], p["r_k"], p["lnx_g"], p["lnx_b"]]
    if has_vres:
        in_specs += [row_w, pl.BlockSpec((w, LANES), fix), pl.BlockSpec((LANES, w), fix)]
        args += [p["v0"], p["v1"], p["v2"]]
    assert c % RW_CHUNK == 0 and s % c == 0
    pos = np.arange(c)
    tri = jnp.asarray(((pos[None, :] <= pos[:, None])
                       & (pos[None, :] // RW_CHUNK == pos[:, None] // RW_CHUNK)).astype(np.float32), BF16)
    lane_head = np.arange(LANES) // RW_HEAD
    bd = jnp.asarray((lane_head[:, None] == lane_head[None, :]).astype(np.float32), BF16)
    in_specs += [pl.BlockSpec((c, c), fix), pl.BlockSpec((LANES, LANES), fix)]
    args += [tri, bd]
    out_blk = pl.BlockSpec((None, c, w), lambda bi, i: (bi, i, 0))
    return pl.pallas_call(
        functools.partial(_rwkv_kernel, has_vres=has_vres),
        grid=(b, s // c),
        in_specs=in_specs,
        out_specs=[out_blk, out_blk],
        out_shape=[jax.ShapeDtypeStruct((b, s, w), BF16), jax.ShapeDtypeStruct((b, s, w), BF16)],
        scratch_shapes=[pltpu.VMEM((w // RW_HEAD, RW_HEAD, RW_HEAD), F32),
                        pltpu.VMEM((3, 8, w), F32), pltpu.VMEM((1, 8, SMALL_W), F32)],
        compiler_params=_cp(("parallel", "arbitrary")),
        name="rwkv7",
    )(*args)


POOL_HALO = 16


def _pool_kernel(cur_ref, halo_ref, w_ref, sc_ref, o_ref):
    i = pl.program_id(1)
    tp = cur_ref.shape[0]
    cur = cur_ref[...].astype(F32)
    halo = jnp.where(i > 0, halo_ref[...].astype(F32), 0.0)
    ext = jnp.concatenate([halo, cur], axis=0)
    sums = {1: ext}
    win = 1
    while win < max(POOL_WINDOWS):
        sums[2 * win] = sums[win] + pltpu.roll(sums[win], win, 0)
        win *= 2
    count = (i * tp + 1 + lax.broadcasted_iota(I32, (tp, 1), 0)).astype(F32)
    for gi, win in enumerate(POOL_WINDOWS):
        sl = slice(POOL_GROUP * gi, POOL_GROUP * (gi + 1))
        mean = sums[win][POOL_HALO:, sl] / jnp.minimum(count, float(win))
        y = _dot((mean - cur[:, sl]).astype(BF16), w_ref[gi])
        o_ref[:, sl] = (y * sc_ref[:, sl]).astype(BF16)


def _pool(proj3, pool_w, pool_scale, *, tp=256):
    b, s, _ = proj3.shape
    w = POOL_WIDTH
    cb = OFF_POOL // w
    per = tp // POOL_HALO
    return pl.pallas_call(
        _pool_kernel,
        grid=(b, s // tp),
        in_specs=[pl.BlockSpec((None, tp, w), lambda bi, i: (bi, i, cb)),
                  pl.BlockSpec((None, POOL_HALO, w), lambda bi, i: (bi, jnp.maximum(i * per - 1, 0), cb)),
                  pl.BlockSpec((len(POOL_WINDOWS), POOL_GROUP, POOL_GROUP), lambda bi, i: (0, 0, 0)),
                  pl.BlockSpec((1, w), lambda bi, i: (0, 0))],
        out_specs=pl.BlockSpec((None, tp, w), lambda bi, i: (bi, i, 0)),
        out_shape=jax.ShapeDtypeStruct((b, s, w), BF16),
        compiler_params=_cp(("parallel", "parallel")),
        name="pool",
    )(proj3, proj3, pool_w, pool_scale.reshape(1, w))


def _merge_kernel(ya_ref, yb_ref, yc_ref, ga_ref, gb_ref, gc_ref, wb_ref, o_ref):
    acc = jax.nn.sigmoid(ga_ref[...].astype(F32)) * _dot(ya_ref[...], wb_ref[0])
    acc = acc + jax.nn.sigmoid(gb_ref[...].astype(F32)) * _dot(yb_ref[...], wb_ref[1])
    acc = acc + jax.nn.sigmoid(gc_ref[...].astype(F32)) * _dot(yc_ref[...], wb_ref[2])
    o_ref[...] = acc.astype(BF16)


def _merge(ya, yb, yc, proj, wb, *, tm=256):
    t, w = ya.shape
    d = wb.shape[2]
    row = lambda i: (i, 0)
    gate = lambda k: pl.BlockSpec((tm, d), lambda i: (i, k))
    return pl.pallas_call(
        _merge_kernel,
        grid=(t // tm,),
        in_specs=[pl.BlockSpec((tm, w), row)] * 3 + [gate(0), gate(1), gate(2),
                                                      pl.BlockSpec((N_BRANCH, w, d), lambda i: (0, 0, 0))],
        out_specs=pl.BlockSpec((tm, d), row),
        out_shape=jax.ShapeDtypeStruct((t, d), BF16),
        compiler_params=_cp(("parallel",)),
        name="merge",
    )(ya, yb, yc, proj, proj, proj, wb)


def _xattn_kernel(h16_ref, h32_ref, wq_ref, mkv_ref, wo_ref, g_ref, b_ref, o32_ref, o16_ref, *, alpha):
    q = _dot(h16_ref[...], wq_ref[...])
    scale = XA_HEAD_DIM ** -0.5
    outs = []
    for hd in range(XA_HEADS):
        sl = slice(XA_HEAD_DIM * hd, XA_HEAD_DIM * (hd + 1))
        slv = slice(XA_WIDTH + XA_HEAD_DIM * hd, XA_WIDTH + XA_HEAD_DIM * (hd + 1))
        s = _dot_nt(q[:, sl].astype(BF16), mkv_ref[:, sl]) * scale
        e = jnp.exp(s - jnp.max(s, axis=1, keepdims=True))
        p = e / jnp.sum(e, axis=1, keepdims=True)
        outs.append(_dot(p.astype(BF16), mkv_ref[:, slv]))
    o = jnp.concatenate(outs, axis=1).astype(BF16)
    y = alpha * h32_ref[...] + _dot(o, wo_ref[...])
    out = _layer_norm_rows(y, g_ref[...], b_ref[...])
    o32_ref[...] = out
    o16_ref[...] = out.astype(BF16)


def _xattn(h16, h32, wq, memkv, wo, g, b, *, alpha, seq, tm=512):
    t, d = h16.shape
    per_b = seq // tm
    nm = memkv.shape[1]
    row = lambda i: (i, 0)
    fix = lambda i: (0, 0)
    return pl.pallas_call(
        functools.partial(_xattn_kernel, alpha=alpha),
        grid=(t // tm,),
        in_specs=[pl.BlockSpec((tm, d), row), pl.BlockSpec((tm, d), row),
                  pl.BlockSpec((d, XA_WIDTH), fix),
                  pl.BlockSpec((None, nm, 2 * XA_WIDTH), lambda i: (i // per_b, 0, 0)),
                  pl.BlockSpec((XA_WIDTH, d), fix), pl.BlockSpec((1, d), fix), pl.BlockSpec((1, d), fix)],
        out_specs=[pl.BlockSpec((tm, d), row), pl.BlockSpec((tm, d), row)],
        out_shape=[jax.ShapeDtypeStruct((t, d), F32), jax.ShapeDtypeStruct((t, d), BF16)],
        compiler_params=_cp(("parallel",)),
        name="xattn",
    )(h16, h32, wq, memkv, wo, g.reshape(1, d), b.reshape(1, d))


def _swiglu_step(x, wg_ref, wu_ref, wd_ref, acc_ref):
    gt = _dot(x, wg_ref[0])
    up = _dot(x, wu_ref[0])
    act = (gt * jax.nn.sigmoid(gt) * up).astype(BF16)
    acc_ref[...] += _dot(act, wd_ref[0])


def _ffn_ln_kernel(x_ref, wg_ref, wu_ref, wd_ref, res_ref, g_ref, b_ref, o32_ref, o16_ref, acc_ref, *, alpha):
    f = pl.program_id(1)

    @pl.when(f == 0)
    def _():
        acc_ref[...] = jnp.zeros(acc_ref.shape, F32)

    _swiglu_step(x_ref[...], wg_ref, wu_ref, wd_ref, acc_ref)

    @pl.when(f == pl.num_programs(1) - 1)
    def _():
        out = _layer_norm_rows(alpha * res_ref[...] + acc_ref[...], g_ref[...], b_ref[...])
        o32_ref[...] = out
        o16_ref[...] = out.astype(BF16)


def _ffn_ln(x, wg, wu, wd, res, g, b, *, alpha, tm=512, tf=512):
    n, d = x.shape
    ff = wg.shape[2]
    row = lambda i, f: (i, 0)
    fix = lambda i, f: (0, 0)
    return pl.pallas_call(
        functools.partial(_ffn_ln_kernel, alpha=alpha),
        grid=(n // tm, ff // tf),
        in_specs=[pl.BlockSpec((tm, d), row),
                  pl.BlockSpec((1, d, tf), lambda i, f: (0, 0, f)),
                  pl.BlockSpec((1, d, tf), lambda i, f: (0, 0, f)),
                  pl.BlockSpec((1, tf, d), lambda i, f: (0, f, 0)),
                  pl.BlockSpec((tm, d), row), pl.BlockSpec((1, d), fix), pl.BlockSpec((1, d), fix)],
        out_specs=[pl.BlockSpec((tm, d), row), pl.BlockSpec((tm, d), row)],
        out_shape=[jax.ShapeDtypeStruct((n, d), F32), jax.ShapeDtypeStruct((n, d), BF16)],
        scratch_shapes=[pltpu.VMEM((tm, d), F32)],
        compiler_params=_cp(("parallel", "arbitrary")),
        name="swiglu_ln",
    )(x, wg, wu, wd, res, g.reshape(1, d), b.reshape(1, d))


def _moe_ffn_kernel(be_ref, idx_ref, idx_next_ref, dst_prev_ref, dst_ref, src_ref, wg_ref, wu_ref, wd_ref,
                    y_ref, xbuf, x16, acc_ref, ybuf, sem_in, sem_out):
    i = pl.program_id(0)
    f = pl.program_id(1)
    n_blk = pl.num_programs(0)
    n_f = pl.num_programs(1)
    tm = x16.shape[0]
    per_step = tm // MOE_NF
    slot = i % 2

    def gather_copy(ids, r, s):
        return pltpu.make_async_copy(src_ref.at[pl.ds(ids[0, r], 1)], xbuf.at[s, pl.ds(r, 1)], sem_in.at[s])

    def scatter_copy(ids, r, s):
        return pltpu.make_async_copy(ybuf.at[s, pl.ds(r, 1)], y_ref.at[pl.ds(ids[0, r], 1)], sem_out.at[s])

    def for_rows(fn):
        def body(r, carry):
            fn(r)
            return carry
        lax.fori_loop(0, tm, body, 0)

    def wait_gather(s):
        for _ in range(tm):
            gather_copy(idx_ref, 0, s).wait()

    def wait_scatter(s):
        for _ in range(tm):
            scatter_copy(dst_ref, 0, s).wait()

    @pl.when((i == 0) & (f == 0))
    def _():
        for_rows(lambda r: gather_copy(idx_ref, r, 0).start())
        ybuf[1] = jnp.zeros(ybuf.shape[1:], F32)

    def start_both(r):
        gather_copy(idx_next_ref, r, 1 - slot).start()
        scatter_copy(dst_prev_ref, r, 1 - slot).start()

    @pl.when(f == 0)
    def _():
        wait_gather(slot)
        x16[...] = xbuf[slot].astype(BF16)
        acc_ref[...] = jnp.zeros(acc_ref.shape, F32)
        for r in range(MOE_NF * per_step, tm):
            start_both(r)

    active = i < be_ref[n_blk]

    @pl.when(active)
    def _():
        for k in range(per_step):
            start_both(f * per_step + k)
        _swiglu_step(x16[...], wg_ref, wu_ref, wd_ref, acc_ref)

    @pl.when(jnp.logical_not(active))
    def _():
        for k in range(per_step):
            start_both(f * per_step + k)

    @pl.when(f == n_f - 1)
    def _():
        @pl.when(i >= 1)
        def _():
            wait_scatter(slot)
        ybuf[slot] = acc_ref[...]

    @pl.when((i == n_blk - 1) & (f == n_f - 1))
    def _():
        for_rows(lambda r: scatter_copy(dst_ref, r, slot).start())
        wait_scatter(slot)
        wait_scatter(1 - slot)
        wait_gather(1 - slot)


def _moe_ffn(block_expert, tok_pad, dst_rows, src, wg, wu, wd, *, tm):
    n = tok_pad.shape[0]
    d = src.shape[1]
    ff = wg.shape[2]
    tf = ff // MOE_NF
    assert ff % MOE_NF == 0 and tf % MXU_DIM == 0
    n_blk = n // tm
    assert n_blk >= 2
    ids = tok_pad.reshape(n_blk, 1, tm)
    dst = jnp.concatenate([n + jnp.arange(tm, dtype=I32), dst_rows]).reshape(n_blk + 1, 1, tm)
    smem = lambda fn: pl.BlockSpec((None, 1, tm), fn, memory_space=pltpu.SMEM)
    ft = lambda i, f, be: jnp.where(i < be[n_blk], f, MOE_NF - 1)
    return pl.pallas_call(
        _moe_ffn_kernel,
        grid_spec=pltpu.PrefetchScalarGridSpec(
            num_scalar_prefetch=1, grid=(n_blk, MOE_NF),
            in_specs=[smem(lambda i, f, be: (i, 0, 0)),
                      smem(lambda i, f, be: (jnp.minimum(i + 1, n_blk - 1), 0, 0)),
                      smem(lambda i, f, be: (i, 0, 0)),
                      smem(lambda i, f, be: (i + 1, 0, 0)),
                      pl.BlockSpec(memory_space=pl.ANY),
                      pl.BlockSpec((1, d, tf), lambda i, f, be: (be[i], 0, ft(i, f, be))),
                      pl.BlockSpec((1, d, tf), lambda i, f, be: (be[i], 0, ft(i, f, be))),
                      pl.BlockSpec((1, tf, d), lambda i, f, be: (be[i], ft(i, f, be), 0))],
            out_specs=pl.BlockSpec(memory_space=pl.ANY),
            scratch_shapes=[pltpu.VMEM((2, tm, d), F32), pltpu.VMEM((tm, d), BF16), pltpu.VMEM((tm, d), F32),
                            pltpu.VMEM((2, tm, d), F32),
                            pltpu.SemaphoreType.DMA((2,)), pltpu.SemaphoreType.DMA((2,))]),
        out_shape=jax.ShapeDtypeStruct((n + tm, d), F32),
        compiler_params=_cp(("arbitrary", "arbitrary")),
        name="moe_swiglu",
    )(block_expert, ids, ids, dst, dst, src, wg, wu, wd)


def _router_kernel(x_ref, w_ref, o_ref):
    logits = _dot(x_ref[...], w_ref[...])
    lane = lax.broadcasted_iota(I32, logits.shape, 1)
    lane_f = lane.astype(F32)
    l1 = jnp.where(lane < N_EXPERTS, logits, VERY_NEG)
    m1 = jnp.max(l1, axis=1, keepdims=True)
    i1 = jnp.min(jnp.where(l1 == m1, lane_f, float(LANES)), axis=1, keepdims=True)
    l2 = jnp.where(lane_f == i1, VERY_NEG, l1)
    m2 = jnp.max(l2, axis=1, keepdims=True)
    i2 = jnp.min(jnp.where(l2 == m2, lane_f, float(LANES)), axis=1, keepdims=True)
    e = jnp.exp(m2 - m1)
    g1 = 1.0 / (1.0 + e)
    g2 = e / (1.0 + e)
    o_ref[...] = jnp.where(lane == 0, i1, jnp.where(lane == 1, i2, jnp.where(lane == 2, g1,
                           jnp.where(lane == 3, g2, 0.0))))


def _router(h16, wr, *, tm=512):
    t, d = h16.shape
    return pl.pallas_call(
        _router_kernel,
        grid=(t // tm,),
        in_specs=[pl.BlockSpec((tm, d), lambda i: (i, 0)), pl.BlockSpec((d, LANES), lambda i: (0, 0))],
        out_specs=pl.BlockSpec((tm, LANES), lambda i: (i, 0)),
        out_shape=jax.ShapeDtypeStruct((t, LANES), F32),
        compiler_params=_cp(("parallel",)),
        name="router",
    )(h16, wr)


def _combine_kernel(y0_ref, y1_ref, rt_ref, res_ref, g_ref, b_ref, o_ref, *, alpha):
    f = y0_ref[...] * rt_ref[:, 2:3] + y1_ref[...] * rt_ref[:, 3:4]
    o_ref[...] = _layer_norm_rows(alpha * res_ref[...] + f, g_ref[...], b_ref[...])


def _combine_ln(y, rt, res, g, b, *, alpha, tm=256):
    t, d = res.shape
    per_k = t // tm
    row = lambda i: (i, 0)
    fix = lambda i: (0, 0)
    return pl.pallas_call(
        functools.partial(_combine_kernel, alpha=alpha),
        grid=(t // tm,),
        in_specs=[pl.BlockSpec((tm, d), row), pl.BlockSpec((tm, d), lambda i: (per_k + i, 0)),
                  pl.BlockSpec((tm, LANES), row), pl.BlockSpec((tm, d), row),
                  pl.BlockSpec((1, d), fix), pl.BlockSpec((1, d), fix)],
        out_specs=pl.BlockSpec((tm, d), row),
        out_shape=jax.ShapeDtypeStruct((t, d), F32),
        compiler_params=_cp(("parallel",)),
        name="moe_combine_ln",
    )(y, y, rt, res, g.reshape(1, d), b.reshape(1, d))


def _moe(h16, h32, router, w_gate, w_up, w_down, g, b, *, alpha):
    t, d = h32.shape
    tm = MOE_TM
    wr = jnp.pad(router, ((0, 0), (0, LANES - N_EXPERTS))).astype(BF16)
    rt = _router(h16, wr)
    expert = rt[:, 0:TOP_K].astype(I32).reshape(-1)
    n_assign = t * TOP_K
    onehot = (expert[:, None] == jnp.arange(N_EXPERTS, dtype=I32)[None, :]).astype(I32)
    csum = jnp.cumsum(onehot, axis=0)
    rank = jnp.sum(csum * onehot, axis=1) - 1
    counts = csum[-1]
    padded = (counts + tm - 1) // tm * tm
    pad_end = jnp.cumsum(padded)
    pad_start = pad_end - padded
    dest = (pad_start[expert] + rank).astype(I32)
    n_pad = n_assign + N_EXPERTS * tm
    n_blocks = n_pad // tm
    block_expert = jnp.minimum(
        jnp.searchsorted(pad_end, jnp.arange(n_blocks, dtype=I32) * tm, side="right"),
        N_EXPERTS - 1).astype(I32)
    block_expert = jnp.concatenate([block_expert, (pad_end[-1:] // tm).astype(I32)])
    assign = jnp.arange(n_assign, dtype=I32)
    token = assign // TOP_K
    info = jnp.stack([token, (assign % TOP_K) * t + token], axis=1)
    info = jnp.full((n_pad, 2), -1, I32).at[dest].set(info)
    is_pad = info[:, 0] < 0
    tok_pad = jnp.where(is_pad, 0, info[:, 0])
    dst_rows = jnp.where(is_pad, n_assign + jnp.cumsum(is_pad.astype(I32)) - 1, info[:, 1]).astype(I32)
    y = _moe_ffn(block_expert, tok_pad, dst_rows, h32, w_gate, w_up, w_down, tm=tm)
    return _combine_ln(y, rt, h32, g, b, alpha=alpha)


def _perm_w_in(w, d_model):
    sizes = ((NSA_WIDTH,) + (NSA_KV_WIDTH,) * 6 + (N_BRANCH * NSA_HEADS,)
             + (RW_WIDTH, RW_WIDTH, RW_WIDTH, RW_DECAY_LORA, RW_A_LORA, RW_GATE_LORA)
             + (POOL_WIDTH,) + (d_model,) * N_BRANCH)
    offs = np.concatenate([[0], np.cumsum(sizes)])
    seg = [w[:, int(offs[i]):int(offs[i + 1])] for i in range(len(sizes))]
    (q, kc, vc, ks, vs, kw, vw, ng, r, k, v, wl, al, gl, pool, ga, gb, gc) = seg
    zeros = lambda n: jnp.zeros((w.shape[0], n), w.dtype)
    cols = [ga, gb, gc, q, r, k, v, pool, kc, vc, ks, vs, kw, vw,
            ng, zeros(SM_WL - N_BRANCH * NSA_HEADS), wl, al, gl, zeros(SMALL_W - SM_GL - RW_GATE_LORA)]
    out = jnp.concatenate(cols, axis=1).astype(BF16)
    assert out.shape[1] == NP_COLS and d_model == 2048
    return out


def _rwkv_params(l, mu, w0, w2, a0, a2, g2, k_k, k_a, r_k, lnx_g, lnx_b, v0, v1, v2):
    w = RW_WIDTH
    rowv = lambda a: a.reshape(1, w).astype(F32)
    m = mu[l]
    o = np.cumsum((0, w, w, w, RW_DECAY_LORA, RW_A_LORA, RW_GATE_LORA))
    mu_sm = jnp.zeros((1, SMALL_W), F32)
    mu_sm = mu_sm.at[0, SM_WL:SM_WL + RW_DECAY_LORA].set(m[o[3]:o[4]])
    mu_sm = mu_sm.at[0, SM_AL:SM_AL + RW_A_LORA].set(m[o[4]:o[5]])
    mu_sm = mu_sm.at[0, SM_GL:SM_GL + RW_GATE_LORA].set(m[o[5]:o[6]])

    def pad_rows(a, off):
        return jnp.zeros((SMALL_W, w), F32).at[off:off + a.shape[0]].set(a).astype(BF16)

    p = dict(mu=m[0:3 * w].reshape(3, w), mu_sm=mu_sm, w0=rowv(w0[l]), w2=pad_rows(w2[l], SM_WL),
             a0=rowv(a0[l]), a2=pad_rows(a2[l], SM_AL), g2=pad_rows(g2[l], SM_GL),
             k_k=rowv(k_k[l]), k_a=rowv(k_a[l]), r_k=rowv(r_k[l]), lnx_g=rowv(lnx_g[l]), lnx_b=rowv(lnx_b[l]))
    if l > 0:
        p["v0"] = rowv(v0[l - 1])
        p["v1"] = jnp.pad(v1[l - 1], ((0, 0), (0, LANES - RW_V_LORA))).astype(BF16)
        p["v2"] = jnp.pad(v2[l - 1], ((0, LANES - RW_V_LORA), (0, 0))).astype(BF16)
    return p


def _stride_units(x2d, b, s):
    x = x2d.reshape(b, s // CMP_STRIDE, CMP_STRIDE, NSA_KV_HEADS, HEAD_DIM)
    return x.transpose(0, 3, 1, 2, 4).reshape(b, NSA_KV_HEADS, s // CMP_STRIDE, CMP_STRIDE * HEAD_DIM)


def _token_mixer(l, h16, proj, b, s, tabs, tabs_c, prm, vfirst):
    d_model = h16.shape[1]
    t = b * s
    proj3 = proj.reshape(b, s, NP_COLS)
    c, shi, slo = tabs
    q_t, ksa, vs_t, kwh, vw_t, g_t = _nsa_prep(proj3, c, shi, slo)
    n_units = s // CMP_STRIDE
    half = CMP_STRIDE * HEAD_DIM
    kvc = proj[:, OFF_KVC:OFF_KVC + 2 * NSA_KV_WIDTH]
    cmp_out = []
    for j in range(2):
        xu = _stride_units(kvc[:, NSA_KV_WIDTH * j:NSA_KV_WIDTH * (j + 1)], b, s)
        pe = prm["nsa_cmp_pe"][l, j].reshape(2, half)
        w1 = prm["nsa_cmp_w1"][l, j].astype(BF16)
        b1 = prm["nsa_cmp_b1"][l, j].reshape(1, CMP_HIDDEN)
        w2p = jnp.pad(prm["nsa_cmp_w2"][l, j], ((0, 0), (0, LANES - HEAD_DIM))).astype(BF16)
        cmp_out.append(_compress(xu, pe, w1, b1, w2p, *tabs_c, is_key=(j == 0)))
    kc, vc_t = cmp_out
    y_a = _nsa_attention(q_t, g_t, kc, vc_t, ksa, vs_t, kwh, vw_t).reshape(t, NSA_WIDTH)
    rp = _rwkv_params(l, prm["rwkv_mu"], prm["rwkv_w0"], prm["rwkv_w2"], prm["rwkv_a0"], prm["rwkv_a2"],
                      prm["rwkv_g2"], prm["rwkv_k_k"], prm["rwkv_k_a"], prm["rwkv_r_k"], prm["rwkv_lnx_g"],
                      prm["rwkv_lnx_b"], prm["rwkv_v0"], prm["rwkv_v1"], prm["rwkv_v2"])
    y_b, v_l = _rwkv(proj3, vfirst, rp, has_vres=(l > 0))
    y_c = _pool(proj3, prm["pool_w"][l].astype(BF16), prm["pool_scale"][l])
    merged = _merge(y_a, y_b.reshape(t, RW_WIDTH), y_c.reshape(t, POOL_WIDTH), proj,
                    prm["w_branch"][l].astype(BF16))
    return merged, v_l


def kernel(x, mem, positions, w_in, nsa_cmp_pe, nsa_cmp_w1, nsa_cmp_b1, nsa_cmp_w2, rwkv_mu, rwkv_w0, rwkv_w2, rwkv_a0, rwkv_a2, rwkv_g2, rwkv_k_k, rwkv_k_a, rwkv_r_k, rwkv_lnx_g, rwkv_lnx_b, rwkv_v0, rwkv_v1, rwkv_v2, pool_w, pool_scale, w_branch, w_out, mem_ln_g, mem_ln_b, mem_wkv, xa_wq, xa_wo, ln_g, ln_b, ffn_w_gate, ffn_w_up, ffn_w_down, moe_router, moe_w_gate, moe_w_up, moe_w_down):
    prm = dict(nsa_cmp_pe=nsa_cmp_pe, nsa_cmp_w1=nsa_cmp_w1, nsa_cmp_b1=nsa_cmp_b1, nsa_cmp_w2=nsa_cmp_w2,
               rwkv_mu=rwkv_mu, rwkv_w0=rwkv_w0, rwkv_w2=rwkv_w2, rwkv_a0=rwkv_a0, rwkv_a2=rwkv_a2,
               rwkv_g2=rwkv_g2, rwkv_k_k=rwkv_k_k, rwkv_k_a=rwkv_k_a, rwkv_r_k=rwkv_r_k,
               rwkv_lnx_g=rwkv_lnx_g, rwkv_lnx_b=rwkv_lnx_b, rwkv_v0=rwkv_v0, rwkv_v1=rwkv_v1,
               rwkv_v2=rwkv_v2, pool_w=pool_w, pool_scale=pool_scale, w_branch=w_branch)
    b, s, d = x.shape
    t = b * s
    depth = w_in.shape[0]
    alpha = (2 * depth) ** 0.25
    n_mem = mem.shape[1]

    inv_freq = ROPE_THETA ** (-jnp.arange(ROPE_HALF, dtype=F32) / ROPE_HALF)
    dd = np.arange(LANES) % HEAD_DIM
    invf_lane = jnp.where(jnp.asarray(dd < ROPE_DIM), inv_freq[jnp.asarray(dd % ROPE_HALF)], 0.0).reshape(1, LANES)
    tabs = tuple(a.reshape(b, s, LANES) for a in _rope_tables(positions.reshape(t, 1), invf_lane, ts=512))
    n_units = s // CMP_STRIDE
    pos_c = positions[:, CMP_BLOCK - 1::CMP_STRIDE]
    pos_c = jnp.concatenate([pos_c, pos_c[:, -1:]], axis=1)
    tabs_c = tuple(a.reshape(b, n_units, LANES)
                   for a in _rope_tables(pos_c.reshape(b * n_units, 1), invf_lane, ts=n_units))

    memkv = _memkv(mem.reshape(b * n_mem, d), mem_ln_g, mem_ln_b, mem_wkv.astype(BF16))
    memkv = memkv.reshape(b, n_mem, 2 * XA_WIDTH)

    h32 = x.reshape(t, d)
    h16 = h32.astype(BF16)
    vfirst = None
    for l in range(depth):
        proj = _matmul(h16, _perm_w_in(w_in[l], d), tm=1024, tn=1024, out_dtype=BF16)
        merged, v_l = _token_mixer(l, h16, proj, b, s, tabs, tabs_c, prm, vfirst)
        if l == 0:
            vfirst = v_l
        h32, h16 = _matmul_res_ln(merged, w_out[l].astype(BF16), h32, ln_g[l, 0], ln_b[l, 0], alpha=alpha)
        h32, h16 = _xattn(h16, h32, xa_wq[l].astype(BF16), memkv, xa_wo[l].astype(BF16),
                          ln_g[l, 1], ln_b[l, 1], alpha=alpha, seq=s)
        if l % 2 == 0:
            e = l // 2
            h32, h16 = _ffn_ln(h16, ffn_w_gate[e:e + 1].astype(BF16), ffn_w_up[e:e + 1].astype(BF16),
                               ffn_w_down[e:e + 1].astype(BF16), h32, ln_g[l, 2], ln_b[l, 2], alpha=alpha)
        else:
            e = l // 2
            h32 = _moe(h16, h32, moe_router[e], moe_w_gate[e].astype(BF16), moe_w_up[e].astype(BF16),
                       moe_w_down[e].astype(BF16), ln_g[l, 2], ln_b[l, 2], alpha=alpha)
            h16 = h32.astype(BF16)
    return h32.reshape(b, s, d)
```
